```python
import jax, jax.numpy as jnp
from jax import lax
import numpy as np

D_MODEL = 4096
BATCH = 1
SEQ = 8192
DEPTH = 4
DEC_BATCH = 16
DEC_SEQ = 16
PAST_LEN = 2048

CHUNK = 64
N_PREV_CHUNKS = 8
BAND_PAST = N_PREV_CHUNKS * CHUNK
N_AB_LAYERS = (DEPTH + 1) // 2
N_RWKV_LAYERS = DEPTH // 2
H_A = 8
DK_A = 256
DV_A = 256
GATE_CAP = 15.0
H_B = 16
DH_B = 128
REL_MAX = 2 * CHUNK
REL_SIZE = CHUNK + REL_MAX
W_AK = H_A * DK_A
W_AV = H_A * DV_A
W_B = H_B * DH_B
AB_SPLITS = (W_AK, 2 * W_AK, 2 * W_AK + W_AV, 2 * W_AK + 2 * W_AV, 2 * W_AK + 2 * W_AV + 2 * H_A,
             2 * W_AK + 2 * W_AV + 2 * H_A + W_B, 2 * W_AK + 2 * W_AV + 2 * H_A + 2 * W_B)
AB_IN = 2 * W_AK + 2 * W_AV + 2 * H_A + 3 * W_B
AB_MIX = W_AV + W_B
DH_C = 64
H_C = D_MODEL // DH_C
LORA_DECAY = 128
LORA_A = 128
LORA_V = 96
LORA_G = 480
GN_EPS = 64e-5
D_FF = ((8 * D_MODEL + 767) // 768) * 256
EPS = 1e-6

kernel_name = 'hybrid_mlstm_chunkband_rwkv7_stream_step'


def rmsnorm(x, g):
    xf = x.astype(jnp.float32)
    y = xf * lax.rsqrt(jnp.mean(xf * xf, axis=-1, keepdims=True) + EPS)
    return (y * g.astype(jnp.float32)).astype(x.dtype)


def soft_cap(x):
    return GATE_CAP * jnp.tanh(x / GATE_CAP)


def swiglu(h, w1, w3, w2):
    return (jax.nn.silu(h @ w1) * (h @ w3)) @ w2


def mlstm_chunkwise(q, k, v, i_pre, f_pre, C0, n0, m0):
    B, T, H, DK = q.shape
    L = min(CHUNK, T)
    nc = T // L
    f32 = jnp.float32
    q = q.astype(f32)
    k = k.astype(f32) * (DK ** -0.5)
    v = v.astype(f32)
    log_i = soft_cap(i_pre.astype(f32))
    log_f = jax.nn.log_sigmoid(soft_cap(f_pre.astype(f32)))

    def chunks(a):
        return jnp.moveaxis(a.reshape((B, nc, L) + a.shape[2:]), 1, 0)

    tril = jnp.tril(jnp.ones((L, L), bool))

    def step(carry, inp):
        C, n, m = carry
        qc, kc, vc, li, lf = inp
        b = jnp.cumsum(lf, axis=1).transpose(0, 2, 1)
        li = li.transpose(0, 2, 1)
        logD = jnp.where(tril, b[..., :, None] - b[..., None, :] + li[..., None, :], -jnp.inf)
        inter = b + m[..., None]
        m_t = jnp.maximum(inter, logD.max(-1))
        P = jnp.exp(logD - m_t[..., None])
        g = jnp.exp(inter - m_t)
        Wqk = P * jnp.einsum('blhd,bshd->bhls', qc, kc)
        num = jnp.einsum('bhls,bshv->bhlv', Wqk, vc) + g[..., None] * jnp.einsum('blhd,bhdv->bhlv', qc, C)
        den = Wqk.sum(-1) + g * jnp.einsum('blhd,bhd->bhl', qc, n)
        h = num / jnp.maximum(jnp.abs(den), jnp.exp(-m_t))[..., None]
        m_new = m_t[..., -1]
        wk = jnp.exp(b[..., -1:] - b + li - m_new[..., None])
        decay = jnp.exp(b[..., -1] + m - m_new)
        C = decay[..., None, None] * C + jnp.einsum('bhs,bshd,bshv->bhdv', wk, kc, vc)
        n = decay[..., None] * n + jnp.einsum('bhs,bshd->bhd', wk, kc)
        return (C, n, m_new), h.transpose(0, 2, 1, 3)

    xs = (chunks(q), chunks(k), chunks(v), chunks(log_i), chunks(log_f))
    (C, n, m), hs = lax.scan(step, (C0.astype(f32), n0.astype(f32), m0.astype(f32)), xs)
    h = jnp.moveaxis(hs, 0, 1).reshape(B, T, H, -1)
    return h, C, n, m


def band_attend(q, k, v, q_pos, k_pos, valid, rel_bias):
    rel = jnp.clip(q_pos[:, None] - k_pos[None, :], -(CHUNK - 1), REL_MAX) + (CHUNK - 1)
    s = jnp.einsum('bqhd,bkhd->bhqk', q, k).astype(jnp.float32) * (DH_B ** -0.5)
    s = s + rel_bias.astype(jnp.float32)[:, rel][None]
    s = jnp.where(valid[None, None, None, :], s, -jnp.inf)
    p = jax.nn.softmax(s, axis=-1).astype(v.dtype)
    return jnp.einsum('bhqk,bkhd->bqhd', p, v)


def chunk_band_prompt(q, k, v, rel_bias):
    B, T, H, Dh = q.shape
    nc = T // CHUNK
    band = BAND_PAST + CHUNK
    pad = ((0, 0), (BAND_PAST, 0), (0, 0), (0, 0))
    kp = jnp.pad(k, pad)
    vp = jnp.pad(v, pad)

    def one(c):
        start = c * CHUNK
        qc = lax.dynamic_slice_in_dim(q, start, CHUNK, axis=1)
        kc = lax.dynamic_slice_in_dim(kp, start, band, axis=1)
        vc = lax.dynamic_slice_in_dim(vp, start, band, axis=1)
        q_pos = start + jnp.arange(CHUNK)
        k_pos = start - BAND_PAST + jnp.arange(band)
        return band_attend(qc, kc, vc, q_pos, k_pos, k_pos >= 0, rel_bias)

    out = lax.map(one, jnp.arange(nc))
    return jnp.moveaxis(out, 0, 1).reshape(B, T, H, Dh)


def chunk_band_sample(q, k, v, past_k, past_v, rel_bias):
    W = past_k.shape[1]
    T = q.shape[1]
    kc = jnp.concatenate([past_k.astype(k.dtype), k], axis=1)
    vc = jnp.concatenate([past_v.astype(v.dtype), v], axis=1)
    q_pos = PAST_LEN + jnp.arange(T)
    k_pos = PAST_LEN - W + jnp.arange(W + T)
    return band_attend(q, kc, vc, q_pos, k_pos, jnp.ones((W + T,), bool), rel_bias)


def mixer_ab(h, w_in, b_if, a_norm_g, rel_bias, w_out, C0, n0, m0, past_k, past_v):
    B, T, _ = h.shape
    qa, ka, va, oa, gif, qb, kb, vb = jnp.split(h @ w_in, AB_SPLITS, axis=-1)
    gif = gif + b_if
    ha, C, n, m = mlstm_chunkwise(qa.reshape(B, T, H_A, DK_A), ka.reshape(B, T, H_A, DK_A),
                                  va.reshape(B, T, H_A, DV_A), gif[..., :H_A], gif[..., H_A:], C0, n0, m0)
    ha = ha * lax.rsqrt(jnp.mean(ha * ha, axis=-1, keepdims=True) + EPS)
    ha = (ha.reshape(B, T, W_AV) * a_norm_g.astype(jnp.float32)).astype(h.dtype)
    ha = jax.nn.sigmoid(oa) * ha
    qb = qb.reshape(B, T, H_B, DH_B)
    kb = kb.reshape(B, T, H_B, DH_B)
    vb = vb.reshape(B, T, H_B, DH_B)
    if past_k is None:
        hb = chunk_band_prompt(qb, kb, vb, rel_bias)
        keep = min(BAND_PAST, T)
        new_k, new_v = kb[:, T - keep:], vb[:, T - keep:]
    else:
        hb = chunk_band_sample(qb, kb, vb, past_k, past_v, rel_bias)
        new_k, new_v = kb, vb
    out = jnp.concatenate([ha, hb.reshape(B, T, W_B)], axis=-1) @ w_out
    return out, C, n, m, new_k, new_v


def rwkv7_time_mix(x, shift0, S0, mu, wr, wk, wv, wo, w0, w1, w2, a0, a1, a2, g1, g2,
                   k_k, k_a, r_k, lnx_g, lnx_b, v_first, vmix):
    B, T, D = x.shape
    f32 = jnp.float32
    xx = jnp.concatenate([shift0[:, None, :].astype(x.dtype), x[:, :-1]], axis=1) - x
    xr, xw, xk, xv, xa, xg = (x + xx * mu[i] for i in range(6))
    r = xr @ wr
    k = xk @ wk
    v = xv @ wv
    w_log = -jax.nn.softplus(-(w0 + jnp.tanh(xw @ w1) @ w2).astype(f32)) - 0.5
    if vmix is not None:
        v0, v1, v2 = vmix
        v = v + (v_first - v) * jax.nn.sigmoid(v0 + (xv @ v1) @ v2)
    a = jax.nn.sigmoid((a0 + (xa @ a1) @ a2).astype(f32))
    g = jax.nn.sigmoid(xg @ g1) @ g2

    def heads(t):
        return t.astype(f32).reshape(B, T, H_C, DH_C)

    kk = heads(k.astype(f32) * k_k.astype(f32))
    kk = kk / jnp.maximum(jnp.sqrt(jnp.sum(kk * kk, axis=-1, keepdims=True)), 1e-12)
    k2 = heads(k.astype(f32) * (1 + (a - 1) * k_a.astype(f32)))
    rh, vh, ah = heads(r), heads(v), heads(a)
    decay = heads(jnp.exp(-jnp.exp(w_log)))

    def tm(t):
        return jnp.moveaxis(t, 1, 0)

    def step(S, inp):
        r_t, w_t, k_t, v_t, a_t, b_t = inp
        sa = jnp.einsum('bhvk,bhk->bhv', S, a_t)
        S = S * w_t[:, :, None, :] + sa[..., None] * b_t[:, :, None, :] + v_t[..., None] * k_t[:, :, None, :]
        return S, jnp.einsum('bhvk,bhk->bhv', S, r_t)

    S, ys = lax.scan(step, S0.astype(f32), (tm(rh), tm(decay), tm(k2), tm(vh), tm(-kk), tm(kk * ah)))
    y = jnp.moveaxis(ys, 0, 1)
    ym = jnp.mean(y, axis=-1, keepdims=True)
    yc = y - ym
    yn = yc * lax.rsqrt(jnp.mean(yc * yc, axis=-1, keepdims=True) + GN_EPS)
    yn = yn.reshape(B, T, D) * lnx_g.astype(f32) + lnx_b.astype(f32)
    bonus = (jnp.sum(rh * k2 * r_k.astype(f32), axis=-1, keepdims=True) * vh).reshape(B, T, D)
    out = ((yn + bonus).astype(x.dtype) * g) @ wo
    return out, v, S, x[:, -1, :]


def trunk(x, c, P, st):
    B = x.shape[0]
    f32 = jnp.float32
    cs = jax.nn.silu(c)
    a_C, a_n, a_m, b_k, b_v, c_S, c_sh = [], [], [], [], [], [], []
    v_first = None
    for l in range(DEPTH):
        mod = (cs @ P['ada_w'][l] + P['ada_b'][l])[:, None, :]
        sh1, sc1, gt1, sh2, sc2, gt2 = jnp.split(mod, 6, axis=-1)
        h = rmsnorm(x, P['norm1_g'][l]) * (1 + sc1) + sh1
        if l % 2 == 0:
            e = l // 2
            if st is None:
                C0 = jnp.zeros((B, H_A, DK_A, DV_A), f32)
                n0 = jnp.zeros((B, H_A, DK_A), f32)
                m0 = jnp.zeros((B, H_A), f32)
                pk, pv = None, None
            else:
                C0, n0, m0 = st['a_C'][e], st['a_n'][e], st['a_m'][e]
                pk, pv = st['b_k'][e], st['b_v'][e]
            mix, C, n, m, nk, nv = mixer_ab(h, P['ab_w_in'][e], P['ab_b_if'][e], P['a_norm_g'][e],
                                            P['b_rel_bias'][e], P['ab_w_out'][e], C0, n0, m0, pk, pv)
            a_C.append(C)
            a_n.append(n)
            a_m.append(m)
            b_k.append(nk)
            b_v.append(nv)
        else:
            o = l // 2
            if st is None:
                S0 = jnp.zeros((B, H_C, DH_C, DH_C), f32)
                shift0 = jnp.zeros((B, D_MODEL), x.dtype)
            else:
                S0, shift0 = st['c_S'][o], st['c_shift'][o]
            vmix = None if v_first is None else (P['c_v0'][o - 1], P['c_v1'][o - 1], P['c_v2'][o - 1])
            mix, v, S, shift = rwkv7_time_mix(
                h, shift0, S0, P['c_mu'][o], P['c_wr'][o], P['c_wk'][o], P['c_wv'][o], P['c_wo'][o],
                P['c_w0'][o], P['c_w1'][o], P['c_w2'][o], P['c_a0'][o], P['c_a1'][o], P['c_a2'][o],
                P['c_g1'][o], P['c_g2'][o], P['c_k_k'][o], P['c_k_a'][o], P['c_r_k'][o],
                P['c_lnx_g'][o], P['c_lnx_b'][o], v_first, vmix)
            if v_first is None:
                v_first = v
            c_S.append(S)
            c_sh.append(shift)
        x = x + gt1 * mix
        h = rmsnorm(x, P['norm2_g'][l]) * (1 + sc2) + sh2
        x = x + gt2 * swiglu(h, P['ffn_w1'][l], P['ffn_w3'][l], P['ffn_w2'][l])
    y = rmsnorm(x, P['final_g'])
    return (y, jnp.stack(a_C), jnp.stack(a_n), jnp.stack(a_m), jnp.stack(b_k), jnp.stack(b_v),
            jnp.stack(c_S), jnp.stack(c_sh))


def setup_inputs(seed: int = 0) -> dict:
    key = jax.random.key(seed)
    ks = iter(jax.random.split(key, 64))
    f32 = jnp.float32

    def nrm(shape, scale=1.0):
        return jax.random.normal(next(ks), shape, f32) * scale

    D = D_MODEL
    NA = N_AB_LAYERS
    NR = N_RWKV_LAYERS
    NRV = max(NR - 1, 0)
    W_CACHE = min(BAND_PAST, PAST_LEN)
    return {
        'x_prompt': nrm((BATCH, SEQ, D)),
        'x_sample': nrm((DEC_BATCH, DEC_SEQ, D)),
        'c_prompt': nrm((BATCH, D)),
        'c_sample': nrm((DEC_BATCH, D)),
        'state_a_C': nrm((NA, DEC_BATCH, H_A, DK_A, DV_A), 0.3),
        'state_a_n': nrm((NA, DEC_BATCH, H_A, DK_A), 0.1),
        'state_a_m': nrm((NA, DEC_BATCH, H_A)),
        'cache_b_k': nrm((NA, DEC_BATCH, W_CACHE, H_B, DH_B)),
        'cache_b_v': nrm((NA, DEC_BATCH, W_CACHE, H_B, DH_B)),
        'state_c_S': nrm((NR, DEC_BATCH, H_C, DH_C, DH_C), 0.1),
        'state_c_shift': nrm((NR, DEC_BATCH, D)),
        'ada_w': nrm((DEPTH, D, 6 * D), 0.5 * D ** -0.5),
        'ada_b': nrm((DEPTH, 6 * D), 0.02),
        'norm1_g': 1.0 + nrm((DEPTH, D), 0.02),
        'norm2_g': 1.0 + nrm((DEPTH, D), 0.02),
        'final_g': 1.0 + nrm((D,), 0.02),
        'ab_w_in': nrm((NA, D, AB_IN), D ** -0.5),
        'ab_b_if': jnp.concatenate([nrm((NA, H_A), 0.1), 3.0 + nrm((NA, H_A), 0.5)], axis=-1),
        'a_norm_g': 1.0 + nrm((NA, W_AV), 0.02),
        'b_rel_bias': nrm((NA, H_B, REL_SIZE), 0.1),
        'ab_w_out': nrm((NA, AB_MIX, D), AB_MIX ** -0.5),
        'c_mu': jax.random.uniform(next(ks), (NR, 6, D), f32, 0.0, 1.0),
        'c_wr': nrm((NR, D, D), D ** -0.5),
        'c_wk': nrm((NR, D, D), D ** -0.5),
        'c_wv': nrm((NR, D, D), D ** -0.5),
        'c_wo': nrm((NR, D, D), D ** -0.5),
        'c_w0': nrm((NR, D), 0.5) - 0.5,
        'c_w1': nrm((NR, D, LORA_DECAY), D ** -0.5),
        'c_w2': nrm((NR, LORA_DECAY, D), 0.3 * LORA_DECAY ** -0.5),
        'c_a0': nrm((NR, D), 0.1),
        'c_a1': nrm((NR, D, LORA_A), D ** -0.5),
        'c_a2': nrm((NR, LORA_A, D), 0.3 * LORA_A ** -0.5),
        'c_v0': nrm((NRV, D), 0.1),
        'c_v1': nrm((NRV, D, LORA_V), D ** -0.5),
        'c_v2': nrm((NRV, LORA_V, D), 0.3 * LORA_V ** -0.5),
        'c_g1': nrm((NR, D, LORA_G), D ** -0.5),
        'c_g2': nrm((NR, LORA_G, D), LORA_G ** -0.5),
        'c_k_k': 0.85 + nrm((NR, D), 0.05),
        'c_k_a': 1.0 + nrm((NR, D), 0.05),
        'c_r_k': nrm((NR, H_C, DH_C), 0.1),
        'c_lnx_g': 1.0 + nrm((NR, D), 0.02),
        'c_lnx_b': nrm((NR, D), 0.02),
        'ffn_w1': nrm((DEPTH, D, D_FF), D ** -0.5),
        'ffn_w3': nrm((DEPTH, D, D_FF), D ** -0.5),
        'ffn_w2': nrm((DEPTH, D_FF, D), D_FF ** -0.5),
    }


def reference(x_prompt, x_sample, c_prompt, c_sample, state_a_C, state_a_n, state_a_m, cache_b_k, cache_b_v,
              state_c_S, state_c_shift, ada_w, ada_b, norm1_g, norm2_g, final_g, ab_w_in, ab_b_if, a_norm_g,
              b_rel_bias, ab_w_out, c_mu, c_wr, c_wk, c_wv, c_wo, c_w0, c_w1, c_w2, c_a0, c_a1, c_a2,
              c_v0, c_v1, c_v2, c_g1, c_g2, c_k_k, c_k_a, c_r_k, c_lnx_g, c_lnx_b, ffn_w1, ffn_w3, ffn_w2):
    P = dict(ada_w=ada_w, ada_b=ada_b, norm1_g=norm1_g, norm2_g=norm2_g, final_g=final_g,
             ab_w_in=ab_w_in, ab_b_if=ab_b_if, a_norm_g=a_norm_g, b_rel_bias=b_rel_bias, ab_w_out=ab_w_out,
             c_mu=c_mu, c_wr=c_wr, c_wk=c_wk, c_wv=c_wv, c_wo=c_wo, c_w0=c_w0, c_w1=c_w1, c_w2=c_w2,
             c_a0=c_a0, c_a1=c_a1, c_a2=c_a2, c_v0=c_v0, c_v1=c_v1, c_v2=c_v2, c_g1=c_g1, c_g2=c_g2,
             c_k_k=c_k_k, c_k_a=c_k_a, c_r_k=c_r_k, c_lnx_g=c_lnx_g, c_lnx_b=c_lnx_b,
             ffn_w1=ffn_w1, ffn_w3=ffn_w3, ffn_w2=ffn_w2)
    st = dict(a_C=state_a_C, a_n=state_a_n, a_m=state_a_m, b_k=cache_b_k, b_v=cache_b_v,
              c_S=state_c_S, c_shift=state_c_shift)
    y_prompt, p_a_C, p_a_n, p_a_m, p_b_k, p_b_v, p_c_S, p_c_shift = trunk(x_prompt, c_prompt, P, None)
    y_sample, s_a_C, s_a_n, s_a_m, s_b_k, s_b_v, s_c_S, s_c_shift = trunk(x_sample, c_sample, P, st)
    return (y_prompt, y_sample, p_a_C, p_a_n, p_a_m, p_b_k, p_b_v, p_c_S, p_c_shift,
            s_a_C, s_a_n, s_a_m, s_b_k, s_b_v, s_c_S, s_c_shift)
```

```python
import functools

import numpy as np
import jax
import jax.numpy as jnp
from jax import lax
from jax.experimental import pallas as pl
from jax.experimental.pallas import tpu as pltpu

F32 = jnp.float32
BF16 = jnp.bfloat16

CHUNK = 64
N_PREV_CHUNKS = 8
BAND_PAST = N_PREV_CHUNKS * CHUNK
PAST_LEN = 2048
REL_MAX = 2 * CHUNK
GATE_CAP = 15.0
EPS = 1e-6
GN_EPS = 64e-5

V7X_VMEM_BYTES = 64 * 1024 * 1024
VMEM_LIMIT = V7X_VMEM_BYTES - 8 * 1024 * 1024
LANES = 128
MATMUL_ROW_SUBTILE = 256


def _params(*sem):
    return pltpu.CompilerParams(dimension_semantics=sem, vmem_limit_bytes=VMEM_LIMIT)


def _tile(n, pref, mult):
    if n <= pref:
        return n
    t = (pref // mult) * mult
    while t >= mult:
        if n % t == 0:
            return t
        t -= mult
    return n


def _silu(x):
    return x * jax.nn.sigmoid(x)


def _resident(block_shape, index_map):
    return pl.BlockSpec(block_shape, index_map, pipeline_mode=pl.Buffered(1))


def _ada_kernel(c_ref, w_ref, b_ref, o_ref):
    cs = _silu(c_ref[...]).astype(BF16)
    acc = jnp.dot(cs, w_ref[...].astype(BF16), preferred_element_type=F32)
    o_ref[...] = acc + b_ref[...]


def _ada_mod(c_all, ada_w, ada_b):
    depth, d, n = ada_w.shape
    rows = c_all.shape[0]
    tn = _tile(n, 512, LANES)
    return pl.pallas_call(
        _ada_kernel,
        grid=(depth, n // tn),
        in_specs=[
            pl.BlockSpec((rows, d), lambda l, j: (0, 0)),
            pl.BlockSpec((None, d, tn), lambda l, j: (l, 0, j)),
            pl.BlockSpec((None, 1, tn), lambda l, j: (l, 0, j)),
        ],
        out_specs=pl.BlockSpec((None, rows, tn), lambda l, j: (l, 0, j)),
        out_shape=jax.ShapeDtypeStruct((depth, rows, n), F32),
        compiler_params=_params("arbitrary", "arbitrary"),
        name="ada_mod",
    )(c_all, ada_w, ada_b.reshape(depth, 1, n))


def _norm_mod_value(x, g, sc, sh):
    y = x * lax.rsqrt(jnp.mean(x * x, axis=-1, keepdims=True) + EPS)
    return (y * g) * (1.0 + sc) + sh


def _norm_mod_kernel(x_ref, g_ref, sc_ref, sh_ref, o_ref):
    o_ref[0] = _norm_mod_value(x_ref[0], g_ref[...], sc_ref[0], sh_ref[0]).astype(o_ref.dtype)


def _norm_mod(x, g, sc, sh, out_dtype=BF16):
    b, t, d = x.shape
    tt = _tile(t, 256, 16)
    row = pl.BlockSpec((1, tt, d), lambda bi, i: (bi, i, 0))
    per_batch = pl.BlockSpec((1, 1, d), lambda bi, i: (bi, 0, 0))
    return pl.pallas_call(
        _norm_mod_kernel,
        grid=(b, t // tt),
        in_specs=[row, pl.BlockSpec((1, d), lambda bi, i: (0, 0)), per_batch, per_batch],
        out_specs=row,
        out_shape=jax.ShapeDtypeStruct((b, t, d), out_dtype),
        compiler_params=_params("arbitrary", "arbitrary"),
        name="norm_mod",
    )(x, g.reshape(1, d), sc, sh)


def _final_norm_kernel(x_ref, g_ref, o_ref):
    x = x_ref[0]
    o_ref[0] = (x * lax.rsqrt(jnp.mean(x * x, axis=-1, keepdims=True) + EPS)) * g_ref[...]


def _final_norm(x, g):
    b, t, d = x.shape
    tt = _tile(t, 256, 8)
    row = pl.BlockSpec((1, tt, d), lambda bi, i: (bi, i, 0))
    return pl.pallas_call(
        _final_norm_kernel,
        grid=(b, t // tt),
        in_specs=[row, pl.BlockSpec((1, d), lambda bi, i: (0, 0))],
        out_specs=row,
        out_shape=jax.ShapeDtypeStruct((b, t, d), F32),
        compiler_params=_params("arbitrary", "arbitrary"),
        name="final_norm",
    )(x, g.reshape(1, d))


def _norm_mix_kernel(x_ref, xp_ref, s0_ref, g_ref, sc_ref, sh_ref, mu_ref, *out_refs):
    i = pl.program_id(1)
    mix_refs, last_ref = out_refs[:6], out_refs[6]
    g, sc, sh = g_ref[...], sc_ref[0], sh_ref[0]
    h = _norm_mod_value(x_ref[0], g, sc, sh)
    hp = _norm_mod_value(xp_ref[0], g, sc, sh)
    prev_row = jnp.where(i == 0, s0_ref[0], hp[7:8, :])
    row_id = lax.broadcasted_iota(jnp.int32, h.shape, 0)
    shifted = jnp.where(row_id == 0, prev_row, pltpu.roll(h, 1, axis=0))
    xx = shifted - h
    for j in range(6):
        mix_refs[j][0] = (h + xx * mu_ref[j:j + 1, :]).astype(BF16)
    last_ref[0] = h[h.shape[0] - 8:, :]


def _norm_mix(x, shift0, g, sc, sh, mu):
    b, t, d = x.shape
    tt = _tile(t, 256, 16)
    row = pl.BlockSpec((1, tt, d), lambda bi, i: (bi, i, 0))
    prev8 = pl.BlockSpec((1, 8, d), lambda bi, i: (bi, jnp.maximum(i * (tt // 8) - 1, 0), 0))
    per_batch = pl.BlockSpec((1, 1, d), lambda bi, i: (bi, 0, 0))
    outs = pl.pallas_call(
        _norm_mix_kernel,
        grid=(b, t // tt),
        in_specs=[row, prev8, per_batch, pl.BlockSpec((1, d), lambda bi, i: (0, 0)), per_batch, per_batch,
                  pl.BlockSpec((6, d), lambda bi, i: (0, 0))],
        out_specs=[row] * 6 + [pl.BlockSpec((1, 8, d), lambda bi, i: (bi, 0, 0))],
        out_shape=[jax.ShapeDtypeStruct((b, t, d), BF16)] * 6 + [jax.ShapeDtypeStruct((b, 8, d), F32)],
        compiler_params=_params("arbitrary", "arbitrary"),
        name="norm_mix",
    )(x, x, shift0.reshape(b, 1, d), g.reshape(1, d), sc, sh, mu)
    return outs[:6], outs[6][:, 7, :]


def _mm_rows(tm):
    sub = MATMUL_ROW_SUBTILE if tm % MATMUL_ROW_SUBTILE == 0 else tm
    return sub, tm // sub


def _mm_kernel(a_ref, w_ref, o_ref, wb_ref, *, tm):
    wb_ref[...] = w_ref[...].astype(BF16)
    sub, n_sub = _mm_rows(tm)

    def body(m, carry):
        rows = pl.ds(pl.multiple_of(m * sub, sub), sub)
        acc = jnp.dot(a_ref[rows, :], wb_ref[...], preferred_element_type=F32)
        o_ref[rows, :] = acc.astype(o_ref.dtype)
        return carry

    lax.fori_loop(0, n_sub, body, 0)


def _mm_resid_kernel(a_ref, w_ref, r_ref, g_ref, o_ref, wb_ref, *, tm):
    wb_ref[...] = w_ref[...].astype(BF16)
    sub, n_sub = _mm_rows(tm)
    per_row_gate = g_ref.shape[0] != 1

    def body(m, carry):
        rows = pl.ds(pl.multiple_of(m * sub, sub), sub)
        acc = jnp.dot(a_ref[rows, :], wb_ref[...], preferred_element_type=F32)
        gate = g_ref[rows, :] if per_row_gate else g_ref[...]
        o_ref[rows, :] = r_ref[rows, :] + gate * acc
        return carry

    lax.fori_loop(0, n_sub, body, 0)


def _mm_swiglu_kernel(a_ref, w1_ref, w3_ref, o_ref, w1b_ref, w3b_ref, *, tm):
    w1b_ref[...] = w1_ref[...].astype(BF16)
    w3b_ref[...] = w3_ref[...].astype(BF16)
    sub, n_sub = _mm_rows(tm)

    def body(m, carry):
        rows = pl.ds(pl.multiple_of(m * sub, sub), sub)
        a = a_ref[rows, :]
        u = jnp.dot(a, w1b_ref[...], preferred_element_type=F32)
        v = jnp.dot(a, w3b_ref[...], preferred_element_type=F32)
        o_ref[rows, :] = (_silu(u) * v).astype(o_ref.dtype)
        return carry

    lax.fori_loop(0, n_sub, body, 0)


def _weight_spec(w, layer, k, tn):
    if w.ndim == 3:
        return pl.BlockSpec((None, k, tn), lambda i, j: (layer, 0, j))
    return pl.BlockSpec((k, tn), lambda i, j: (0, j))


def _matmul(a, w, *, layer=0, n_cols=None, out_dtype=F32, tm_pref=2048, tn_pref=512):
    m, k = a.shape
    n = w.shape[-1] if n_cols is None else n_cols
    tm = _tile(m, tm_pref, MATMUL_ROW_SUBTILE)
    tn = _tile(n, tn_pref, LANES)
    return pl.pallas_call(
        functools.partial(_mm_kernel, tm=tm),
        grid=(m // tm, n // tn),
        in_specs=[_resident((tm, k), lambda i, j: (i, 0)), _weight_spec(w, layer, k, tn)],
        out_specs=pl.BlockSpec((tm, tn), lambda i, j: (i, j)),
        out_shape=jax.ShapeDtypeStruct((m, n), out_dtype),
        scratch_shapes=[pltpu.VMEM((k, tn), BF16)],
        compiler_params=_params("arbitrary", "arbitrary"),
        name="matmul",
    )(a, w)


def _matmul_resid(a, w, res, gate, *, layer=0, tm_pref=2048, tn_pref=512):
    m, k = a.shape
    n = w.shape[-1]
    tm = _tile(m, tm_pref, MATMUL_ROW_SUBTILE)
    tn = _tile(n, tn_pref, LANES)
    if gate.shape[0] == 1:
        gate_spec = pl.BlockSpec((1, tn), lambda i, j: (0, j))
    else:
        gate_spec = pl.BlockSpec((tm, tn), lambda i, j: (i, j))
    return pl.pallas_call(
        functools.partial(_mm_resid_kernel, tm=tm),
        grid=(m // tm, n // tn),
        in_specs=[_resident((tm, k), lambda i, j: (i, 0)), _weight_spec(w, layer, k, tn),
                  pl.BlockSpec((tm, tn), lambda i, j: (i, j)), gate_spec],
        out_specs=pl.BlockSpec((tm, tn), lambda i, j: (i, j)),
        out_shape=jax.ShapeDtypeStruct((m, n), F32),
        scratch_shapes=[pltpu.VMEM((k, tn), BF16)],
        compiler_params=_params("arbitrary", "arbitrary"),
        name="matmul_resid",
    )(a, w, res, gate)


def _matmul_swiglu(a, w1, w3, *, layer, tm_pref=2048, tn_pref=256):
    m, k = a.shape
    n = w1.shape[-1]
    tm = _tile(m, tm_pref, MATMUL_ROW_SUBTILE)
    tn = _tile(n, tn_pref, LANES)
    return pl.pallas_call(
        functools.partial(_mm_swiglu_kernel, tm=tm),
        grid=(m // tm, n // tn),
        in_specs=[_resident((tm, k), lambda i, j: (i, 0)), _weight_spec(w1, layer, k, tn),
                  _weight_spec(w3, layer, k, tn)],
        out_specs=pl.BlockSpec((tm, tn), lambda i, j: (i, j)),
        out_shape=jax.ShapeDtypeStruct((m, n), BF16),
        scratch_shapes=[pltpu.VMEM((k, tn), BF16), pltpu.VMEM((k, tn), BF16)],
        compiler_params=_params("arbitrary", "arbitrary"),
        name="matmul_swiglu",
    )(a, w1, w3)


def _lora_kernel(a_ref, w1_ref, w2_ref, *rest, mid, epilogue):
    o_ref = rest[-1]
    z = jnp.dot(a_ref[...], w1_ref[...], preferred_element_type=F32)
    if mid == "tanh":
        z = jnp.tanh(z)
    elif mid == "sigmoid":
        z = jax.nn.sigmoid(z)
    z = jnp.dot(z.astype(BF16), w2_ref[...], preferred_element_type=F32)
    if epilogue == "log_decay":
        w_log = -jax.nn.softplus(-(rest[0][...] + z)) - 0.5
        o_ref[...] = -jnp.exp(w_log)
    elif epilogue == "sigmoid":
        o_ref[...] = jax.nn.sigmoid(rest[0][...] + z)
    elif epilogue == "vmix":
        v, vf = rest[1][...], rest[2][...]
        o_ref[...] = v + (vf - v) * jax.nn.sigmoid(rest[0][...] + z)
    else:
        o_ref[...] = z


def _lora(a, w1, w2, *, mid, epilogue, bias=None, extra=()):
    m, d = a.shape
    r = w1.shape[1]
    rp = -(-r // LANES) * LANES
    w1p = jnp.pad(w1, ((0, 0), (0, rp - r))).astype(BF16)
    w2p = jnp.pad(w2, ((0, rp - r), (0, 0))).astype(BF16)
    n = w2.shape[1]
    tm = _tile(m, 256, 16)
    row = pl.BlockSpec((tm, n), lambda i: (i, 0))
    in_specs = [pl.BlockSpec((tm, d), lambda i: (i, 0)), _resident((d, rp), lambda i: (0, 0)),
                _resident((rp, n), lambda i: (0, 0))]
    args = [a, w1p, w2p]
    if bias is not None:
        in_specs.append(pl.BlockSpec((1, n), lambda i: (0, 0)))
        args.append(bias.reshape(1, n))
    for e in extra:
        in_specs.append(row)
        args.append(e)
    return pl.pallas_call(
        functools.partial(_lora_kernel, mid=mid, epilogue=epilogue),
        grid=(m // tm,),
        in_specs=in_specs,
        out_specs=row,
        out_shape=jax.ShapeDtypeStruct((m, n), F32),
        compiler_params=_params("arbitrary"),
        name="lora_" + epilogue,
    )(*args)


def _dot_nt(a, b):
    return lax.dot_general(a, b, (((1,), (1,)), ((), ())), preferred_element_type=F32)


def _dot_tn(a, b):
    return lax.dot_general(a, b, (((0,), (0,)), ((), ())), preferred_element_type=F32)


def _soft_cap(x):
    return GATE_CAP * jnp.tanh(x / GATE_CAP)


def _mlstm_kernel(q_ref, k_ref, v_ref, o_ref, gc_ref, gr_ref, bc_ref, br_ref, ng_ref, c0_ref, n0_ref, m0_ref,
                  h_ref, c_ref, n_ref, m_ref, *, heads, dk, dv, length):
    c_idx = pl.program_id(1)

    @pl.when(c_idx == 0)
    def _():
        c_ref[...] = c0_ref[...]
        n_ref[...] = n0_ref[...]
        m_ref[...] = m0_ref[...]

    gcol = gc_ref[...] + bc_ref[...]
    grow = gr_ref[...] + br_ref[...]
    li_col, lf_col = _soft_cap(gcol[:, :heads]), jax.nn.log_sigmoid(_soft_cap(gcol[:, heads:]))
    li_row, lf_row = _soft_cap(grow[:heads, :]), jax.nn.log_sigmoid(_soft_cap(grow[heads:, :]))
    r_id = lax.broadcasted_iota(jnp.int32, (length, length), 0)
    c_id = lax.broadcasted_iota(jnp.int32, (length, length), 1)
    tril = c_id <= r_id
    scale = dk ** -0.5

    for h in range(heads):
        lf_r, li_r = lf_row[h:h + 1, :], li_row[h:h + 1, :]
        lf_c, li_c = lf_col[:, h:h + 1], li_col[:, h:h + 1]
        b_col = jnp.sum(jnp.where(tril, lf_r, 0.0), axis=1, keepdims=True)
        b_row = jnp.sum(jnp.where(r_id <= c_id, lf_c, 0.0), axis=0, keepdims=True)
        b_last = b_col[length - 1:length, :]
        m_prev = m_ref[0, h:h + 1, :]
        log_d = jnp.where(tril, b_col - b_row + li_r, -jnp.inf)
        inter = b_col + m_prev
        m_t = jnp.maximum(inter, jnp.max(log_d, axis=1, keepdims=True))
        p = jnp.exp(log_d - m_t)
        g = jnp.exp(inter - m_t)

        q = q_ref[:, h * dk:(h + 1) * dk]
        k = k_ref[:, h * dk:(h + 1) * dk] * scale
        v = v_ref[:, h * dv:(h + 1) * dv]
        qb, kb, vb = q.astype(BF16), k.astype(BF16), v.astype(BF16)
        c_state = c_ref[0, h]
        n_state = n_ref[0, h:h + 1, :]

        wqk = p * _dot_nt(qb, kb)
        num = jnp.dot(wqk.astype(BF16), vb, preferred_element_type=F32) \
            + g * jnp.dot(qb, c_state.astype(BF16), preferred_element_type=F32)
        den = jnp.sum(wqk, axis=1, keepdims=True) + g * jnp.sum(q * n_state, axis=1, keepdims=True)
        hh = num / jnp.maximum(jnp.abs(den), jnp.exp(-m_t))

        hh = hh * lax.rsqrt(jnp.mean(hh * hh, axis=1, keepdims=True) + EPS)
        hh = hh * ng_ref[:, h * dv:(h + 1) * dv]
        h_ref[:, h * dv:(h + 1) * dv] = (jax.nn.sigmoid(o_ref[:, h * dv:(h + 1) * dv]) * hh).astype(h_ref.dtype)

        m_new = m_t[length - 1:length, :]
        wk_col = jnp.exp(b_last - b_col + li_c - m_new)
        decay = jnp.exp(b_last + m_prev - m_new)
        kw = k * wk_col
        c_ref[0, h] = decay * c_state + _dot_tn(kw.astype(BF16), vb)
        n_ref[0, h:h + 1, :] = decay * n_state + jnp.sum(kw, axis=0, keepdims=True)
        m_ref[0, h:h + 1, :] = m_new


def _mlstm(proj, gif, b_if, a_norm_g, c0, n0, m0, *, batch, seq):
    _, heads, dk, dv = c0.shape
    length = min(CHUNK, seq)
    nc = seq // length
    wq, wv = heads * dk, heads * dv
    assert wq == wv
    gcol = gif
    grow = gif.reshape(batch, nc, length, 2 * heads).transpose(0, 1, 3, 2)
    row = lambda col: pl.BlockSpec((length, wq), lambda b, c: (b * nc + c, col))
    state4 = pl.BlockSpec((1, heads, dk, dv), lambda b, c: (b, 0, 0, 0))
    state3 = pl.BlockSpec((1, heads, dk), lambda b, c: (b, 0, 0))
    state_m = pl.BlockSpec((1, heads, 1), lambda b, c: (b, 0, 0))
    h, c_out, n_out, m_out = pl.pallas_call(
        functools.partial(_mlstm_kernel, heads=heads, dk=dk, dv=dv, length=length),
        grid=(batch, nc),
        in_specs=[row(0), row(1), row(2), row(3),
                  pl.BlockSpec((length, 2 * heads), lambda b, c: (b * nc + c, 0)),
                  pl.BlockSpec((None, None, 2 * heads, length), lambda b, c: (b, c, 0, 0)),
                  pl.BlockSpec((1, 2 * heads), lambda b, c: (0, 0)),
                  pl.BlockSpec((2 * heads, 1), lambda b, c: (0, 0)),
                  pl.BlockSpec((1, wv), lambda b, c: (0, 0)),
                  state4, state3, state_m],
        out_specs=[pl.BlockSpec((length, wv), lambda b, c: (b * nc + c, 0)), state4, state3, state_m],
        out_shape=[jax.ShapeDtypeStruct((batch * seq, wv), BF16),
                   jax.ShapeDtypeStruct((batch, heads, dk, dv), F32),
                   jax.ShapeDtypeStruct((batch, heads, dk), F32),
                   jax.ShapeDtypeStruct((batch, heads, 1), F32)],
        compiler_params=_params("arbitrary", "arbitrary"),
        name="mlstm",
    )(proj, proj, proj, proj, gcol, grow, b_if.reshape(1, 2 * heads), b_if.reshape(2 * heads, 1),
      a_norm_g.reshape(1, wv), c0, n0, m0.reshape(batch, heads, 1))
    return h, c_out, n_out, m_out.reshape(batch, heads)


def _rel_bias_table(rel_bias):
    i = np.arange(CHUNK)[:, None]
    j = np.arange(BAND_PAST + CHUNK)[None, :]
    rel = np.clip(i - j + BAND_PAST, -(CHUNK - 1), REL_MAX) + (CHUNK - 1)
    return rel_bias[:, rel]


def _band_chunk(q, k, v, bias, scale):
    s = _dot_nt(q, k) * scale + bias
    p = jnp.exp(s - jnp.max(s, axis=1, keepdims=True))
    o = jnp.dot(p.astype(BF16), v, preferred_element_type=F32)
    return o / jnp.sum(p, axis=1, keepdims=True)


def _attn_prompt_kernel(q_ref, k_ref, v_ref, bias_ref, o_ref, kb_ref, vb_ref, *, seq, scale):
    kb_ref[...] = k_ref[...].astype(BF16)
    vb_ref[...] = v_ref[...].astype(BF16)
    nc = seq // CHUNK
    band = BAND_PAST + CHUNK
    bias = bias_ref[0]

    for c in range(min(N_PREV_CHUNKS, nc)):
        width = (c + 1) * CHUNK
        q = q_ref[c * CHUNK:(c + 1) * CHUNK, :].astype(BF16)
        o = _band_chunk(q, kb_ref[0:width, :], vb_ref[0:width, :], bias[:, band - width:], scale)
        o_ref[c * CHUNK:(c + 1) * CHUNK, :] = o.astype(o_ref.dtype)

    def body(c, carry):
        start = pl.multiple_of(c * CHUNK, CHUNK)
        k_rows = pl.ds(pl.multiple_of(start - BAND_PAST, CHUNK), band)
        q = q_ref[pl.ds(start, CHUNK), :].astype(BF16)
        o = _band_chunk(q, kb_ref[k_rows, :], vb_ref[k_rows, :], bias, scale)
        o_ref[pl.ds(start, CHUNK), :] = o.astype(o_ref.dtype)
        return carry

    if nc > N_PREV_CHUNKS:
        lax.fori_loop(N_PREV_CHUNKS, nc, body, 0)


def _attn_prompt(proj_b, bias_table, *, batch, seq, heads, dh):
    col = lambda base: pl.BlockSpec((seq, dh), lambda b, h: (b, base * heads + h))
    return pl.pallas_call(
        functools.partial(_attn_prompt_kernel, seq=seq, scale=dh ** -0.5),
        grid=(batch, heads),
        in_specs=[col(0), col(1), col(2),
                  pl.BlockSpec((1, CHUNK, BAND_PAST + CHUNK), lambda b, h: (h, 0, 0))],
        out_specs=pl.BlockSpec((seq, dh), lambda b, h: (b, h)),
        out_shape=jax.ShapeDtypeStruct((batch * seq, heads * dh), BF16),
        scratch_shapes=[pltpu.VMEM((seq, dh), BF16), pltpu.VMEM((seq, dh), BF16)],
        compiler_params=_params("arbitrary", "arbitrary"),
        name="attn_prompt",
    )(proj_b, proj_b, proj_b, bias_table)


def _attn_sample_kernel(q_ref, k_ref, v_ref, pk_ref, pv_ref, bias_ref, o_ref, *, seq, width, scale):
    q = q_ref[...].astype(BF16)
    bias = bias_ref[0]
    off = BAND_PAST - width
    s_past = _dot_nt(q, pk_ref[...].astype(BF16)) * scale + bias[:seq, off:off + width]
    s_new = _dot_nt(q, k_ref[...].astype(BF16)) * scale + bias[:seq, BAND_PAST:BAND_PAST + seq]
    m = jnp.maximum(jnp.max(s_past, axis=1, keepdims=True), jnp.max(s_new, axis=1, keepdims=True))
    p_past, p_new = jnp.exp(s_past - m), jnp.exp(s_new - m)
    o = jnp.dot(p_past.astype(BF16), pv_ref[...].astype(BF16), preferred_element_type=F32) \
        + jnp.dot(p_new.astype(BF16), v_ref[...].astype(BF16), preferred_element_type=F32)
    denom = jnp.sum(p_past, axis=1, keepdims=True) + jnp.sum(p_new, axis=1, keepdims=True)
    o_ref[...] = (o / denom).astype(o_ref.dtype)


def _attn_sample(proj_b, past_k, past_v, layer, bias_table, *, batch, seq, heads, dh):
    width = past_k.shape[2]
    col = lambda base: pl.BlockSpec((seq, dh), lambda b, h: (b, base * heads + h))
    past = pl.BlockSpec((None, None, width, dh), lambda b, h: (layer, b, 0, h))
    return pl.pallas_call(
        functools.partial(_attn_sample_kernel, seq=seq, width=width, scale=dh ** -0.5),
        grid=(batch, heads),
        in_specs=[col(0), col(1), col(2), past, past,
                  pl.BlockSpec((1, CHUNK, BAND_PAST + CHUNK), lambda b, h: (h, 0, 0))],
        out_specs=pl.BlockSpec((seq, dh), lambda b, h: (b, h)),
        out_shape=jax.ShapeDtypeStruct((batch * seq, heads * dh), BF16),
        compiler_params=_params("arbitrary", "arbitrary"),
        name="attn_sample",
    )(proj_b, proj_b, proj_b, past_k, past_v, bias_table)


def _split_dot(a_exact, x):
    hi = x.astype(BF16)
    lo = (x - hi.astype(F32)).astype(BF16)
    return jnp.dot(a_exact, hi, preferred_element_type=F32) + jnp.dot(a_exact, lo, preferred_element_type=F32)


def _rwkv_kernel(r_ref, lw_ref, k_ref, v_ref, a_ref, g_ref, kk_ref, ka_ref, rk_ref, lg_ref, lb_ref, s0_ref,
                 o_ref, s_ref, *, heads, dh, length):
    c_idx = pl.program_id(2)

    @pl.when(c_idx == 0)
    def _():
        s_ref[...] = s0_ref[...]

    r_id = lax.broadcasted_iota(jnp.int32, (length, length), 0)
    c_id = lax.broadcasted_iota(jnp.int32, (length, length), 1)
    incl = c_id <= r_id
    strict = c_id < r_id
    r_id2 = lax.broadcasted_iota(jnp.int32, (length, 2 * length), 0)
    c_id2 = lax.broadcasted_iota(jnp.int32, (length, 2 * length), 1)
    incl2 = jnp.where(c_id2 < length, c_id2, c_id2 - length) <= r_id2
    tril_bf = jnp.where(incl, 1.0, 0.0).astype(BF16)
    n_double = max(int(np.ceil(np.log2(length))), 1)

    for h in range(heads):
        cols = slice(h * dh, (h + 1) * dh)
        r, lw, k, v, a = r_ref[:, cols], lw_ref[:, cols], k_ref[:, cols], v_ref[:, cols], a_ref[:, cols]
        kk = k * kk_ref[:, cols]
        kk = kk / jnp.maximum(jnp.sqrt(jnp.sum(kk * kk, axis=1, keepdims=True)), 1e-12)
        k2 = k * (1.0 + (a - 1.0) * ka_ref[:, cols])
        cum = _split_dot(tril_bf, lw)
        w_in = jnp.exp(cum)
        w_inv = jnp.exp(-cum)
        w_ex = jnp.exp(cum - lw)
        at = (-kk) * w_ex
        bt = (kk * a) * w_inv
        kt = k2 * w_inv
        rt = r * w_in
        lhs = jnp.concatenate([at, rt], axis=0).astype(BF16)
        rhs = jnp.concatenate([bt, kt], axis=0).astype(BF16)
        s0 = s_ref[0, h]
        aa = _dot_nt(lhs, rhs)
        ls = _dot_nt(lhs, s0.astype(BF16))
        vb = v.astype(BF16)
        n_ab = jnp.where(strict, aa[:length, :length], 0.0)
        a_ak = jnp.where(strict, aa[:length, length:], 0.0)
        x = ls[:length] + jnp.dot(a_ak.astype(BF16), vb, preferred_element_type=F32)
        pw = n_ab
        for j in range(n_double):
            pwb = pw.astype(BF16)
            x = x + jnp.dot(pwb, x.astype(BF16), preferred_element_type=F32)
            if j + 1 < n_double:
                pw = jnp.dot(pwb, pwb, preferred_element_type=F32)
        uv = jnp.concatenate([x, v], axis=0).astype(BF16)
        a_r = jnp.where(incl2, aa[length:, :], 0.0)
        y = ls[length:] + jnp.dot(a_r.astype(BF16), uv, preferred_element_type=F32)
        s_ref[0, h] = (s0 + _dot_tn(uv, rhs)) * w_in[length - 1:length, :]

        yc = y - jnp.mean(y, axis=1, keepdims=True)
        yn = yc * lax.rsqrt(jnp.mean(yc * yc, axis=1, keepdims=True) + GN_EPS)
        yn = yn * lg_ref[:, cols] + lb_ref[:, cols]
        bonus = jnp.sum(r * k2 * rk_ref[:, cols], axis=1, keepdims=True) * v
        o_ref[:, cols] = ((yn + bonus) * g_ref[:, cols]).astype(o_ref.dtype)


def _rwkv(r, lw, k, v, a, g, k_k, k_a, r_k, lnx_g, lnx_b, s0, *, batch, seq):
    _, heads, dh, _ = s0.shape
    d = heads * dh
    length = min(CHUNK, seq)
    nc = seq // length
    hg = _tile(heads, 8, 2)
    wg = hg * dh
    row = pl.BlockSpec((length, wg), lambda b, gi, c: (b * nc + c, gi))
    par = pl.BlockSpec((1, wg), lambda b, gi, c: (0, gi))
    state = pl.BlockSpec((1, hg, dh, dh), lambda b, gi, c: (b, gi, 0, 0))
    return pl.pallas_call(
        functools.partial(_rwkv_kernel, heads=hg, dh=dh, length=length),
        grid=(batch, heads // hg, nc),
        in_specs=[row] * 6 + [par] * 5 + [state],
        out_specs=[row, state],
        out_shape=[jax.ShapeDtypeStruct((batch * seq, d), BF16), jax.ShapeDtypeStruct(s0.shape, F32)],
        compiler_params=_params("arbitrary", "arbitrary", "arbitrary"),
        name="rwkv",
    )(r, lw, k, v, a, g, k_k.reshape(1, d), k_a.reshape(1, d), r_k.reshape(1, d), lnx_g.reshape(1, d),
      lnx_b.reshape(1, d), s0)


def _trunk(x, mod, P, st):
    b, t, d = x.shape
    m = b * t
    depth = P["ada_w"].shape[0]
    _, heads_a, dk, dv = P["a_shape"]
    heads_b, dh_b = P["b_shape"]
    _, heads_c, dh_c, _ = P["c_shape"]
    w_a = heads_a * dk
    w_b = heads_b * dh_b
    a_C, a_n, a_m, b_k, b_v, c_S, c_sh = [], [], [], [], [], [], []
    v_first = None

    def gate_rows(gt):
        return gt if b == 1 else jnp.broadcast_to(gt[:, None, :], (b, t, d)).reshape(m, d)

    for l in range(depth):
        sh1, sc1, gt1, sh2, sc2, gt2 = (z.reshape(b, 1, d) for z in jnp.split(mod[l], 6, axis=-1))
        if l % 2 == 0:
            e = l // 2
            h = _norm_mod(x, P["norm1_g"][l], sc1, sh1).reshape(m, d)
            if st is None:
                c0 = jnp.zeros((b, heads_a, dk, dv), F32)
                n0 = jnp.zeros((b, heads_a, dk), F32)
                m0 = jnp.zeros((b, heads_a), F32)
            else:
                c0, n0, m0 = st["a_C"][e], st["a_n"][e], st["a_m"][e]
            proj_a = _matmul(h, P["ab_w_in"], layer=e, n_cols=4 * w_a)
            gif = _matmul(h, P["ab_w_gate"][e])
            proj_b = _matmul(h, P["ab_w_b"][e])
            ha, C, n, mm = _mlstm(proj_a, gif, P["ab_b_if"][e], P["a_norm_g"][e], c0, n0, m0, batch=b, seq=t)
            if st is None:
                hb = _attn_prompt(proj_b, P["bias_table"][e], batch=b, seq=t, heads=heads_b, dh=dh_b)
                keep = min(BAND_PAST, t)
            else:
                hb = _attn_sample(proj_b, st["b_k"], st["b_v"], e, P["bias_table"][e],
                                  batch=b, seq=t, heads=heads_b, dh=dh_b)
                keep = t
            kb = proj_b[:, w_b:2 * w_b].reshape(b, t, heads_b, dh_b)
            vb = proj_b[:, 2 * w_b:].reshape(b, t, heads_b, dh_b)
            a_C.append(C)
            a_n.append(n)
            a_m.append(mm)
            b_k.append(kb[:, t - keep:])
            b_v.append(vb[:, t - keep:])
            mix = jnp.concatenate([ha, hb], axis=-1)
            x = _matmul_resid(mix, P["ab_w_out"], x.reshape(m, d), gate_rows(gt1.reshape(b, d)), layer=e)
        else:
            o = l // 2
            if st is None:
                s0 = jnp.zeros((b, heads_c, dh_c, dh_c), F32)
                shift0 = jnp.zeros((b, d), F32)
            else:
                s0, shift0 = st["c_S"][o], st["c_shift"][o]
            (xr, xw, xk, xv, xa, xg), shift = _norm_mix(x, shift0, P["norm1_g"][l], sc1, sh1, P["c_mu"][o])
            flat = lambda z: z.reshape(m, d)
            r = _matmul(flat(xr), P["c_wr"], layer=o)
            k = _matmul(flat(xk), P["c_wk"], layer=o)
            v = _matmul(flat(xv), P["c_wv"], layer=o)
            lw = _lora(flat(xw), P["c_w1"][o], P["c_w2"][o], mid="tanh", epilogue="log_decay", bias=P["c_w0"][o])
            a = _lora(flat(xa), P["c_a1"][o], P["c_a2"][o], mid="none", epilogue="sigmoid", bias=P["c_a0"][o])
            g = _lora(flat(xg), P["c_g1"][o], P["c_g2"][o], mid="sigmoid", epilogue="none")
            if v_first is None:
                v_first = v
            else:
                v = _lora(flat(xv), P["c_v1"][o - 1], P["c_v2"][o - 1], mid="none", epilogue="vmix",
                          bias=P["c_v0"][o - 1], extra=(v, v_first))
            y, S = _rwkv(r, lw, k, v, a, g, P["c_k_k"][o], P["c_k_a"][o], P["c_r_k"][o], P["c_lnx_g"][o],
                         P["c_lnx_b"][o], s0, batch=b, seq=t)
            c_S.append(S)
            c_sh.append(shift)
            x = _matmul_resid(y, P["c_wo"], x.reshape(m, d), gate_rows(gt1.reshape(b, d)), layer=o)
        x = x.reshape(b, t, d)
        h = _norm_mod(x, P["norm2_g"][l], sc2, sh2).reshape(m, d)
        hid = _matmul_swiglu(h, P["ffn_w1"], P["ffn_w3"], layer=l)
        x = _matmul_resid(hid, P["ffn_w2"], x.reshape(m, d), gate_rows(gt2.reshape(b, d)), layer=l,
                          tm_pref=1024, tn_pref=256).reshape(b, t, d)
    y = _final_norm(x, P["final_g"])
    return (y, jnp.stack(a_C), jnp.stack(a_n), jnp.stack(a_m), jnp.stack(b_k), jnp.stack(b_v),
            jnp.stack(c_S), jnp.stack(c_sh))


def kernel(x_prompt, x_sample, c_prompt, c_sample, state_a_C, state_a_n, state_a_m, cache_b_k, cache_b_v, state_c_S, state_c_shift, ada_w, ada_b, norm1_g, norm2_g, final_g, ab_w_in, ab_b_if, a_norm_g, b_rel_bias, ab_w_out, c_mu, c_wr, c_wk, c_wv, c_wo, c_w0, c_w1, c_w2, c_a0, c_a1, c_a2, c_v0, c_v1, c_v2, c_g1, c_g2, c_k_k, c_k_a, c_r_k, c_lnx_g, c_lnx_b, ffn_w1, ffn_w3, ffn_w2):
    n_ab, dec_b, heads_a, dk, dv = state_a_C.shape
    _, _, width, heads_b, dh_b = cache_b_k.shape
    w_a = heads_a * dk
    gate_lo = 2 * w_a + 2 * heads_a * dv
    gate_hi = gate_lo + 2 * heads_a
    P = dict(ada_w=ada_w, norm1_g=norm1_g, norm2_g=norm2_g, final_g=final_g,
             ab_w_in=ab_w_in, ab_w_gate=ab_w_in[:, :, gate_lo:gate_hi], ab_w_b=ab_w_in[:, :, gate_hi:],
             ab_b_if=ab_b_if, a_norm_g=a_norm_g,
             bias_table=jnp.stack([_rel_bias_table(b_rel_bias[e]) for e in range(n_ab)]),
             ab_w_out=ab_w_out, c_mu=c_mu, c_wr=c_wr, c_wk=c_wk, c_wv=c_wv, c_wo=c_wo,
             c_w0=c_w0, c_w1=c_w1, c_w2=c_w2, c_a0=c_a0, c_a1=c_a1, c_a2=c_a2, c_v0=c_v0, c_v1=c_v1, c_v2=c_v2,
             c_g1=c_g1, c_g2=c_g2, c_k_k=c_k_k, c_k_a=c_k_a, c_r_k=c_r_k, c_lnx_g=c_lnx_g, c_lnx_b=c_lnx_b,
             ffn_w1=ffn_w1, ffn_w3=ffn_w3, ffn_w2=ffn_w2,
             a_shape=state_a_C.shape[1:], b_shape=(heads_b, dh_b), c_shape=state_c_S.shape[1:])
    st = dict(a_C=state_a_C, a_n=state_a_n, a_m=state_a_m,
              b_k=cache_b_k.reshape(n_ab, dec_b, width, heads_b * dh_b),
              b_v=cache_b_v.reshape(n_ab, dec_b, width, heads_b * dh_b),
              c_S=state_c_S, c_shift=state_c_shift)

    n_p, n_s = c_prompt.shape[0], c_sample.shape[0]
    rows = -(-(n_p + n_s) // 16) * 16
    c_all = jnp.concatenate([c_prompt, c_sample, jnp.zeros((rows - n_p - n_s, c_prompt.shape[1]), F32)], axis=0)
    mod = _ada_mod(c_all, ada_w, ada_b)

    outs_p = _trunk(x_prompt, mod[:, :n_p], P, None)
    outs_s = _trunk(x_sample, mod[:, n_p:n_p + n_s], P, st)
    return (outs_p[0], outs_s[0]) + tuple(outs_p[1:]) + tuple(outs_s[1:])
```

```python
import functools

import numpy as np
import jax
import jax.numpy as jnp
from jax import lax
from jax.experimental import pallas as pl
from jax.experimental.pallas import tpu as pltpu

F32 = jnp.float32
BF16 = jnp.bfloat16

CHUNK = 64
N_PREV_CHUNKS = 8
BAND_PAST = N_PREV_CHUNKS * CHUNK
PAST_LEN = 2048
REL_MAX = 2 * CHUNK
GATE_CAP = 15.0
EPS = 1e-6
GN_EPS = 64e-5

V7X_VMEM_BYTES = 64 * 1024 * 1024
VMEM_LIMIT = V7X_VMEM_BYTES - 8 * 1024 * 1024
LANES = 128
MATMUL_ROW_SUBTILE = 1024


def _params(*sem):
    return pltpu.CompilerParams(dimension_semantics=sem, vmem_limit_bytes=VMEM_LIMIT)


def _tile(n, pref, mult):
    if n <= pref:
        return n
    t = (pref // mult) * mult
    while t >= mult:
        if n % t == 0:
            return t
        t -= mult
    return n


def _silu(x):
    return x * jax.nn.sigmoid(x)


def _resident(block_shape, index_map):
    return pl.BlockSpec(block_shape, index_map, pipeline_mode=pl.Buffered(1))


def _ada_kernel(c_ref, w_ref, b_ref, o_ref):
    cs = _silu(c_ref[...]).astype(BF16)
    acc = jnp.dot(cs, w_ref[...].astype(BF16), preferred_element_type=F32)
    o_ref[...] = acc + b_ref[...]


def _ada_mod(c_all, ada_w, ada_b):
    depth, d, n = ada_w.shape
    rows = c_all.shape[0]
    tn = _tile(n, 512, LANES)
    return pl.pallas_call(
        _ada_kernel,
        grid=(depth, n // tn),
        in_specs=[
            pl.BlockSpec((rows, d), lambda l, j: (0, 0)),
            pl.BlockSpec((None, d, tn), lambda l, j: (l, 0, j)),
            pl.BlockSpec((None, 1, tn), lambda l, j: (l, 0, j)),
        ],
        out_specs=pl.BlockSpec((None, rows, tn), lambda l, j: (l, 0, j)),
        out_shape=jax.ShapeDtypeStruct((depth, rows, n), F32),
        compiler_params=_params("arbitrary", "arbitrary"),
        name="ada_mod",
    )(c_all, ada_w, ada_b.reshape(depth, 1, n))


def _norm_mod_value(x, g, sc, sh):
    y = x * lax.rsqrt(jnp.mean(x * x, axis=-1, keepdims=True) + EPS)
    return (y * g) * (1.0 + sc) + sh


def _norm_mod_kernel(x_ref, g_ref, sc_ref, sh_ref, o_ref):
    o_ref[0] = _norm_mod_value(x_ref[0], g_ref[...], sc_ref[0], sh_ref[0]).astype(o_ref.dtype)


def _norm_mod(x, g, sc, sh, out_dtype=BF16):
    b, t, d = x.shape
    tt = _tile(t, 256, 16)
    row = pl.BlockSpec((1, tt, d), lambda bi, i: (bi, i, 0))
    per_batch = pl.BlockSpec((1, 1, d), lambda bi, i: (bi, 0, 0))
    return pl.pallas_call(
        _norm_mod_kernel,
        grid=(b, t // tt),
        in_specs=[row, pl.BlockSpec((1, d), lambda bi, i: (0, 0)), per_batch, per_batch],
        out_specs=row,
        out_shape=jax.ShapeDtypeStruct((b, t, d), out_dtype),
        compiler_params=_params("arbitrary", "arbitrary"),
        name="norm_mod",
    )(x, g.reshape(1, d), sc, sh)


def _final_norm_kernel(x_ref, g_ref, o_ref):
    x = x_ref[0]
    o_ref[0] = (x * lax.rsqrt(jnp.mean(x * x, axis=-1, keepdims=True) + EPS)) * g_ref[...]


def _final_norm(x, g):
    b, t, d = x.shape
    tt = _tile(t, 256, 8)
    row = pl.BlockSpec((1, tt, d), lambda bi, i: (bi, i, 0))
    return pl.pallas_call(
        _final_norm_kernel,
        grid=(b, t // tt),
        in_specs=[row, pl.BlockSpec((1, d), lambda bi, i: (0, 0))],
        out_specs=row,
        out_shape=jax.ShapeDtypeStruct((b, t, d), F32),
        compiler_params=_params("arbitrary", "arbitrary"),
        name="final_norm",
    )(x, g.reshape(1, d))


def _norm_mix_kernel(x_ref, xp_ref, s0_ref, g_ref, sc_ref, sh_ref, mu_ref, *out_refs):
    i = pl.program_id(1)
    mix_refs, last_ref = out_refs[:6], out_refs[6]
    g, sc, sh = g_ref[...], sc_ref[0], sh_ref[0]
    h = _norm_mod_value(x_ref[0], g, sc, sh)
    hp = _norm_mod_value(xp_ref[0], g, sc, sh)
    prev_row = jnp.where(i == 0, s0_ref[0], hp[7:8, :])
    row_id = lax.broadcasted_iota(jnp.int32, h.shape, 0)
    shifted = jnp.where(row_id == 0, prev_row, pltpu.roll(h, 1, axis=0))
    xx = shifted - h
    for j in range(6):
        mix_refs[j][0] = (h + xx * mu_ref[j:j + 1, :]).astype(BF16)
    last_ref[0] = h[h.shape[0] - 8:, :]


def _norm_mix(x, shift0, g, sc, sh, mu):
    b, t, d = x.shape
    tt = _tile(t, 256, 16)
    row = pl.BlockSpec((1, tt, d), lambda bi, i: (bi, i, 0))
    prev8 = pl.BlockSpec((1, 8, d), lambda bi, i: (bi, jnp.maximum(i * (tt // 8) - 1, 0), 0))
    per_batch = pl.BlockSpec((1, 1, d), lambda bi, i: (bi, 0, 0))
    outs = pl.pallas_call(
        _norm_mix_kernel,
        grid=(b, t // tt),
        in_specs=[row, prev8, per_batch, pl.BlockSpec((1, d), lambda bi, i: (0, 0)), per_batch, per_batch,
                  pl.BlockSpec((6, d), lambda bi, i: (0, 0))],
        out_specs=[row] * 6 + [pl.BlockSpec((1, 8, d), lambda bi, i: (bi, 0, 0))],
        out_shape=[jax.ShapeDtypeStruct((b, t, d), BF16)] * 6 + [jax.ShapeDtypeStruct((b, 8, d), F32)],
        compiler_params=_params("arbitrary", "arbitrary"),
        name="norm_mix",
    )(x, x, shift0.reshape(b, 1, d), g.reshape(1, d), sc, sh, mu)
    return outs[:6], outs[6][:, 7, :]


def _mm_rows(tm):
    sub = MATMUL_ROW_SUBTILE if tm % MATMUL_ROW_SUBTILE == 0 else tm
    return sub, tm // sub


def _row_slice(m, sub):
    return pl.ds(m * sub, sub) if isinstance(m, int) else pl.ds(pl.multiple_of(m * sub, sub), sub)


def _for_row_subtiles(n_sub, body):
    if n_sub <= 2:
        for m in range(n_sub):
            body(m, 0)
    else:
        lax.fori_loop(0, n_sub, body, 0)


def _mm_kernel(a_ref, w_ref, o_ref, wb_ref, *, tm):
    wb_ref[...] = w_ref[...].astype(BF16)
    sub, n_sub = _mm_rows(tm)

    def body(m, carry):
        rows = _row_slice(m, sub)
        acc = jnp.dot(a_ref[rows, :], wb_ref[...], preferred_element_type=F32)
        o_ref[rows, :] = acc.astype(o_ref.dtype)
        return carry

    _for_row_subtiles(n_sub, body)


def _mm_resid_kernel(*refs, tm, parts):
    a_refs, w_refs = refs[:parts], refs[parts:2 * parts]
    r_ref, g_ref, o_ref = refs[2 * parts:2 * parts + 3]
    wb_refs = refs[2 * parts + 3:]
    for w_ref, wb_ref in zip(w_refs, wb_refs):
        wb_ref[...] = w_ref[...].astype(BF16)
    sub, n_sub = _mm_rows(tm)
    per_row_gate = g_ref.shape[0] != 1

    def body(m, carry):
        rows = _row_slice(m, sub)
        acc = jnp.dot(a_refs[0][rows, :], wb_refs[0][...], preferred_element_type=F32)
        for a_ref, wb_ref in zip(a_refs[1:], wb_refs[1:]):
            acc = acc + jnp.dot(a_ref[rows, :], wb_ref[...], preferred_element_type=F32)
        gate = g_ref[rows, :] if per_row_gate else g_ref[...]
        o_ref[rows, :] = r_ref[rows, :] + gate * acc
        return carry

    _for_row_subtiles(n_sub, body)


def _mm_swiglu_kernel(a_ref, w1_ref, w3_ref, o_ref, w1b_ref, w3b_ref, *, tm):
    w1b_ref[...] = w1_ref[...].astype(BF16)
    w3b_ref[...] = w3_ref[...].astype(BF16)
    sub, n_sub = _mm_rows(tm)

    def body(m, carry):
        rows = _row_slice(m, sub)
        a = a_ref[rows, :]
        u = jnp.dot(a, w1b_ref[...], preferred_element_type=F32)
        v = jnp.dot(a, w3b_ref[...], preferred_element_type=F32)
        o_ref[rows, :] = (_silu(u) * v).astype(o_ref.dtype)
        return carry

    _for_row_subtiles(n_sub, body)


def _weight_spec(w, layer, k, tn):
    if w.ndim == 3:
        return pl.BlockSpec((None, k, tn), lambda i, j: (layer, 0, j))
    return pl.BlockSpec((k, tn), lambda i, j: (0, j))


def _matmul(a, w, *, layer=0, n_cols=None, out_dtype=F32, tm_pref=2048, tn_pref=512):
    m, k = a.shape
    n = w.shape[-1] if n_cols is None else n_cols
    tm = _tile(m, tm_pref, MATMUL_ROW_SUBTILE)
    tn = _tile(n, tn_pref, LANES)
    return pl.pallas_call(
        functools.partial(_mm_kernel, tm=tm),
        grid=(m // tm, n // tn),
        in_specs=[_resident((tm, k), lambda i, j: (i, 0)), _weight_spec(w, layer, k, tn)],
        out_specs=pl.BlockSpec((tm, tn), lambda i, j: (i, j)),
        out_shape=jax.ShapeDtypeStruct((m, n), out_dtype),
        scratch_shapes=[pltpu.VMEM((k, tn), BF16)],
        compiler_params=_params("arbitrary", "arbitrary"),
        name="matmul",
    )(a, w)


def _matmul_resid(a_parts, w, res, gate, *, layer, tm_pref=2048, tn_pref=512):
    parts = len(a_parts)
    m = a_parts[0].shape[0]
    ks = [a.shape[1] for a in a_parts]
    offs = [sum(ks[:p]) for p in range(parts)]
    n = w.shape[-1]
    assert w.shape[1] == sum(ks) and all(off % k == 0 for off, k in zip(offs, ks))
    tm = _tile(m, tm_pref, MATMUL_ROW_SUBTILE)
    tn = _tile(n, tn_pref, LANES)
    if gate.shape[0] == 1:
        gate_spec = pl.BlockSpec((1, tn), lambda i, j: (0, j))
    else:
        gate_spec = pl.BlockSpec((tm, tn), lambda i, j: (i, j))
    w_specs = [pl.BlockSpec((None, k, tn), functools.partial(lambda i, j, blk: (layer, blk, j), blk=off // k))
               for off, k in zip(offs, ks)]
    return pl.pallas_call(
        functools.partial(_mm_resid_kernel, tm=tm, parts=parts),
        grid=(m // tm, n // tn),
        in_specs=[_resident((tm, k), lambda i, j: (i, 0)) for k in ks] + w_specs
                 + [pl.BlockSpec((tm, tn), lambda i, j: (i, j)), gate_spec],
        out_specs=pl.BlockSpec((tm, tn), lambda i, j: (i, j)),
        out_shape=jax.ShapeDtypeStruct((m, n), F32),
        scratch_shapes=[pltpu.VMEM((k, tn), BF16) for k in ks],
        compiler_params=_params("arbitrary", "arbitrary"),
        name="matmul_resid",
    )(*a_parts, *([w] * parts), res, gate)


def _matmul_swiglu(a, w1, w3, *, layer, tm_pref=2048, tn_pref=256):
    m, k = a.shape
    n = w1.shape[-1]
    tm = _tile(m, tm_pref, MATMUL_ROW_SUBTILE)
    tn = _tile(n, tn_pref, LANES)
    return pl.pallas_call(
        functools.partial(_mm_swiglu_kernel, tm=tm),
        grid=(m // tm, n // tn),
        in_specs=[_resident((tm, k), lambda i, j: (i, 0)), _weight_spec(w1, layer, k, tn),
                  _weight_spec(w3, layer, k, tn)],
        out_specs=pl.BlockSpec((tm, tn), lambda i, j: (i, j)),
        out_shape=jax.ShapeDtypeStruct((m, n), BF16),
        scratch_shapes=[pltpu.VMEM((k, tn), BF16), pltpu.VMEM((k, tn), BF16)],
        compiler_params=_params("arbitrary", "arbitrary"),
        name="matmul_swiglu",
    )(a, w1, w3)


def _lora_kernel(a_ref, w1_ref, w2_ref, *rest, mid, epilogue):
    o_ref = rest[-1]
    z = jnp.dot(a_ref[...], w1_ref[...], preferred_element_type=F32)
    if mid == "tanh":
        z = jnp.tanh(z)
    elif mid == "sigmoid":
        z = jax.nn.sigmoid(z)
    z = jnp.dot(z.astype(BF16), w2_ref[...], preferred_element_type=F32)
    if epilogue == "log_decay":
        w_log = -jax.nn.softplus(-(rest[0][...] + z)) - 0.5
        o_ref[...] = -jnp.exp(w_log)
    elif epilogue == "sigmoid":
        o_ref[...] = jax.nn.sigmoid(rest[0][...] + z)
    elif epilogue == "vmix":
        v, vf = rest[1][...], rest[2][...]
        o_ref[...] = v + (vf - v) * jax.nn.sigmoid(rest[0][...] + z)
    else:
        o_ref[...] = z


def _lora(a, w1, w2, *, mid, epilogue, bias=None, extra=()):
    m, d = a.shape
    r = w1.shape[1]
    rp = -(-r // LANES) * LANES
    w1p = jnp.pad(w1, ((0, 0), (0, rp - r))).astype(BF16)
    w2p = jnp.pad(w2, ((0, rp - r), (0, 0))).astype(BF16)
    n = w2.shape[1]
    tm = _tile(m, 256, 16)
    row = pl.BlockSpec((tm, n), lambda i: (i, 0))
    in_specs = [pl.BlockSpec((tm, d), lambda i: (i, 0)), _resident((d, rp), lambda i: (0, 0)),
                _resident((rp, n), lambda i: (0, 0))]
    args = [a, w1p, w2p]
    if bias is not None:
        in_specs.append(pl.BlockSpec((1, n), lambda i: (0, 0)))
        args.append(bias.reshape(1, n))
    for e in extra:
        in_specs.append(row)
        args.append(e)
    return pl.pallas_call(
        functools.partial(_lora_kernel, mid=mid, epilogue=epilogue),
        grid=(m // tm,),
        in_specs=in_specs,
        out_specs=row,
        out_shape=jax.ShapeDtypeStruct((m, n), F32),
        compiler_params=_params("arbitrary"),
        name="lora_" + epilogue,
    )(*args)


def _dot_nt(a, b):
    return lax.dot_general(a, b, (((1,), (1,)), ((), ())), preferred_element_type=F32)


def _dot_tn(a, b):
    return lax.dot_general(a, b, (((0,), (0,)), ((), ())), preferred_element_type=F32)


def _soft_cap(x):
    return GATE_CAP * jnp.tanh(x / GATE_CAP)


def _mlstm_kernel(q_ref, k_ref, v_ref, o_ref, gc_ref, gr_ref, bc_ref, br_ref, ng_ref, c0_ref, n0_ref, m0_ref,
                  h_ref, c_ref, n_ref, m_ref, *, heads, dk, dv, length):
    c_idx = pl.program_id(1)

    @pl.when(c_idx == 0)
    def _():
        c_ref[...] = c0_ref[...]
        n_ref[...] = n0_ref[...]
        m_ref[...] = m0_ref[...]

    gcol = gc_ref[...] + bc_ref[...]
    grow = gr_ref[...] + br_ref[...]
    li_col, lf_col = _soft_cap(gcol[:, :heads]), jax.nn.log_sigmoid(_soft_cap(gcol[:, heads:]))
    li_row, lf_row = _soft_cap(grow[:heads, :]), jax.nn.log_sigmoid(_soft_cap(grow[heads:, :]))
    r_id = lax.broadcasted_iota(jnp.int32, (length, length), 0)
    c_id = lax.broadcasted_iota(jnp.int32, (length, length), 1)
    tril = c_id <= r_id
    scale = dk ** -0.5

    hs = range(heads)
    gates = []
    for h in hs:
        lf_r, li_r = lf_row[h:h + 1, :], li_row[h:h + 1, :]
        lf_c, li_c = lf_col[:, h:h + 1], li_col[:, h:h + 1]
        b_col = jnp.sum(jnp.where(tril, lf_r, 0.0), axis=1, keepdims=True)
        b_row = jnp.sum(jnp.where(r_id <= c_id, lf_c, 0.0), axis=0, keepdims=True)
        b_last = b_col[length - 1:length, :]
        m_prev = m_ref[0, h:h + 1, :]
        log_d = jnp.where(tril, b_col - b_row + li_r, -jnp.inf)
        inter = b_col + m_prev
        m_t = jnp.maximum(inter, jnp.max(log_d, axis=1, keepdims=True))
        m_new = m_t[length - 1:length, :]
        gates.append(dict(p=jnp.exp(log_d - m_t), g=jnp.exp(inter - m_t), m_t=m_t, m_new=m_new,
                          wk_col=jnp.exp(b_last - b_col + li_c - m_new),
                          decay=jnp.exp(b_last + m_prev - m_new)))

    q = [q_ref[:, h * dk:(h + 1) * dk] for h in hs]
    k = [k_ref[:, h * dk:(h + 1) * dk] * scale for h in hs]
    kw = [k[h] * gates[h]["wk_col"] for h in hs]
    qb = [x.astype(BF16) for x in q]
    kb = [x.astype(BF16) for x in k]
    vb = [v_ref[:, h * dv:(h + 1) * dv].astype(BF16) for h in hs]
    c_state = [c_ref[0, h] for h in hs]
    n_state = [n_ref[0, h:h + 1, :] for h in hs]
    s_qk = [_dot_nt(qb[h], kb[h]) for h in hs]
    q_c = [jnp.dot(qb[h], c_state[h].astype(BF16), preferred_element_type=F32) for h in hs]
    kv = [_dot_tn(kw[h].astype(BF16), vb[h]) for h in hs]
    wqk = [gates[h]["p"] * s_qk[h] for h in hs]
    num = [jnp.dot(wqk[h].astype(BF16), vb[h], preferred_element_type=F32) + gates[h]["g"] * q_c[h] for h in hs]

    for h in hs:
        g = gates[h]
        den = jnp.sum(wqk[h], axis=1, keepdims=True) + g["g"] * jnp.sum(q[h] * n_state[h], axis=1, keepdims=True)
        hh = num[h] / jnp.maximum(jnp.abs(den), jnp.exp(-g["m_t"]))
        hh = hh * lax.rsqrt(jnp.mean(hh * hh, axis=1, keepdims=True) + EPS)
        hh = hh * ng_ref[:, h * dv:(h + 1) * dv]
        h_ref[:, h * dv:(h + 1) * dv] = (jax.nn.sigmoid(o_ref[:, h * dv:(h + 1) * dv]) * hh).astype(h_ref.dtype)
        c_ref[0, h] = g["decay"] * c_state[h] + kv[h]
        n_ref[0, h:h + 1, :] = g["decay"] * n_state[h] + jnp.sum(kw[h], axis=0, keepdims=True)
        m_ref[0, h:h + 1, :] = g["m_new"]


def _mlstm(proj, gif, b_if, a_norm_g, c0, n0, m0, *, batch, seq):
    _, heads, dk, dv = c0.shape
    length = min(CHUNK, seq)
    nc = seq // length
    wq, wv = heads * dk, heads * dv
    assert wq == wv
    gcol = gif
    grow = gif.reshape(batch, nc, length, 2 * heads).transpose(0, 1, 3, 2)
    row = lambda col: pl.BlockSpec((length, wq), lambda b, c: (b * nc + c, col))
    state4 = pl.BlockSpec((1, heads, dk, dv), lambda b, c: (b, 0, 0, 0))
    state3 = pl.BlockSpec((1, heads, dk), lambda b, c: (b, 0, 0))
    state_m = pl.BlockSpec((1, heads, 1), lambda b, c: (b, 0, 0))
    h, c_out, n_out, m_out = pl.pallas_call(
        functools.partial(_mlstm_kernel, heads=heads, dk=dk, dv=dv, length=length),
        grid=(batch, nc),
        in_specs=[row(0), row(1), row(2), row(3),
                  pl.BlockSpec((length, 2 * heads), lambda b, c: (b * nc + c, 0)),
                  pl.BlockSpec((None, None, 2 * heads, length), lambda b, c: (b, c, 0, 0)),
                  pl.BlockSpec((1, 2 * heads), lambda b, c: (0, 0)),
                  pl.BlockSpec((2 * heads, 1), lambda b, c: (0, 0)),
                  pl.BlockSpec((1, wv), lambda b, c: (0, 0)),
                  state4, state3, state_m],
        out_specs=[pl.BlockSpec((length, wv), lambda b, c: (b * nc + c, 0)), state4, state3, state_m],
        out_shape=[jax.ShapeDtypeStruct((batch * seq, wv), BF16),
                   jax.ShapeDtypeStruct((batch, heads, dk, dv), F32),
                   jax.ShapeDtypeStruct((batch, heads, dk), F32),
                   jax.ShapeDtypeStruct((batch, heads, 1), F32)],
        compiler_params=_params("arbitrary", "arbitrary"),
        name="mlstm",
    )(proj, proj, proj, proj, gcol, grow, b_if.reshape(1, 2 * heads), b_if.reshape(2 * heads, 1),
      a_norm_g.reshape(1, wv), c0, n0, m0.reshape(batch, heads, 1))
    return h, c_out, n_out, m_out.reshape(batch, heads)


def _rel_bias_table(rel_bias):
    heads, rel_size = rel_bias.shape
    assert rel_size == CHUNK + REL_MAX
    band = BAND_PAST + CHUNK
    n_dist = band + CHUNK - 1
    ext = jnp.concatenate([rel_bias, jnp.broadcast_to(rel_bias[:, -1:], (heads, n_dist - rel_size))], axis=1)
    rev = ext[:, ::-1]
    return jnp.stack([rev[:, CHUNK - 1 - i:CHUNK - 1 - i + band] for i in range(CHUNK)], axis=1)


def _band_chunks(qs, ks, vs, biases, scale):
    n = range(len(qs))
    s = [_dot_nt(qs[i], ks[i]) * scale + biases[i] for i in n]
    p = [jnp.exp(s[i] - jnp.max(s[i], axis=1, keepdims=True)) for i in n]
    o = [jnp.dot(p[i].astype(BF16), vs[i], preferred_element_type=F32) for i in n]
    return [o[i] / jnp.sum(p[i], axis=1, keepdims=True) for i in n]


ATTN_CHUNKS_PER_ITER = 4


def _attn_prompt_kernel(q_ref, k_ref, v_ref, bias_ref, o_ref, kb_ref, vb_ref, *, seq, scale):
    kb_ref[...] = k_ref[...].astype(BF16)
    vb_ref[...] = v_ref[...].astype(BF16)
    nc = seq // CHUNK
    band = BAND_PAST + CHUNK
    bias = bias_ref[0]

    lead = list(range(min(N_PREV_CHUNKS, nc)))
    for c0 in range(0, len(lead), ATTN_CHUNKS_PER_ITER):
        cs = lead[c0:c0 + ATTN_CHUNKS_PER_ITER]
        widths = [(c + 1) * CHUNK for c in cs]
        outs = _band_chunks([q_ref[c * CHUNK:(c + 1) * CHUNK, :].astype(BF16) for c in cs],
                            [kb_ref[0:w, :] for w in widths], [vb_ref[0:w, :] for w in widths],
                            [bias[:, band - w:] for w in widths], scale)
        for c, o in zip(cs, outs):
            o_ref[c * CHUNK:(c + 1) * CHUNK, :] = o.astype(o_ref.dtype)

    rest = nc - N_PREV_CHUNKS
    per_iter = ATTN_CHUNKS_PER_ITER if rest % ATTN_CHUNKS_PER_ITER == 0 else 1

    def body(it, carry):
        starts = [pl.multiple_of((N_PREV_CHUNKS + it * per_iter + u) * CHUNK, CHUNK) for u in range(per_iter)]
        k_rows = [pl.ds(pl.multiple_of(s - BAND_PAST, CHUNK), band) for s in starts]
        outs = _band_chunks([q_ref[pl.ds(s, CHUNK), :].astype(BF16) for s in starts],
                            [kb_ref[r, :] for r in k_rows], [vb_ref[r, :] for r in k_rows],
                            [bias] * per_iter, scale)
        for s, o in zip(starts, outs):
            o_ref[pl.ds(s, CHUNK), :] = o.astype(o_ref.dtype)
        return carry

    if rest > 0:
        lax.fori_loop(0, rest // per_iter, body, 0)


def _attn_prompt(proj_b, bias_table, *, batch, seq, heads, dh):
    col = lambda base: pl.BlockSpec((seq, dh), lambda b, h: (b, base * heads + h))
    return pl.pallas_call(
        functools.partial(_attn_prompt_kernel, seq=seq, scale=dh ** -0.5),
        grid=(batch, heads),
        in_specs=[col(0), col(1), col(2),
                  pl.BlockSpec((1, CHUNK, BAND_PAST + CHUNK), lambda b, h: (h, 0, 0))],
        out_specs=pl.BlockSpec((seq, dh), lambda b, h: (b, h)),
        out_shape=jax.ShapeDtypeStruct((batch * seq, heads * dh), BF16),
        scratch_shapes=[pltpu.VMEM((seq, dh), BF16), pltpu.VMEM((seq, dh), BF16)],
        compiler_params=_params("arbitrary", "arbitrary"),
        name="attn_prompt",
    )(proj_b, proj_b, proj_b, bias_table)


def _attn_sample_kernel(q_ref, k_ref, v_ref, pk_ref, pv_ref, bias_ref, o_ref, *, seq, width, scale):
    q = q_ref[...].astype(BF16)
    bias = bias_ref[0]
    off = BAND_PAST - width
    s_past = _dot_nt(q, pk_ref[...].astype(BF16)) * scale + bias[:seq, off:off + width]
    s_new = _dot_nt(q, k_ref[...].astype(BF16)) * scale + bias[:seq, BAND_PAST:BAND_PAST + seq]
    m = jnp.maximum(jnp.max(s_past, axis=1, keepdims=True), jnp.max(s_new, axis=1, keepdims=True))
    p_past, p_new = jnp.exp(s_past - m), jnp.exp(s_new - m)
    o = jnp.dot(p_past.astype(BF16), pv_ref[...].astype(BF16), preferred_element_type=F32) \
        + jnp.dot(p_new.astype(BF16), v_ref[...].astype(BF16), preferred_element_type=F32)
    denom = jnp.sum(p_past, axis=1, keepdims=True) + jnp.sum(p_new, axis=1, keepdims=True)
    o_ref[...] = (o / denom).astype(o_ref.dtype)


def _attn_sample(proj_b, past_k, past_v, layer, bias_table, *, batch, seq, heads, dh):
    width = past_k.shape[2]
    col = lambda base: pl.BlockSpec((seq, dh), lambda b, h: (b, base * heads + h))
    past = pl.BlockSpec((None, None, width, dh), lambda b, h: (layer, b, 0, h))
    return pl.pallas_call(
        functools.partial(_attn_sample_kernel, seq=seq, width=width, scale=dh ** -0.5),
        grid=(batch, heads),
        in_specs=[col(0), col(1), col(2), past, past,
                  pl.BlockSpec((1, CHUNK, BAND_PAST + CHUNK), lambda b, h: (h, 0, 0))],
        out_specs=pl.BlockSpec((seq, dh), lambda b, h: (b, h)),
        out_shape=jax.ShapeDtypeStruct((batch * seq, heads * dh), BF16),
        compiler_params=_params("arbitrary", "arbitrary"),
        name="attn_sample",
    )(proj_b, proj_b, proj_b, past_k, past_v, bias_table)


def _split_dot(a_exact, x):
    hi = x.astype(BF16)
    lo = (x - hi.astype(F32)).astype(BF16)
    return jnp.dot(a_exact, hi, preferred_element_type=F32) + jnp.dot(a_exact, lo, preferred_element_type=F32)


def _rwkv_kernel(r_ref, lw_ref, k_ref, v_ref, a_ref, g_ref, kk_ref, ka_ref, rk_ref, lg_ref, lb_ref, s0_ref,
                 o_ref, s_ref, *, heads, dh, length):
    c_idx = pl.program_id(2)

    @pl.when(c_idx == 0)
    def _():
        s_ref[...] = s0_ref[...]

    r_id = lax.broadcasted_iota(jnp.int32, (length, length), 0)
    c_id = lax.broadcasted_iota(jnp.int32, (length, length), 1)
    incl = c_id <= r_id
    strict = c_id < r_id
    r_id2 = lax.broadcasted_iota(jnp.int32, (length, 2 * length), 0)
    c_id2 = lax.broadcasted_iota(jnp.int32, (length, 2 * length), 1)
    incl2 = jnp.where(c_id2 < length, c_id2, c_id2 - length) <= r_id2
    tril_bf = jnp.where(incl, 1.0, 0.0).astype(BF16)
    n_double = max(int(np.ceil(np.log2(length))), 1)

    hs = range(heads)
    sl = lambda x, h: x[:, h * dh:(h + 1) * dh]
    r_all, lw_all, k_all, v_all, a_all = r_ref[...], lw_ref[...], k_ref[...], v_ref[...], a_ref[...]
    cum = _split_dot(tril_bf, lw_all)
    w_in = jnp.exp(cum)
    w_inv = jnp.exp(-cum)
    w_ex = jnp.exp(cum - lw_all)
    kk_all = k_all * kk_ref[...]
    k2_all = k_all * (1.0 + (a_all - 1.0) * ka_ref[...])
    rt_all = r_all * w_in
    kt_all = k2_all * w_inv
    ba_all = a_all * w_inv
    rk2_all = r_all * k2_all * rk_ref[...]

    lhs, rhs = [], []
    for h in hs:
        kk = sl(kk_all, h)
        kk = kk / jnp.maximum(jnp.sqrt(jnp.sum(kk * kk, axis=1, keepdims=True)), 1e-12)
        at = (-kk) * sl(w_ex, h)
        bt = kk * sl(ba_all, h)
        lhs.append(jnp.concatenate([at, sl(rt_all, h)], axis=0).astype(BF16))
        rhs.append(jnp.concatenate([bt, sl(kt_all, h)], axis=0).astype(BF16))
    s0 = [s_ref[0, h] for h in hs]
    vf = [sl(v_all, h) for h in hs]
    vb = [v.astype(BF16) for v in vf]
    aa = [_dot_nt(lhs[h], rhs[h]) for h in hs]
    ls = [_dot_nt(lhs[h], s0[h].astype(BF16)) for h in hs]
    x = [ls[h][:length] + jnp.dot(jnp.where(strict, aa[h][:length, length:], 0.0).astype(BF16), vb[h],
                                  preferred_element_type=F32) for h in hs]
    pw = [jnp.where(strict, aa[h][:length, :length], 0.0) for h in hs]
    for j in range(n_double):
        pwb = [p.astype(BF16) for p in pw]
        x = [x[h] + jnp.dot(pwb[h], x[h].astype(BF16), preferred_element_type=F32) for h in hs]
        if j + 1 < n_double:
            pw = [jnp.dot(pwb[h], pwb[h], preferred_element_type=F32) for h in hs]
    uv = [jnp.concatenate([x[h], vf[h]], axis=0).astype(BF16) for h in hs]
    y = [ls[h][length:] + jnp.dot(jnp.where(incl2, aa[h][length:, :], 0.0).astype(BF16), uv[h],
                                  preferred_element_type=F32) for h in hs]
    ds = [_dot_tn(uv[h], rhs[h]) for h in hs]
    for h in hs:
        s_ref[0, h] = (s0[h] + ds[h]) * sl(w_in, h)[length - 1:length, :]

    for h in hs:
        cols = slice(h * dh, (h + 1) * dh)
        yc = y[h] - jnp.mean(y[h], axis=1, keepdims=True)
        yn = yc * lax.rsqrt(jnp.mean(yc * yc, axis=1, keepdims=True) + GN_EPS)
        yn = yn * lg_ref[:, cols] + lb_ref[:, cols]
        bonus = jnp.sum(sl(rk2_all, h), axis=1, keepdims=True) * vf[h]
        o_ref[:, cols] = ((yn + bonus) * g_ref[:, cols]).astype(o_ref.dtype)


def _rwkv(r, lw, k, v, a, g, k_k, k_a, r_k, lnx_g, lnx_b, s0, *, batch, seq):
    _, heads, dh, _ = s0.shape
    d = heads * dh
    length = min(CHUNK, seq)
    nc = seq // length
    hg = _tile(heads, 16, 2)
    wg = hg * dh
    row = pl.BlockSpec((length, wg), lambda b, gi, c: (b * nc + c, gi))
    par = pl.BlockSpec((1, wg), lambda b, gi, c: (0, gi))
    state = pl.BlockSpec((1, hg, dh, dh), lambda b, gi, c: (b, gi, 0, 0))
    return pl.pallas_call(
        functools.partial(_rwkv_kernel, heads=hg, dh=dh, length=length),
        grid=(batch, heads // hg, nc),
        in_specs=[row] * 6 + [par] * 5 + [state],
        out_specs=[row, state],
        out_shape=[jax.ShapeDtypeStruct((batch * seq, d), BF16), jax.ShapeDtypeStruct(s0.shape, F32)],
        compiler_params=_params("arbitrary", "arbitrary", "arbitrary"),
        name="rwkv",
    )(r, lw, k, v, a, g, k_k.reshape(1, d), k_a.reshape(1, d), r_k.reshape(1, d), lnx_g.reshape(1, d),
      lnx_b.reshape(1, d), s0)


def _trunk(x, mod, P, st):
    b, t, d = x.shape
    m = b * t
    depth = P["ada_w"].shape[0]
    _, heads_a, dk, dv = P["a_shape"]
    heads_b, dh_b = P["b_shape"]
    _, heads_c, dh_c, _ = P["c_shape"]
    w_a = heads_a * dk
    w_b = heads_b * dh_b
    a_C, a_n, a_m, b_k, b_v, c_S, c_sh = [], [], [], [], [], [], []
    v_first = None

    def gate_rows(gt):
        return gt if b == 1 else jnp.broadcast_to(gt[:, None, :], (b, t, d)).reshape(m, d)

    for l in range(depth):
        sh1, sc1, gt1, sh2, sc2, gt2 = (z.reshape(b, 1, d) for z in jnp.split(mod[l], 6, axis=-1))
        if l % 2 == 0:
            e = l // 2
            h = _norm_mod(x, P["norm1_g"][l], sc1, sh1).reshape(m, d)
            if st is None:
                c0 = jnp.zeros((b, heads_a, dk, dv), F32)
                n0 = jnp.zeros((b, heads_a, dk), F32)
                m0 = jnp.zeros((b, heads_a), F32)
            else:
                c0, n0, m0 = st["a_C"][e], st["a_n"][e], st["a_m"][e]
            proj_a = _matmul(h, P["ab_w_a"], layer=e)
            gif = _matmul(h, P["ab_w_gate"], layer=e)
            proj_b = _matmul(h, P["ab_w_b"], layer=e)
            ha, C, n, mm = _mlstm(proj_a, gif, P["ab_b_if"][e], P["a_norm_g"][e], c0, n0, m0, batch=b, seq=t)
            if st is None:
                hb = _attn_prompt(proj_b, P["bias_table"][e], batch=b, seq=t, heads=heads_b, dh=dh_b)
                keep = min(BAND_PAST, t)
            else:
                hb = _attn_sample(proj_b, st["b_k"], st["b_v"], e, P["bias_table"][e],
                                  batch=b, seq=t, heads=heads_b, dh=dh_b)
                keep = t
            kept = proj_b.reshape(b, t, 3 * w_b)[:, t - keep:, :]
            a_C.append(C)
            a_n.append(n)
            a_m.append(mm)
            b_k.append(kept[:, :, w_b:2 * w_b].reshape(b, keep, heads_b, dh_b))
            b_v.append(kept[:, :, 2 * w_b:].reshape(b, keep, heads_b, dh_b))
            x = _matmul_resid([ha, hb], P["ab_w_out"], x.reshape(m, d), gate_rows(gt1.reshape(b, d)), layer=e)
        else:
            o = l // 2
            if st is None:
                s0 = jnp.zeros((b, heads_c, dh_c, dh_c), F32)
                shift0 = jnp.zeros((b, d), F32)
            else:
                s0, shift0 = st["c_S"][o], st["c_shift"][o]
            (xr, xw, xk, xv, xa, xg), shift = _norm_mix(x, shift0, P["norm1_g"][l], sc1, sh1, P["c_mu"][o])
            flat = lambda z: z.reshape(m, d)
            r = _matmul(flat(xr), P["c_wr"], layer=o)
            k = _matmul(flat(xk), P["c_wk"], layer=o)
            v = _matmul(flat(xv), P["c_wv"], layer=o)
            lw = _lora(flat(xw), P["c_w1"][o], P["c_w2"][o], mid="tanh", epilogue="log_decay", bias=P["c_w0"][o])
            a = _lora(flat(xa), P["c_a1"][o], P["c_a2"][o], mid="none", epilogue="sigmoid", bias=P["c_a0"][o])
            g = _lora(flat(xg), P["c_g1"][o], P["c_g2"][o], mid="sigmoid", epilogue="none")
            if v_first is None:
                v_first = v
            else:
                v = _lora(flat(xv), P["c_v1"][o - 1], P["c_v2"][o - 1], mid="none", epilogue="vmix",
                          bias=P["c_v0"][o - 1], extra=(v, v_first))
            y, S = _rwkv(r, lw, k, v, a, g, P["c_k_k"][o], P["c_k_a"][o], P["c_r_k"][o], P["c_lnx_g"][o],
                         P["c_lnx_b"][o], s0, batch=b, seq=t)
            c_S.append(S)
            c_sh.append(shift)
            x = _matmul_resid([y], P["c_wo"], x.reshape(m, d), gate_rows(gt1.reshape(b, d)), layer=o)
        x = x.reshape(b, t, d)
        h = _norm_mod(x, P["norm2_g"][l], sc2, sh2).reshape(m, d)
        hid = _matmul_swiglu(h, P["ffn_w1"], P["ffn_w3"], layer=l)
        x = _matmul_resid([hid], P["ffn_w2"], x.reshape(m, d), gate_rows(gt2.reshape(b, d)), layer=l,
                          tm_pref=1024, tn_pref=256).reshape(b, t, d)
    y = _final_norm(x, P["final_g"])
    return (y, jnp.stack(a_C), jnp.stack(a_n), jnp.stack(a_m), jnp.stack(b_k), jnp.stack(b_v),
            jnp.stack(c_S), jnp.stack(c_sh))


def kernel(x_prompt, x_sample, c_prompt, c_sample, state_a_C, state_a_n, state_a_m, cache_b_k, cache_b_v, state_c_S, state_c_shift, ada_w, ada_b, norm1_g, norm2_g, final_g, ab_w_in, ab_b_if, a_norm_g, b_rel_bias, ab_w_out, c_mu, c_wr, c_wk, c_wv, c_wo, c_w0, c_w1, c_w2, c_a0, c_a1, c_a2, c_v0, c_v1, c_v2, c_g1, c_g2, c_k_k, c_k_a, c_r_k, c_lnx_g, c_lnx_b, ffn_w1, ffn_w3, ffn_w2):
    n_ab, dec_b, heads_a, dk, dv = state_a_C.shape
    _, _, width, heads_b, dh_b = cache_b_k.shape
    w_a = heads_a * dk
    gate_lo = 2 * w_a + 2 * heads_a * dv
    gate_hi = gate_lo + 2 * heads_a
    P = dict(ada_w=ada_w, norm1_g=norm1_g, norm2_g=norm2_g, final_g=final_g,
             ab_w_a=ab_w_in[:, :, :gate_lo], ab_w_gate=ab_w_in[:, :, gate_lo:gate_hi], ab_w_b=ab_w_in[:, :, gate_hi:],
             ab_b_if=ab_b_if, a_norm_g=a_norm_g,
             bias_table=jnp.stack([_rel_bias_table(b_rel_bias[e]) for e in range(n_ab)]),
             ab_w_out=ab_w_out, c_mu=c_mu, c_wr=c_wr, c_wk=c_wk, c_wv=c_wv, c_wo=c_wo,
             c_w0=c_w0, c_w1=c_w1, c_w2=c_w2, c_a0=c_a0, c_a1=c_a1, c_a2=c_a2, c_v0=c_v0, c_v1=c_v1, c_v2=c_v2,
             c_g1=c_g1, c_g2=c_g2, c_k_k=c_k_k, c_k_a=c_k_a, c_r_k=c_r_k, c_lnx_g=c_lnx_g, c_lnx_b=c_lnx_b,
             ffn_w1=ffn_w1, ffn_w3=ffn_w3, ffn_w2=ffn_w2,
             a_shape=state_a_C.shape[1:], b_shape=(heads_b, dh_b), c_shape=state_c_S.shape[1:])
    st = dict(a_C=state_a_C, a_n=state_a_n, a_m=state_a_m,
              b_k=cache_b_k.reshape(n_ab, dec_b, width, heads_b * dh_b),
              b_v=cache_b_v.reshape(n_ab, dec_b, width, heads_b * dh_b),
              c_S=state_c_S, c_shift=state_c_shift)

    n_p, n_s = c_prompt.shape[0], c_sample.shape[0]
    rows = -(-(n_p + n_s) // 16) * 16
    c_all = jnp.concatenate([c_prompt, c_sample, jnp.zeros((rows - n_p - n_s, c_prompt.shape[1]), F32)], axis=0)
    mod = _ada_mod(c_all, ada_w, ada_b)

    outs_p = _trunk(x_prompt, mod[:, :n_p], P, None)
    outs_s = _trunk(x_sample, mod[:, n_p:n_p + n_s], P, st)
    return (outs_p[0], outs_s[0]) + tuple(outs_p[1:]) + tuple(outs_s[1:])
```

```python
import functools

import numpy as np
import jax
import jax.numpy as jnp
from jax import lax
from jax.experimental import pallas as pl
from jax.experimental.pallas import tpu as pltpu

F32 = jnp.float32
BF16 = jnp.bfloat16

CHUNK = 64
N_PREV_CHUNKS = 8
BAND_PAST = N_PREV_CHUNKS * CHUNK
PAST_LEN = 2048
REL_MAX = 2 * CHUNK
GATE_CAP = 15.0
EPS = 1e-6
GN_EPS = 64e-5

V7X_VMEM_BYTES = 64 * 1024 * 1024
VMEM_LIMIT = V7X_VMEM_BYTES - 8 * 1024 * 1024
LANES = 128
MATMUL_ROW_SUBTILE = 1024


def _params(*sem):
    return pltpu.CompilerParams(dimension_semantics=sem, vmem_limit_bytes=VMEM_LIMIT)


def _tile(n, pref, mult):
    if n <= pref:
        return n
    t = (pref // mult) * mult
    while t >= mult:
        if n % t == 0:
            return t
        t -= mult
    return n


def _silu(x):
    return x * jax.nn.sigmoid(x)


def _resident(block_shape, index_map):
    return pl.BlockSpec(block_shape, index_map, pipeline_mode=pl.Buffered(1))


def _ada_kernel(c_ref, w_ref, b_ref, o_ref):
    cs = _silu(c_ref[...]).astype(BF16)
    acc = jnp.dot(cs, w_ref[...].astype(BF16), preferred_element_type=F32)
    o_ref[...] = acc + b_ref[...]


def _ada_mod(c_all, ada_w, ada_b):
    depth, d, n = ada_w.shape
    rows = c_all.shape[0]
    tn = _tile(n, 512, LANES)
    return pl.pallas_call(
        _ada_kernel,
        grid=(depth, n // tn),
        in_specs=[
            pl.BlockSpec((rows, d), lambda l, j: (0, 0)),
            pl.BlockSpec((None, d, tn), lambda l, j: (l, 0, j)),
            pl.BlockSpec((None, 1, tn), lambda l, j: (l, 0, j)),
        ],
        out_specs=pl.BlockSpec((None, rows, tn), lambda l, j: (l, 0, j)),
        out_shape=jax.ShapeDtypeStruct((depth, rows, n), F32),
        compiler_params=_params("arbitrary", "arbitrary"),
        name="ada_mod",
    )(c_all, ada_w, ada_b.reshape(depth, 1, n))


def _norm_mod_value(x, g, sc, sh):
    y = x * lax.rsqrt(jnp.mean(x * x, axis=-1, keepdims=True) + EPS)
    return (y * g) * (1.0 + sc) + sh


def _norm_mod_kernel(x_ref, g_ref, sc_ref, sh_ref, o_ref):
    o_ref[0] = _norm_mod_value(x_ref[0], g_ref[...], sc_ref[0], sh_ref[0]).astype(o_ref.dtype)


def _norm_mod(x, g, sc, sh, out_dtype=BF16):
    b, t, d = x.shape
    tt = _tile(t, 256, 16)
    row = pl.BlockSpec((1, tt, d), lambda bi, i: (bi, i, 0))
    per_batch = pl.BlockSpec((1, 1, d), lambda bi, i: (bi, 0, 0))
    return pl.pallas_call(
        _norm_mod_kernel,
        grid=(b, t // tt),
        in_specs=[row, pl.BlockSpec((1, d), lambda bi, i: (0, 0)), per_batch, per_batch],
        out_specs=row,
        out_shape=jax.ShapeDtypeStruct((b, t, d), out_dtype),
        compiler_params=_params("arbitrary", "arbitrary"),
        name="norm_mod",
    )(x, g.reshape(1, d), sc, sh)


def _final_norm_kernel(x_ref, g_ref, o_ref):
    x = x_ref[0]
    o_ref[0] = (x * lax.rsqrt(jnp.mean(x * x, axis=-1, keepdims=True) + EPS)) * g_ref[...]


def _final_norm(x, g):
    b, t, d = x.shape
    tt = _tile(t, 256, 8)
    row = pl.BlockSpec((1, tt, d), lambda bi, i: (bi, i, 0))
    return pl.pallas_call(
        _final_norm_kernel,
        grid=(b, t // tt),
        in_specs=[row, pl.BlockSpec((1, d), lambda bi, i: (0, 0))],
        out_specs=row,
        out_shape=jax.ShapeDtypeStruct((b, t, d), F32),
        compiler_params=_params("arbitrary", "arbitrary"),
        name="final_norm",
    )(x, g.reshape(1, d))


def _norm_mix_kernel(x_ref, xp_ref, s0_ref, g_ref, sc_ref, sh_ref, mu_ref, *out_refs):
    i = pl.program_id(1)
    mix_refs, last_ref = out_refs[:6], out_refs[6]
    tt, d = x_ref.shape[1], x_ref.shape[2]
    x, xp = x_ref[0], xp_ref[0]
    inv = lax.rsqrt(jnp.mean(x * x, axis=-1, keepdims=True) + EPS)
    inv_p = lax.rsqrt(jnp.mean(xp * xp, axis=-1, keepdims=True) + EPS)
    cw = LANES if d % LANES == 0 else d
    row_id = lax.broadcasted_iota(jnp.int32, (tt, cw), 0)
    for c in range(d // cw):
        cols = slice(c * cw, (c + 1) * cw)
        g, sc, sh = g_ref[:, cols], sc_ref[0, :, cols], sh_ref[0, :, cols]
        h = ((x_ref[0, :, cols] * inv) * g) * (1.0 + sc) + sh
        hp = ((xp_ref[0, :, cols] * inv_p) * g) * (1.0 + sc) + sh
        prev_row = jnp.where(i == 0, s0_ref[0, :, cols], hp[7:8, :])
        shifted = jnp.where(row_id == 0, prev_row, pltpu.roll(h, 1, axis=0))
        xx = shifted - h
        for j in range(6):
            mix_refs[j][0, :, cols] = (h + xx * mu_ref[j:j + 1, cols]).astype(BF16)
        last_ref[0, :, cols] = h[tt - 8:, :]


def _norm_mix(x, shift0, g, sc, sh, mu):
    b, t, d = x.shape
    tt = _tile(t, 256, 16)
    row = pl.BlockSpec((1, tt, d), lambda bi, i: (bi, i, 0))
    prev8 = pl.BlockSpec((1, 8, d), lambda bi, i: (bi, jnp.maximum(i * (tt // 8) - 1, 0), 0))
    per_batch = pl.BlockSpec((1, 1, d), lambda bi, i: (bi, 0, 0))
    outs = pl.pallas_call(
        _norm_mix_kernel,
        grid=(b, t // tt),
        in_specs=[row, prev8, per_batch, pl.BlockSpec((1, d), lambda bi, i: (0, 0)), per_batch, per_batch,
                  pl.BlockSpec((6, d), lambda bi, i: (0, 0))],
        out_specs=[row] * 6 + [pl.BlockSpec((1, 8, d), lambda bi, i: (bi, 0, 0))],
        out_shape=[jax.ShapeDtypeStruct((b, t, d), BF16)] * 6 + [jax.ShapeDtypeStruct((b, 8, d), F32)],
        compiler_params=_params("arbitrary", "arbitrary"),
        name="norm_mix",
    )(x, x, shift0.reshape(b, 1, d), g.reshape(1, d), sc, sh, mu)
    return outs[:6], outs[6][:, 7, :]


def _mm_rows(tm):
    sub = MATMUL_ROW_SUBTILE if tm % MATMUL_ROW_SUBTILE == 0 else tm
    return sub, tm // sub


def _row_slice(m, sub):
    return pl.ds(m * sub, sub) if isinstance(m, int) else pl.ds(pl.multiple_of(m * sub, sub), sub)


def _for_row_subtiles(n_sub, body):
    if n_sub <= 2:
        for m in range(n_sub):
            body(m, 0)
    else:
        lax.fori_loop(0, n_sub, body, 0)


def _mm_kernel(a_ref, w_ref, o_ref, wb_ref, *, tm, w_transposed):
    wb_ref[...] = w_ref[...].astype(BF16)
    sub, n_sub = _mm_rows(tm)

    def body(m, carry):
        rows = _row_slice(m, sub)
        if w_transposed:
            acc = lax.dot_general(a_ref[rows, :], wb_ref[...], (((1,), (1,)), ((), ())),
                                  preferred_element_type=F32)
        else:
            acc = jnp.dot(a_ref[rows, :], wb_ref[...], preferred_element_type=F32)
        o_ref[rows, :] = acc.astype(o_ref.dtype)
        return carry

    _for_row_subtiles(n_sub, body)


def _mm_resid_kernel(*refs, tm, parts):
    a_refs, w_refs = refs[:parts], refs[parts:2 * parts]
    r_ref, g_ref, o_ref = refs[2 * parts:2 * parts + 3]
    wb_refs = refs[2 * parts + 3:]
    for w_ref, wb_ref in zip(w_refs, wb_refs):
        wb_ref[...] = w_ref[...].astype(BF16)
    sub, n_sub = _mm_rows(tm)
    per_row_gate = g_ref.shape[0] != 1

    def body(m, carry):
        rows = _row_slice(m, sub)
        acc = jnp.dot(a_refs[0][rows, :], wb_refs[0][...], preferred_element_type=F32)
        for a_ref, wb_ref in zip(a_refs[1:], wb_refs[1:]):
            acc = acc + jnp.dot(a_ref[rows, :], wb_ref[...], preferred_element_type=F32)
        gate = g_ref[rows, :] if per_row_gate else g_ref[...]
        o_ref[rows, :] = r_ref[rows, :] + gate * acc
        return carry

    _for_row_subtiles(n_sub, body)


def _mm_swiglu_kernel(a_ref, w1_ref, w3_ref, o_ref, w1b_ref, w3b_ref, *, tm):
    w1b_ref[...] = w1_ref[...].astype(BF16)
    w3b_ref[...] = w3_ref[...].astype(BF16)
    sub, n_sub = _mm_rows(tm)

    def body(m, carry):
        rows = _row_slice(m, sub)
        a = a_ref[rows, :]
        u = jnp.dot(a, w1b_ref[...], preferred_element_type=F32)
        v = jnp.dot(a, w3b_ref[...], preferred_element_type=F32)
        o_ref[rows, :] = (_silu(u) * v).astype(o_ref.dtype)
        return carry

    _for_row_subtiles(n_sub, body)


def _weight_spec(w, layer, k, tn):
    if w.ndim == 3:
        return pl.BlockSpec((None, k, tn), lambda i, j: (layer, 0, j))
    return pl.BlockSpec((k, tn), lambda i, j: (0, j))


def _matmul(a, w, *, layer=0, out_dtype=F32, tm_pref=2048, tn_pref=512):
    m, k = a.shape
    n = w.shape[-1]
    tm = _tile(m, tm_pref, MATMUL_ROW_SUBTILE)
    tn = _tile(n, tn_pref, LANES)
    return pl.pallas_call(
        functools.partial(_mm_kernel, tm=tm, w_transposed=False),
        grid=(m // tm, n // tn),
        in_specs=[_resident((tm, k), lambda i, j: (i, 0)), _weight_spec(w, layer, k, tn)],
        out_specs=pl.BlockSpec((tm, tn), lambda i, j: (i, j)),
        out_shape=jax.ShapeDtypeStruct((m, n), out_dtype),
        scratch_shapes=[pltpu.VMEM((k, tn), BF16)],
        compiler_params=_params("arbitrary", "arbitrary"),
        name="matmul",
    )(a, w)


def _matmul_wt(a, wt, *, layer, row0, n, tm_pref=2048, tn_pref=512):
    m, k = a.shape
    tm = _tile(m, tm_pref, MATMUL_ROW_SUBTILE)
    tn = _tile(n, tn_pref, LANES)
    assert row0 % tn == 0 and n % tn == 0
    return pl.pallas_call(
        functools.partial(_mm_kernel, tm=tm, w_transposed=True),
        grid=(m // tm, n // tn),
        in_specs=[_resident((tm, k), lambda i, j: (i, 0)),
                  pl.BlockSpec((None, tn, k), lambda i, j: (layer, row0 // tn + j, 0))],
        out_specs=pl.BlockSpec((tm, tn), lambda i, j: (i, j)),
        out_shape=jax.ShapeDtypeStruct((m, n), F32),
        scratch_shapes=[pltpu.VMEM((tn, k), BF16)],
        compiler_params=_params("arbitrary", "arbitrary"),
        name="matmul_wt",
    )(a, wt)


def _matmul_resid(a_parts, w, res, gate, *, layer, tm_pref=2048, tn_pref=512):
    parts = len(a_parts)
    m = a_parts[0].shape[0]
    ks = [a.shape[1] for a in a_parts]
    offs = [sum(ks[:p]) for p in range(parts)]
    n = w.shape[-1]
    assert w.shape[1] == sum(ks) and all(off % k == 0 for off, k in zip(offs, ks))
    tm = _tile(m, tm_pref, MATMUL_ROW_SUBTILE)
    tn = _tile(n, tn_pref, LANES)
    if gate.shape[0] == 1:
        gate_spec = pl.BlockSpec((1, tn), lambda i, j: (0, j))
    else:
        gate_spec = pl.BlockSpec((tm, tn), lambda i, j: (i, j))
    w_specs = [pl.BlockSpec((None, k, tn), functools.partial(lambda i, j, blk: (layer, blk, j), blk=off // k))
               for off, k in zip(offs, ks)]
    return pl.pallas_call(
        functools.partial(_mm_resid_kernel, tm=tm, parts=parts),
        grid=(m // tm, n // tn),
        in_specs=[_resident((tm, k), lambda i, j: (i, 0)) for k in ks] + w_specs
                 + [pl.BlockSpec((tm, tn), lambda i, j: (i, j)), gate_spec],
        out_specs=pl.BlockSpec((tm, tn), lambda i, j: (i, j)),
        out_shape=jax.ShapeDtypeStruct((m, n), F32),
        scratch_shapes=[pltpu.VMEM((k, tn), BF16) for k in ks],
        compiler_params=_params("arbitrary", "arbitrary"),
        name="matmul_resid",
    )(*a_parts, *([w] * parts), res, gate)


def _matmul_swiglu(a, w1, w3, *, layer, tm_pref=2048, tn_pref=256):
    m, k = a.shape
    n = w1.shape[-1]
    tm = _tile(m, tm_pref, MATMUL_ROW_SUBTILE)
    tn = _tile(n, tn_pref, LANES)
    return pl.pallas_call(
        functools.partial(_mm_swiglu_kernel, tm=tm),
        grid=(m // tm, n // tn),
        in_specs=[_resident((tm, k), lambda i, j: (i, 0)), _weight_spec(w1, layer, k, tn),
                  _weight_spec(w3, layer, k, tn)],
        out_specs=pl.BlockSpec((tm, tn), lambda i, j: (i, j)),
        out_shape=jax.ShapeDtypeStruct((m, n), BF16),
        scratch_shapes=[pltpu.VMEM((k, tn), BF16), pltpu.VMEM((k, tn), BF16)],
        compiler_params=_params("arbitrary", "arbitrary"),
        name="matmul_swiglu",
    )(a, w1, w3)


def _lora_kernel(a_ref, w1_ref, w2_ref, *rest, mid, epilogue):
    o_ref = rest[-1]
    z = jnp.dot(a_ref[...], w1_ref[...], preferred_element_type=F32)
    if mid == "tanh":
        z = jnp.tanh(z)
    elif mid == "sigmoid":
        z = jax.nn.sigmoid(z)
    z = jnp.dot(z.astype(BF16), w2_ref[...], preferred_element_type=F32)
    if epilogue == "log_decay":
        o_ref[...] = (-float(np.exp(-0.5))) * jax.nn.sigmoid(rest[0][...] + z)
    elif epilogue == "sigmoid":
        o_ref[...] = jax.nn.sigmoid(rest[0][...] + z)
    elif epilogue == "vmix":
        v, vf = rest[1][...], rest[2][...]
        o_ref[...] = v + (vf - v) * jax.nn.sigmoid(rest[0][...] + z)
    else:
        o_ref[...] = z


def _lora(a, w1, w2, *, mid, epilogue, bias=None, extra=()):
    m, d = a.shape
    r = w1.shape[1]
    rp = -(-r // LANES) * LANES
    w1p = jnp.pad(w1, ((0, 0), (0, rp - r))).astype(BF16)
    w2p = jnp.pad(w2, ((0, rp - r), (0, 0))).astype(BF16)
    n = w2.shape[1]
    tm = _tile(m, 256, 16)
    row = pl.BlockSpec((tm, n), lambda i: (i, 0))
    in_specs = [pl.BlockSpec((tm, d), lambda i: (i, 0)), _resident((d, rp), lambda i: (0, 0)),
                _resident((rp, n), lambda i: (0, 0))]
    args = [a, w1p, w2p]
    if bias is not None:
        in_specs.append(pl.BlockSpec((1, n), lambda i: (0, 0)))
        args.append(bias.reshape(1, n))
    for e in extra:
        in_specs.append(row)
        args.append(e)
    return pl.pallas_call(
        functools.partial(_lora_kernel, mid=mid, epilogue=epilogue),
        grid=(m // tm,),
        in_specs=in_specs,
        out_specs=row,
        out_shape=jax.ShapeDtypeStruct((m, n), F32),
        compiler_params=_params("arbitrary"),
        name="lora_" + epilogue,
    )(*args)


def _dot_nt(a, b):
    return lax.dot_general(a, b, (((1,), (1,)), ((), ())), preferred_element_type=F32)


def _dot_tn(a, b):
    return lax.dot_general(a, b, (((0,), (0,)), ((), ())), preferred_element_type=F32)


MLSTM_CHUNK = 256


def _soft_cap(x):
    return GATE_CAP * jnp.tanh(x / GATE_CAP)


def _mlstm_kernel(q_ref, k_ref, v_ref, o_ref, gc_ref, gr_ref, bc_ref, br_ref, ng_ref, c0_ref, n0_ref, m0_ref,
                  h_ref, c_ref, n_ref, m_ref, *, heads, dk, dv, length):
    c_idx = pl.program_id(1)

    @pl.when(c_idx == 0)
    def _():
        c_ref[...] = c0_ref[...]
        n_ref[...] = n0_ref[...]
        m_ref[...] = m0_ref[...]

    gcol = gc_ref[...] + bc_ref[...]
    grow = gr_ref[...] + br_ref[...]
    li_col, lf_col = _soft_cap(gcol[:, :heads]), jax.nn.log_sigmoid(_soft_cap(gcol[:, heads:]))
    li_row, lf_row = _soft_cap(grow[:heads, :]), jax.nn.log_sigmoid(_soft_cap(grow[heads:, :]))
    r_id = lax.broadcasted_iota(jnp.int32, (length, length), 0)
    c_id = lax.broadcasted_iota(jnp.int32, (length, length), 1)
    tril = c_id <= r_id
    scale = dk ** -0.5

    hs = range(heads)
    gates = []
    for h in hs:
        lf_r, li_r = lf_row[h:h + 1, :], li_row[h:h + 1, :]
        lf_c, li_c = lf_col[:, h:h + 1], li_col[:, h:h + 1]
        b_col = jnp.sum(jnp.where(tril, lf_r, 0.0), axis=1, keepdims=True)
        b_row = jnp.sum(jnp.where(r_id <= c_id, lf_c, 0.0), axis=0, keepdims=True)
        b_last = b_col[length - 1:length, :]
        m_prev = m_ref[0, h:h + 1, :]
        log_d = jnp.where(tril, b_col - b_row + li_r, -jnp.inf)
        inter = b_col + m_prev
        m_t = jnp.maximum(inter, jnp.max(log_d, axis=1, keepdims=True))
        m_new = m_t[length - 1:length, :]
        gates.append(dict(p=jnp.exp(log_d - m_t), g=jnp.exp(inter - m_t), m_t=m_t, m_new=m_new,
                          wk_col=jnp.exp(b_last - b_col + li_c - m_new),
                          decay=jnp.exp(b_last + m_prev - m_new)))

    q = [q_ref[:, h * dk:(h + 1) * dk] for h in hs]
    k = [k_ref[:, h * dk:(h + 1) * dk] * scale for h in hs]
    kw = [k[h] * gates[h]["wk_col"] for h in hs]
    qb = [x.astype(BF16) for x in q]
    kb = [x.astype(BF16) for x in k]
    vb = [v_ref[:, h * dv:(h + 1) * dv].astype(BF16) for h in hs]
    c_state = [c_ref[0, h] for h in hs]
    n_state = [n_ref[0, h:h + 1, :] for h in hs]
    s_qk = [_dot_nt(qb[h], kb[h]) for h in hs]
    q_c = [jnp.dot(qb[h], c_state[h].astype(BF16), preferred_element_type=F32) for h in hs]
    kv = [_dot_tn(kw[h].astype(BF16), vb[h]) for h in hs]
    wqk = [gates[h]["p"] * s_qk[h] for h in hs]
    num = [jnp.dot(wqk[h].astype(BF16), vb[h], preferred_element_type=F32) + gates[h]["g"] * q_c[h] for h in hs]

    for h in hs:
        g = gates[h]
        den = jnp.sum(wqk[h], axis=1, keepdims=True) + g["g"] * jnp.sum(q[h] * n_state[h], axis=1, keepdims=True)
        hh = num[h] / jnp.maximum(jnp.abs(den), jnp.exp(-g["m_t"]))
        hh = hh * lax.rsqrt(jnp.mean(hh * hh, axis=1, keepdims=True) + EPS)
        hh = hh * ng_ref[:, h * dv:(h + 1) * dv]
        h_ref[:, h * dv:(h + 1) * dv] = (jax.nn.sigmoid(o_ref[:, h * dv:(h + 1) * dv]) * hh).astype(h_ref.dtype)
        c_ref[0, h] = g["decay"] * c_state[h] + kv[h]
        n_ref[0, h:h + 1, :] = g["decay"] * n_state[h] + jnp.sum(kw[h], axis=0, keepdims=True)
        m_ref[0, h:h + 1, :] = g["m_new"]


def _mlstm(proj, gif, b_if, a_norm_g, c0, n0, m0, *, batch, seq):
    _, heads, dk, dv = c0.shape
    length = _tile(seq, MLSTM_CHUNK, CHUNK)
    nc = seq // length
    wq, wv = heads * dk, heads * dv
    assert wq == wv
    gcol = gif
    grow = gif.reshape(batch, nc, length, 2 * heads).transpose(0, 1, 3, 2)
    row = lambda col: pl.BlockSpec((length, wq), lambda b, c: (b * nc + c, col))
    state4 = pl.BlockSpec((1, heads, dk, dv), lambda b, c: (b, 0, 0, 0))
    state3 = pl.BlockSpec((1, heads, dk), lambda b, c: (b, 0, 0))
    state_m = pl.BlockSpec((1, heads, 1), lambda b, c: (b, 0, 0))
    h, c_out, n_out, m_out = pl.pallas_call(
        functools.partial(_mlstm_kernel, heads=heads, dk=dk, dv=dv, length=length),
        grid=(batch, nc),
        in_specs=[row(0), row(1), row(2), row(3),
                  pl.BlockSpec((length, 2 * heads), lambda b, c: (b * nc + c, 0)),
                  pl.BlockSpec((None, None, 2 * heads, length), lambda b, c: (b, c, 0, 0)),
                  pl.BlockSpec((1, 2 * heads), lambda b, c: (0, 0)),
                  pl.BlockSpec((2 * heads, 1), lambda b, c: (0, 0)),
                  pl.BlockSpec((1, wv), lambda b, c: (0, 0)),
                  state4, state3, state_m],
        out_specs=[pl.BlockSpec((length, wv), lambda b, c: (b * nc + c, 0)), state4, state3, state_m],
        out_shape=[jax.ShapeDtypeStruct((batch * seq, wv), BF16),
                   jax.ShapeDtypeStruct((batch, heads, dk, dv), F32),
                   jax.ShapeDtypeStruct((batch, heads, dk), F32),
                   jax.ShapeDtypeStruct((batch, heads, 1), F32)],
        compiler_params=_params("arbitrary", "arbitrary"),
        name="mlstm",
    )(proj, proj, proj, proj, gcol, grow, b_if.reshape(1, 2 * heads), b_if.reshape(2 * heads, 1),
      a_norm_g.reshape(1, wv), c0, n0, m0.reshape(batch, heads, 1))
    return h, c_out, n_out, m_out.reshape(batch, heads)


def _rel_bias_table(rel_bias):
    heads, rel_size = rel_bias.shape
    assert rel_size == CHUNK + REL_MAX
    band = BAND_PAST + CHUNK
    n_dist = band + CHUNK - 1
    ext = jnp.concatenate([rel_bias, jnp.broadcast_to(rel_bias[:, -1:], (heads, n_dist - rel_size))], axis=1)
    rev = ext[:, ::-1]
    return jnp.stack([rev[:, CHUNK - 1 - i:CHUNK - 1 - i + band] for i in range(CHUNK)], axis=1)


def _band_chunks(qs, ks, vs, biases, scale):
    n = range(len(qs))
    s = [_dot_nt(qs[i], ks[i]) * scale + biases[i] for i in n]
    p = [jnp.exp(s[i] - jnp.max(s[i], axis=1, keepdims=True)) for i in n]
    o = [jnp.dot(p[i].astype(BF16), vs[i], preferred_element_type=F32) for i in n]
    return [o[i] / jnp.sum(p[i], axis=1, keepdims=True) for i in n]


ATTN_CHUNKS_PER_ITER = 4


def _attn_prompt_kernel(q_ref, k_ref, v_ref, bias_ref, o_ref, kb_ref, vb_ref, *, seq, scale):
    kb_ref[...] = k_ref[...].astype(BF16)
    vb_ref[...] = v_ref[...].astype(BF16)
    nc = seq // CHUNK
    band = BAND_PAST + CHUNK
    bias = bias_ref[0]

    lead = list(range(min(N_PREV_CHUNKS, nc)))
    for c0 in range(0, len(lead), ATTN_CHUNKS_PER_ITER):
        cs = lead[c0:c0 + ATTN_CHUNKS_PER_ITER]
        widths = [(c + 1) * CHUNK for c in cs]
        outs = _band_chunks([q_ref[c * CHUNK:(c + 1) * CHUNK, :].astype(BF16) for c in cs],
                            [kb_ref[0:w, :] for w in widths], [vb_ref[0:w, :] for w in widths],
                            [bias[:, band - w:] for w in widths], scale)
        for c, o in zip(cs, outs):
            o_ref[c * CHUNK:(c + 1) * CHUNK, :] = o.astype(o_ref.dtype)

    rest = nc - N_PREV_CHUNKS
    per_iter = ATTN_CHUNKS_PER_ITER if rest % ATTN_CHUNKS_PER_ITER == 0 else 1

    def body(it, carry):
        starts = [pl.multiple_of((N_PREV_CHUNKS + it * per_iter + u) * CHUNK, CHUNK) for u in range(per_iter)]
        k_rows = [pl.ds(pl.multiple_of(s - BAND_PAST, CHUNK), band) for s in starts]
        outs = _band_chunks([q_ref[pl.ds(s, CHUNK), :].astype(BF16) for s in starts],
                            [kb_ref[r, :] for r in k_rows], [vb_ref[r, :] for r in k_rows],
                            [bias] * per_iter, scale)
        for s, o in zip(starts, outs):
            o_ref[pl.ds(s, CHUNK), :] = o.astype(o_ref.dtype)
        return carry

    if rest > 0:
        lax.fori_loop(0, rest // per_iter, body, 0)


def _attn_prompt(proj_b, bias_table, *, batch, seq, heads, dh):
    col = lambda base: pl.BlockSpec((seq, dh), lambda b, h: (b, base * heads + h))
    return pl.pallas_call(
        functools.partial(_attn_prompt_kernel, seq=seq, scale=dh ** -0.5),
        grid=(batch, heads),
        in_specs=[col(0), col(1), col(2),
                  pl.BlockSpec((1, CHUNK, BAND_PAST + CHUNK), lambda b, h: (h, 0, 0))],
        out_specs=pl.BlockSpec((seq, dh), lambda b, h: (b, h)),
        out_shape=jax.ShapeDtypeStruct((batch * seq, heads * dh), BF16),
        scratch_shapes=[pltpu.VMEM((seq, dh), BF16), pltpu.VMEM((seq, dh), BF16)],
        compiler_params=_params("arbitrary", "arbitrary"),
        name="attn_prompt",
    )(proj_b, proj_b, proj_b, bias_table)


def _attn_sample_kernel(q_ref, k_ref, v_ref, pk_ref, pv_ref, bias_ref, o_ref, *, seq, width, scale):
    q = q_ref[...].astype(BF16)
    bias = bias_ref[0]
    off = BAND_PAST - width
    s_past = _dot_nt(q, pk_ref[...].astype(BF16)) * scale + bias[:seq, off:off + width]
    s_new = _dot_nt(q, k_ref[...].astype(BF16)) * scale + bias[:seq, BAND_PAST:BAND_PAST + seq]
    m = jnp.maximum(jnp.max(s_past, axis=1, keepdims=True), jnp.max(s_new, axis=1, keepdims=True))
    p_past, p_new = jnp.exp(s_past - m), jnp.exp(s_new - m)
    o = jnp.dot(p_past.astype(BF16), pv_ref[...].astype(BF16), preferred_element_type=F32) \
        + jnp.dot(p_new.astype(BF16), v_ref[...].astype(BF16), preferred_element_type=F32)
    denom = jnp.sum(p_past, axis=1, keepdims=True) + jnp.sum(p_new, axis=1, keepdims=True)
    o_ref[...] = (o / denom).astype(o_ref.dtype)


def _attn_sample(proj_b, past_k, past_v, layer, bias_table, *, batch, seq, heads, dh):
    width = past_k.shape[2]
    col = lambda base: pl.BlockSpec((seq, dh), lambda b, h: (b, base * heads + h))
    past = pl.BlockSpec((None, None, width, dh), lambda b, h: (layer, b, 0, h))
    return pl.pallas_call(
        functools.partial(_attn_sample_kernel, seq=seq, width=width, scale=dh ** -0.5),
        grid=(batch, heads),
        in_specs=[col(0), col(1), col(2), past, past,
                  pl.BlockSpec((1, CHUNK, BAND_PAST + CHUNK), lambda b, h: (h, 0, 0))],
        out_specs=pl.BlockSpec((seq, dh), lambda b, h: (b, h)),
        out_shape=jax.ShapeDtypeStruct((batch * seq, heads * dh), BF16),
        compiler_params=_params("arbitrary", "arbitrary"),
        name="attn_sample",
    )(proj_b, proj_b, proj_b, past_k, past_v, bias_table)


def _split_dot(a_exact, x):
    hi = x.astype(BF16)
    lo = (x - hi.astype(F32)).astype(BF16)
    return jnp.dot(a_exact, hi, preferred_element_type=F32) + jnp.dot(a_exact, lo, preferred_element_type=F32)


def _block_diag_rows(x, lo_mask):
    return jnp.concatenate([jnp.where(lo_mask, x, 0.0), jnp.where(lo_mask, 0.0, x)], axis=0).astype(BF16)


def _rwkv_kernel(r_ref, lw_ref, k_ref, v_ref, a_ref, g_ref, kk_ref, ka_ref, rk_ref, lg_ref, lb_ref, s0_ref,
                 o_ref, s_ref, s2_ref, *, pairs, dh, length, n_chunks):
    c_idx = pl.program_id(2)
    pw_ = 2 * dh
    ps = range(pairs)

    @pl.when(c_idx == 0)
    def _():
        for p in ps:
            s2_ref[p] = jnp.concatenate([s0_ref[0, 2 * p], s0_ref[0, 2 * p + 1]], axis=1)

    def lane_lo(shape, half):
        return lax.broadcasted_iota(jnp.int32, shape, 1) < half

    f_lo = lane_lo((length, pw_), dh)
    f_lo_s = lane_lo((dh, pw_), dh)
    t_lo = lane_lo((length, 2 * length), length)
    r_id = lax.broadcasted_iota(jnp.int32, (length, length), 0)
    c_id = lax.broadcasted_iota(jnp.int32, (length, length), 1)
    tril_bf = jnp.where(c_id <= r_id, 1.0, 0.0).astype(BF16)
    row2 = lax.broadcasted_iota(jnp.int32, (length, 2 * length), 0)
    col2 = lax.broadcasted_iota(jnp.int32, (length, 2 * length), 1) & (length - 1)
    strict2 = col2 < row2
    row4 = lax.broadcasted_iota(jnp.int32, (length, 4 * length), 0)
    col4 = lax.broadcasted_iota(jnp.int32, (length, 4 * length), 1) & (length - 1)
    incl4 = col4 <= row4
    n_double = max(int(np.ceil(np.log2(length))), 1)

    def head_sums(x, lo_mask):
        s_lo = jnp.sum(jnp.where(lo_mask, x, 0.0), axis=1, keepdims=True)
        s_hi = jnp.sum(jnp.where(lo_mask, 0.0, x), axis=1, keepdims=True)
        return jnp.where(lo_mask, s_lo, s_hi)

    r_all, lw_all, k_all, v_all, a_all = r_ref[...], lw_ref[...], k_ref[...], v_ref[...], a_ref[...]
    cum = _split_dot(tril_bf, lw_all)
    w_in = jnp.exp(cum)
    w_inv = jnp.exp(-cum)
    w_ex = jnp.exp(cum - lw_all)
    kk_all = k_all * kk_ref[...]
    k2_all = k_all * (1.0 + (a_all - 1.0) * ka_ref[...])
    rt_all = r_all * w_in
    kt_all = k2_all * w_inv
    ba_all = a_all * w_inv
    rk2_all = r_all * k2_all * rk_ref[...]
    sl = lambda x, p: x[:, p * pw_:(p + 1) * pw_]

    lhs, rhs, rhs_bd = [], [], []
    for p in ps:
        kk = sl(kk_all, p)
        kk = kk / jnp.maximum(jnp.sqrt(head_sums(kk * kk, f_lo)), 1e-12)
        at = (-kk) * sl(w_ex, p)
        bt = kk * sl(ba_all, p)
        kt = sl(kt_all, p)
        lhs.append(jnp.concatenate([at, sl(rt_all, p)], axis=0).astype(BF16))
        rhs.append(jnp.concatenate([bt, kt], axis=0).astype(BF16))
        rhs_bd.append(jnp.concatenate([_block_diag_rows(bt, f_lo), _block_diag_rows(kt, f_lo)], axis=0))
    s0 = [s2_ref[p] for p in ps]
    vf = [sl(v_all, p) for p in ps]
    v_bd = [_block_diag_rows(vf[p], f_lo) for p in ps]
    aals = [_dot_nt(lhs[p], jnp.concatenate([rhs_bd[p], _block_diag_rows(s0[p], f_lo_s)], axis=0)) for p in ps]
    aa = [z[:, :4 * length] for z in aals]
    ls = [z[:, 4 * length:] for z in aals]
    x = [ls[p][:length] + jnp.dot(jnp.where(strict2, aa[p][:length, 2 * length:], 0.0).astype(BF16), v_bd[p],
                                  preferred_element_type=F32) for p in ps]
    pw = [jnp.where(strict2, aa[p][:length, :2 * length], 0.0) for p in ps]
    for j in range(n_double):
        pwb = [q.astype(BF16) for q in pw]
        if j + 1 < n_double:
            both = [jnp.dot(pwb[p], jnp.concatenate([_block_diag_rows(x[p], f_lo), _block_diag_rows(pw[p], t_lo)],
                                                    axis=1), preferred_element_type=F32) for p in ps]
            x = [x[p] + both[p][:, :pw_] for p in ps]
            pw = [both[p][:, pw_:] for p in ps]
        else:
            x = [x[p] + jnp.dot(pwb[p], _block_diag_rows(x[p], f_lo), preferred_element_type=F32) for p in ps]
    uv_bd = [jnp.concatenate([_block_diag_rows(x[p], f_lo), v_bd[p]], axis=0) for p in ps]
    y = [ls[p][length:] + jnp.dot(jnp.where(incl4, aa[p][length:, :], 0.0).astype(BF16), uv_bd[p],
                                  preferred_element_type=F32) for p in ps]
    uv = [jnp.concatenate([x[p], vf[p]], axis=0).astype(BF16) for p in ps]
    ds = [_dot_tn(uv[p], rhs[p]) for p in ps]
    for p in ps:
        delta = jnp.where(f_lo_s, ds[p][:dh, :], ds[p][dh:, :])
        s2_ref[p] = (s0[p] + delta) * sl(w_in, p)[length - 1:length, :]

    inv_dh = 1.0 / dh
    for p in ps:
        cols = slice(p * pw_, (p + 1) * pw_)
        yc = y[p] - head_sums(y[p], f_lo) * inv_dh
        yn = yc * lax.rsqrt(head_sums(yc * yc, f_lo) * inv_dh + GN_EPS)
        yn = yn * lg_ref[:, cols] + lb_ref[:, cols]
        bonus = head_sums(sl(rk2_all, p), f_lo) * vf[p]
        o_ref[:, cols] = ((yn + bonus) * g_ref[:, cols]).astype(o_ref.dtype)

    @pl.when(c_idx == n_chunks - 1)
    def _():
        for p in ps:
            s_ref[0, 2 * p] = s2_ref[p][:, :dh]
            s_ref[0, 2 * p + 1] = s2_ref[p][:, dh:]


def _rwkv(r, lw, k, v, a, g, k_k, k_a, r_k, lnx_g, lnx_b, s0, *, batch, seq):
    _, heads, dh, _ = s0.shape
    d = heads * dh
    length = min(CHUNK, seq)
    nc = seq // length
    assert 2 * dh == LANES and length & (length - 1) == 0 and heads % 2 == 0
    hg = _tile(heads, 16, 2)
    wg = hg * dh
    row = pl.BlockSpec((length, wg), lambda b, gi, c: (b * nc + c, gi))
    par = pl.BlockSpec((1, wg), lambda b, gi, c: (0, gi))
    state = pl.BlockSpec((1, hg, dh, dh), lambda b, gi, c: (b, gi, 0, 0))
    return pl.pallas_call(
        functools.partial(_rwkv_kernel, pairs=hg // 2, dh=dh, length=length, n_chunks=nc),
        grid=(batch, heads // hg, nc),
        in_specs=[row] * 6 + [par] * 5 + [state],
        out_specs=[row, state],
        out_shape=[jax.ShapeDtypeStruct((batch * seq, d), BF16), jax.ShapeDtypeStruct(s0.shape, F32)],
        scratch_shapes=[pltpu.VMEM((hg // 2, dh, 2 * dh), F32)],
        compiler_params=_params("arbitrary", "arbitrary", "arbitrary"),
        name="rwkv",
    )(r, lw, k, v, a, g, k_k.reshape(1, d), k_a.reshape(1, d), r_k.reshape(1, d), lnx_g.reshape(1, d),
      lnx_b.reshape(1, d), s0)


def _trunk(x, mod, P, st):
    b, t, d = x.shape
    m = b * t
    depth = P["ada_w"].shape[0]
    _, heads_a, dk, dv = P["a_shape"]
    heads_b, dh_b = P["b_shape"]
    _, heads_c, dh_c, _ = P["c_shape"]
    w_a = heads_a * dk
    w_b = heads_b * dh_b
    a_C, a_n, a_m, b_k, b_v, c_S, c_sh = [], [], [], [], [], [], []
    v_first = None

    def gate_rows(gt):
        return gt if b == 1 else jnp.broadcast_to(gt[:, None, :], (b, t, d)).reshape(m, d)

    for l in range(depth):
        sh1, sc1, gt1, sh2, sc2, gt2 = (z.reshape(b, 1, d) for z in jnp.split(mod[l], 6, axis=-1))
        if l % 2 == 0:
            e = l // 2
            h = _norm_mod(x, P["norm1_g"][l], sc1, sh1).reshape(m, d)
            if st is None:
                c0 = jnp.zeros((b, heads_a, dk, dv), F32)
                n0 = jnp.zeros((b, heads_a, dk), F32)
                m0 = jnp.zeros((b, heads_a), F32)
            else:
                c0, n0, m0 = st["a_C"][e], st["a_n"][e], st["a_m"][e]
            proj_a = _matmul_wt(h, P["ab_wt"], layer=e, row0=0, n=4 * w_a)
            gif = _matmul_wt(h, P["ab_wt"], layer=e, row0=4 * w_a, n=2 * heads_a)
            proj_b = _matmul_wt(h, P["ab_wt_b"], layer=e, row0=0, n=3 * w_b)
            ha, C, n, mm = _mlstm(proj_a, gif, P["ab_b_if"][e], P["a_norm_g"][e], c0, n0, m0, batch=b, seq=t)
            if st is None:
                hb = _attn_prompt(proj_b, P["bias_table"][e], batch=b, seq=t, heads=heads_b, dh=dh_b)
                keep = min(BAND_PAST, t)
            else:
                hb = _attn_sample(proj_b, st["b_k"], st["b_v"], e, P["bias_table"][e],
                                  batch=b, seq=t, heads=heads_b, dh=dh_b)
                keep = t
            kept = proj_b.reshape(b, t, 3 * w_b)[:, t - keep:, :]
            a_C.append(C)
            a_n.append(n)
            a_m.append(mm)
            b_k.append(kept[:, :, w_b:2 * w_b].reshape(b, keep, heads_b, dh_b))
            b_v.append(kept[:, :, 2 * w_b:].reshape(b, keep, heads_b, dh_b))
            x = _matmul_resid([ha, hb], P["ab_w_out"], x.reshape(m, d), gate_rows(gt1.reshape(b, d)), layer=e)
        else:
            o = l // 2
            if st is None:
                s0 = jnp.zeros((b, heads_c, dh_c, dh_c), F32)
                shift0 = jnp.zeros((b, d), F32)
            else:
                s0, shift0 = st["c_S"][o], st["c_shift"][o]
            (xr, xw, xk, xv, xa, xg), shift = _norm_mix(x, shift0, P["norm1_g"][l], sc1, sh1, P["c_mu"][o])
            flat = lambda z: z.reshape(m, d)
            r = _matmul(flat(xr), P["c_wr"], layer=o)
            k = _matmul(flat(xk), P["c_wk"], layer=o)
            v = _matmul(flat(xv), P["c_wv"], layer=o)
            lw = _lora(flat(xw), P["c_w1"][o], P["c_w2"][o], mid="tanh", epilogue="log_decay", bias=P["c_w0"][o])
            a = _lora(flat(xa), P["c_a1"][o], P["c_a2"][o], mid="none", epilogue="sigmoid", bias=P["c_a0"][o])
            g = _lora(flat(xg), P["c_g1"][o], P["c_g2"][o], mid="sigmoid", epilogue="none")
            if v_first is None:
                v_first = v
            else:
                v = _lora(flat(xv), P["c_v1"][o - 1], P["c_v2"][o - 1], mid="none", epilogue="vmix",
                          bias=P["c_v0"][o - 1], extra=(v, v_first))
            y, S = _rwkv(r, lw, k, v, a, g, P["c_k_k"][o], P["c_k_a"][o], P["c_r_k"][o], P["c_lnx_g"][o],
                         P["c_lnx_b"][o], s0, batch=b, seq=t)
            c_S.append(S)
            c_sh.append(shift)
            x = _matmul_resid([y], P["c_wo"], x.reshape(m, d), gate_rows(gt1.reshape(b, d)), layer=o)
        x = x.reshape(b, t, d)
        h = _norm_mod(x, P["norm2_g"][l], sc2, sh2).reshape(m, d)
        hid = _matmul_swiglu(h, P["ffn_w1"], P["ffn_w3"], layer=l)
        x = _matmul_resid([hid], P["ffn_w2"], x.reshape(m, d), gate_rows(gt2.reshape(b, d)), layer=l,
                          tm_pref=1024, tn_pref=256).reshape(b, t, d)
    y = _final_norm(x, P["final_g"])
    return (y, jnp.stack(a_C), jnp.stack(a_n), jnp.stack(a_m), jnp.stack(b_k), jnp.stack(b_v),
            jnp.stack(c_S), jnp.stack(c_sh))


def kernel(x_prompt, x_sample, c_prompt, c_sample, state_a_C, state_a_n, state_a_m, cache_b_k, cache_b_v, state_c_S, state_c_shift, ada_w, ada_b, norm1_g, norm2_g, final_g, ab_w_in, ab_b_if, a_norm_g, b_rel_bias, ab_w_out, c_mu, c_wr, c_wk, c_wv, c_wo, c_w0, c_w1, c_w2, c_a0, c_a1, c_a2, c_v0, c_v1, c_v2, c_g1, c_g2, c_k_k, c_k_a, c_r_k, c_lnx_g, c_lnx_b, ffn_w1, ffn_w3, ffn_w2):
    n_ab, dec_b, heads_a, dk, dv = state_a_C.shape
    _, _, width, heads_b, dh_b = cache_b_k.shape
    w_a = heads_a * dk
    gate_lo = 2 * w_a + 2 * heads_a * dv
    gate_hi = gate_lo + 2 * heads_a
    assert gate_lo == 4 * w_a
    ab_wt = jnp.swapaxes(ab_w_in, 1, 2)
    P = dict(ada_w=ada_w, norm1_g=norm1_g, norm2_g=norm2_g, final_g=final_g,
             ab_wt=ab_wt, ab_wt_b=ab_wt[:, gate_hi:, :],
             ab_b_if=ab_b_if, a_norm_g=a_norm_g,
             bias_table=jnp.stack([_rel_bias_table(b_rel_bias[e]) for e in range(n_ab)]),
             ab_w_out=ab_w_out, c_mu=c_mu, c_wr=c_wr, c_wk=c_wk, c_wv=c_wv, c_wo=c_wo,
             c_w0=c_w0, c_w1=c_w1, c_w2=c_w2, c_a0=c_a0, c_a1=c_a1, c_a2=c_a2, c_v0=c_v0, c_v1=c_v1, c_v2=c_v2,
             c_g1=c_g1, c_g2=c_g2, c_k_k=c_k_k, c_k_a=c_k_a, c_r_k=c_r_k, c_lnx_g=c_lnx_g, c_lnx_b=c_lnx_b,
             ffn_w1=ffn_w1, ffn_w3=ffn_w3, ffn_w2=ffn_w2,
             a_shape=state_a_C.shape[1:], b_shape=(heads_b, dh_b), c_shape=state_c_S.shape[1:])
    st = dict(a_C=state_a_C, a_n=state_a_n, a_m=state_a_m,
              b_k=cache_b_k.reshape(n_ab, dec_b, width, heads_b * dh_b),
              b_v=cache_b_v.reshape(n_ab, dec_b, width, heads_b * dh_b),
              c_S=state_c_S, c_shift=state_c_shift)

    n_p, n_s = c_prompt.shape[0], c_sample.shape[0]
    rows = -(-(n_p + n_s) // 16) * 16
    c_all = jnp.concatenate([c_prompt, c_sample, jnp.zeros((rows - n_p - n_s, c_prompt.shape[1]), F32)], axis=0)
    mod = _ada_mod(c_all, ada_w, ada_b)

    outs_p = _trunk(x_prompt, mod[:, :n_p], P, None)
    outs_s = _trunk(x_sample, mod[:, n_p:n_p + n_s], P, st)
    return (outs_p[0], outs_s[0]) + tuple(outs_p[1:]) + tuple(outs_s[1:])
```

```python
import functools

import numpy as np
import jax
import jax.numpy as jnp
from jax import lax
from jax.experimental import pallas as pl
from jax.experimental.pallas import tpu as pltpu

F32 = jnp.float32
BF16 = jnp.bfloat16

CHUNK = 64
N_PREV_CHUNKS = 8
BAND_PAST = N_PREV_CHUNKS * CHUNK
PAST_LEN = 2048
REL_MAX = 2 * CHUNK
GATE_CAP = 15.0
EPS = 1e-6
GN_EPS = 64e-5

V7X_VMEM_BYTES = 64 * 1024 * 1024
VMEM_LIMIT = V7X_VMEM_BYTES - 8 * 1024 * 1024
LANES = 128
MATMUL_ROW_SUBTILE = 1024


def _params(*sem):
    return pltpu.CompilerParams(dimension_semantics=sem, vmem_limit_bytes=VMEM_LIMIT)


def _tile(n, pref, mult):
    if n <= pref:
        return n
    t = (pref // mult) * mult
    while t >= mult:
        if n % t == 0:
            return t
        t -= mult
    return n


def _silu(x):
    return x * jax.nn.sigmoid(x)


def _resident(block_shape, index_map):
    return pl.BlockSpec(block_shape, index_map, pipeline_mode=pl.Buffered(1))


def _ada_kernel(c_ref, w_ref, b_ref, o_ref):
    cs = _silu(c_ref[...]).astype(BF16)
    acc = jnp.dot(cs, w_ref[...].astype(BF16), preferred_element_type=F32)
    o_ref[...] = acc + b_ref[...]


def _ada_mod(c_all, ada_w, ada_b):
    depth, d, n = ada_w.shape
    rows = c_all.shape[0]
    tn = _tile(n, 512, LANES)
    return pl.pallas_call(
        _ada_kernel,
        grid=(depth, n // tn),
        in_specs=[
            pl.BlockSpec((rows, d), lambda l, j: (0, 0)),
            pl.BlockSpec((None, d, tn), lambda l, j: (l, 0, j)),
            pl.BlockSpec((None, 1, tn), lambda l, j: (l, 0, j)),
        ],
        out_specs=pl.BlockSpec((None, rows, tn), lambda l, j: (l, 0, j)),
        out_shape=jax.ShapeDtypeStruct((depth, rows, n), F32),
        compiler_params=_params("arbitrary", "arbitrary"),
        name="ada_mod",
    )(c_all, ada_w, ada_b.reshape(depth, 1, n))


def _norm_mod_value(x, g, sc, sh):
    y = x * lax.rsqrt(jnp.mean(x * x, axis=-1, keepdims=True) + EPS)
    return (y * g) * (1.0 + sc) + sh


def _norm_mod_kernel(x_ref, g_ref, sc_ref, sh_ref, o_ref):
    o_ref[0] = _norm_mod_value(x_ref[0], g_ref[...], sc_ref[0], sh_ref[0]).astype(o_ref.dtype)


def _norm_mod(x, g, sc, sh, out_dtype=BF16):
    b, t, d = x.shape
    tt = _tile(t, 256, 16)
    row = pl.BlockSpec((1, tt, d), lambda bi, i: (bi, i, 0))
    per_batch = pl.BlockSpec((1, 1, d), lambda bi, i: (bi, 0, 0))
    return pl.pallas_call(
        _norm_mod_kernel,
        grid=(b, t // tt),
        in_specs=[row, pl.BlockSpec((1, d), lambda bi, i: (0, 0)), per_batch, per_batch],
        out_specs=row,
        out_shape=jax.ShapeDtypeStruct((b, t, d), out_dtype),
        compiler_params=_params("arbitrary", "arbitrary"),
        name="norm_mod",
    )(x, g.reshape(1, d), sc, sh)


def _final_norm_kernel(x_ref, g_ref, o_ref):
    x = x_ref[0]
    o_ref[0] = (x * lax.rsqrt(jnp.mean(x * x, axis=-1, keepdims=True) + EPS)) * g_ref[...]


def _final_norm(x, g):
    b, t, d = x.shape
    tt = _tile(t, 256, 8)
    row = pl.BlockSpec((1, tt, d), lambda bi, i: (bi, i, 0))
    return pl.pallas_call(
        _final_norm_kernel,
        grid=(b, t // tt),
        in_specs=[row, pl.BlockSpec((1, d), lambda bi, i: (0, 0))],
        out_specs=row,
        out_shape=jax.ShapeDtypeStruct((b, t, d), F32),
        compiler_params=_params("arbitrary", "arbitrary"),
        name="final_norm",
    )(x, g.reshape(1, d))


def _norm_mix_kernel(x_ref, xp_ref, s0_ref, g_ref, sc_ref, sh_ref, mu_ref, *out_refs):
    i = pl.program_id(1)
    mix_refs, last_ref = out_refs[:6], out_refs[6]
    tt, d = x_ref.shape[1], x_ref.shape[2]
    x, xp = x_ref[0], xp_ref[0]
    inv = lax.rsqrt(jnp.mean(x * x, axis=-1, keepdims=True) + EPS)
    inv_p = lax.rsqrt(jnp.mean(xp * xp, axis=-1, keepdims=True) + EPS)
    cw = LANES if d % LANES == 0 else d
    row_id = lax.broadcasted_iota(jnp.int32, (tt, cw), 0)
    for c in range(d // cw):
        cols = slice(c * cw, (c + 1) * cw)
        g, sc, sh = g_ref[:, cols], sc_ref[0, :, cols], sh_ref[0, :, cols]
        h = ((x_ref[0, :, cols] * inv) * g) * (1.0 + sc) + sh
        hp = ((xp_ref[0, :, cols] * inv_p) * g) * (1.0 + sc) + sh
        prev_row = jnp.where(i == 0, s0_ref[0, :, cols], hp[7:8, :])
        shifted = jnp.where(row_id == 0, prev_row, pltpu.roll(h, 1, axis=0))
        xx = shifted - h
        for j in range(6):
            mix_refs[j][0, :, cols] = (h + xx * mu_ref[j:j + 1, cols]).astype(BF16)
        last_ref[0, :, cols] = h[tt - 8:, :]


def _norm_mix(x, shift0, g, sc, sh, mu):
    b, t, d = x.shape
    tt = _tile(t, 256, 16)
    row = pl.BlockSpec((1, tt, d), lambda bi, i: (bi, i, 0))
    prev8 = pl.BlockSpec((1, 8, d), lambda bi, i: (bi, jnp.maximum(i * (tt // 8) - 1, 0), 0))
    per_batch = pl.BlockSpec((1, 1, d), lambda bi, i: (bi, 0, 0))
    outs = pl.pallas_call(
        _norm_mix_kernel,
        grid=(b, t // tt),
        in_specs=[row, prev8, per_batch, pl.BlockSpec((1, d), lambda bi, i: (0, 0)), per_batch, per_batch,
                  pl.BlockSpec((6, d), lambda bi, i: (0, 0))],
        out_specs=[row] * 6 + [pl.BlockSpec((1, 8, d), lambda bi, i: (bi, 0, 0))],
        out_shape=[jax.ShapeDtypeStruct((b, t, d), BF16)] * 6 + [jax.ShapeDtypeStruct((b, 8, d), F32)],
        compiler_params=_params("arbitrary", "arbitrary"),
        name="norm_mix",
    )(x, x, shift0.reshape(b, 1, d), g.reshape(1, d), sc, sh, mu)
    return outs[:6], outs[6][:, 7, :]


def _mm_rows(tm):
    sub = MATMUL_ROW_SUBTILE if tm % MATMUL_ROW_SUBTILE == 0 else tm
    return sub, tm // sub


def _row_slice(m, sub):
    return pl.ds(m * sub, sub) if isinstance(m, int) else pl.ds(pl.multiple_of(m * sub, sub), sub)


def _for_row_subtiles(n_sub, body):
    if n_sub <= 2:
        for m in range(n_sub):
            body(m, 0)
    else:
        lax.fori_loop(0, n_sub, body, 0)


def _emit_bf16_weights(wb_refs, wq_refs):
    @pl.when(pl.program_id(0) == 0)
    def _():
        for wb_ref, wq_ref in zip(wb_refs, wq_refs):
            wq_ref[...] = wb_ref[...]


def _mm_kernel(a_ref, w_ref, o_ref, *rest, tm, w_transposed, emit):
    wb_ref = rest[-1]
    wb_ref[...] = w_ref[...].astype(BF16)
    sub, n_sub = _mm_rows(tm)

    def body(m, carry):
        rows = _row_slice(m, sub)
        if w_transposed:
            acc = lax.dot_general(a_ref[rows, :], wb_ref[...], (((1,), (1,)), ((), ())),
                                  preferred_element_type=F32)
        else:
            acc = jnp.dot(a_ref[rows, :], wb_ref[...], preferred_element_type=F32)
        o_ref[rows, :] = acc.astype(o_ref.dtype)
        return carry

    _for_row_subtiles(n_sub, body)
    if emit:
        _emit_bf16_weights([wb_ref], [rest[0]])


def _mm_resid_kernel(*refs, tm, parts):
    a_refs, w_refs = refs[:parts], refs[parts:2 * parts]
    r_ref, g_ref, o_ref = refs[2 * parts:2 * parts + 3]
    wb_refs = refs[2 * parts + 3:]
    for w_ref, wb_ref in zip(w_refs, wb_refs):
        wb_ref[...] = w_ref[...].astype(BF16)
    sub, n_sub = _mm_rows(tm)
    per_row_gate = g_ref.shape[0] != 1

    def body(m, carry):
        rows = _row_slice(m, sub)
        acc = jnp.dot(a_refs[0][rows, :], wb_refs[0][...], preferred_element_type=F32)
        for a_ref, wb_ref in zip(a_refs[1:], wb_refs[1:]):
            acc = acc + jnp.dot(a_ref[rows, :], wb_ref[...], preferred_element_type=F32)
        gate = g_ref[rows, :] if per_row_gate else g_ref[...]
        o_ref[rows, :] = r_ref[rows, :] + gate * acc
        return carry

    _for_row_subtiles(n_sub, body)


def _mm_swiglu_kernel(a_ref, w1_ref, w3_ref, o_ref, *rest, tm, emit):
    w1b_ref, w3b_ref = rest[-2:]
    w1b_ref[...] = w1_ref[...].astype(BF16)
    w3b_ref[...] = w3_ref[...].astype(BF16)
    sub, n_sub = _mm_rows(tm)

    def body(m, carry):
        rows = _row_slice(m, sub)
        a = a_ref[rows, :]
        u = jnp.dot(a, w1b_ref[...], preferred_element_type=F32)
        v = jnp.dot(a, w3b_ref[...], preferred_element_type=F32)
        o_ref[rows, :] = (_silu(u) * v).astype(o_ref.dtype)
        return carry

    _for_row_subtiles(n_sub, body)
    if emit:
        _emit_bf16_weights([w1b_ref, w3b_ref], rest[:2])


def _parked(n_tiles):
    return lambda i, j: jnp.where(i == 0, j, n_tiles - 1)


def _weight_spec(w, layer, k, tn):
    if w.ndim == 3:
        return pl.BlockSpec((None, k, tn), lambda i, j: (layer, 0, j))
    return pl.BlockSpec((k, tn), lambda i, j: (0, j))


def _matmul(a, w, *, layer=0, emit=False, tm_pref=2048, tn_pref=512):
    m, k = a.shape
    n = w.shape[-1]
    tm = _tile(m, tm_pref, MATMUL_ROW_SUBTILE)
    tn = _tile(n, tn_pref, LANES)
    park = _parked(n // tn)
    out_specs = [pl.BlockSpec((tm, tn), lambda i, j: (i, j))]
    out_shape = [jax.ShapeDtypeStruct((m, n), F32)]
    if emit:
        out_specs.append(pl.BlockSpec((k, tn), lambda i, j: (0, park(i, j))))
        out_shape.append(jax.ShapeDtypeStruct((k, n), BF16))
    outs = pl.pallas_call(
        functools.partial(_mm_kernel, tm=tm, w_transposed=False, emit=emit),
        grid=(m // tm, n // tn),
        in_specs=[_resident((tm, k), lambda i, j: (i, 0)), _weight_spec(w, layer, k, tn)],
        out_specs=out_specs,
        out_shape=out_shape,
        scratch_shapes=[pltpu.VMEM((k, tn), BF16)],
        compiler_params=_params("arbitrary", "arbitrary"),
        name="matmul",
    )(a, w)
    return tuple(outs) if emit else outs[0]


def _matmul_wt(a, wt, *, layer, row0, n, emit=False, tm_pref=2048, tn_pref=512):
    m, k = a.shape
    tm = _tile(m, tm_pref, MATMUL_ROW_SUBTILE)
    tn = _tile(n, tn_pref, LANES)
    assert row0 % tn == 0 and n % tn == 0
    park = _parked(n // tn)
    out_specs = [pl.BlockSpec((tm, tn), lambda i, j: (i, j))]
    out_shape = [jax.ShapeDtypeStruct((m, n), F32)]
    if emit:
        out_specs.append(pl.BlockSpec((tn, k), lambda i, j: (park(i, j), 0)))
        out_shape.append(jax.ShapeDtypeStruct((n, k), BF16))
    outs = pl.pallas_call(
        functools.partial(_mm_kernel, tm=tm, w_transposed=True, emit=emit),
        grid=(m // tm, n // tn),
        in_specs=[_resident((tm, k), lambda i, j: (i, 0)),
                  pl.BlockSpec((None, tn, k), lambda i, j: (layer, row0 // tn + j, 0))],
        out_specs=out_specs,
        out_shape=out_shape,
        scratch_shapes=[pltpu.VMEM((tn, k), BF16)],
        compiler_params=_params("arbitrary", "arbitrary"),
        name="matmul_wt",
    )(a, wt)
    return tuple(outs) if emit else outs[0]


def _matmul_resid(a_parts, w, res, gate, *, layer, tm_pref=2048, tn_pref=512):
    parts = len(a_parts)
    m = a_parts[0].shape[0]
    ks = [a.shape[1] for a in a_parts]
    offs = [sum(ks[:p]) for p in range(parts)]
    n = w.shape[-1]
    assert w.shape[1] == sum(ks) and all(off % k == 0 for off, k in zip(offs, ks))
    tm = _tile(m, tm_pref, MATMUL_ROW_SUBTILE)
    tn = _tile(n, tn_pref, LANES)
    if gate.shape[0] == 1:
        gate_spec = pl.BlockSpec((1, tn), lambda i, j: (0, j))
    else:
        gate_spec = pl.BlockSpec((tm, tn), lambda i, j: (i, j))
    w_specs = [pl.BlockSpec((None, k, tn), functools.partial(lambda i, j, blk: (layer, blk, j), blk=off // k))
               for off, k in zip(offs, ks)]
    return pl.pallas_call(
        functools.partial(_mm_resid_kernel, tm=tm, parts=parts),
        grid=(m // tm, n // tn),
        in_specs=[_resident((tm, k), lambda i, j: (i, 0)) for k in ks] + w_specs
                 + [pl.BlockSpec((tm, tn), lambda i, j: (i, j)), gate_spec],
        out_specs=pl.BlockSpec((tm, tn), lambda i, j: (i, j)),
        out_shape=jax.ShapeDtypeStruct((m, n), F32),
        scratch_shapes=[pltpu.VMEM((k, tn), BF16) for k in ks],
        compiler_params=_params("arbitrary", "arbitrary"),
        name="matmul_resid",
    )(*a_parts, *([w] * parts), res, gate)


def _matmul_swiglu(a, w1, w3, *, layer=0, emit=False, tm_pref=2048, tn_pref=256):
    m, k = a.shape
    n = w1.shape[-1]
    tm = _tile(m, tm_pref, MATMUL_ROW_SUBTILE)
    tn = _tile(n, tn_pref, LANES)
    park = _parked(n // tn)
    out_specs = [pl.BlockSpec((tm, tn), lambda i, j: (i, j))]
    out_shape = [jax.ShapeDtypeStruct((m, n), BF16)]
    if emit:
        out_specs += [pl.BlockSpec((k, tn), lambda i, j: (0, park(i, j)))] * 2
        out_shape += [jax.ShapeDtypeStruct((k, n), BF16)] * 2
    outs = pl.pallas_call(
        functools.partial(_mm_swiglu_kernel, tm=tm, emit=emit),
        grid=(m // tm, n // tn),
        in_specs=[_resident((tm, k), lambda i, j: (i, 0)), _weight_spec(w1, layer, k, tn),
                  _weight_spec(w3, layer, k, tn)],
        out_specs=out_specs,
        out_shape=out_shape,
        scratch_shapes=[pltpu.VMEM((k, tn), BF16), pltpu.VMEM((k, tn), BF16)],
        compiler_params=_params("arbitrary", "arbitrary"),
        name="matmul_swiglu",
    )(a, w1, w3)
    return tuple(outs) if emit else outs[0]


def _lora_kernel(a_ref, w1_ref, w2_ref, *rest, mid, epilogue):
    o_ref = rest[-1]
    z = jnp.dot(a_ref[...], w1_ref[...], preferred_element_type=F32)
    if mid == "tanh":
        z = jnp.tanh(z)
    elif mid == "sigmoid":
        z = jax.nn.sigmoid(z)
    z = jnp.dot(z.astype(BF16), w2_ref[...], preferred_element_type=F32)
    if epilogue == "log_decay":
        o_ref[...] = (-float(np.exp(-0.5))) * jax.nn.sigmoid(rest[0][...] + z)
    elif epilogue == "sigmoid":
        o_ref[...] = jax.nn.sigmoid(rest[0][...] + z)
    elif epilogue == "vmix":
        v, vf = rest[1][...], rest[2][...]
        o_ref[...] = v + (vf - v) * jax.nn.sigmoid(rest[0][...] + z)
    else:
        o_ref[...] = z


def _lora(a, w1, w2, *, mid, epilogue, bias=None, extra=()):
    m, d = a.shape
    r = w1.shape[1]
    rp = -(-r // LANES) * LANES
    w1p = jnp.pad(w1, ((0, 0), (0, rp - r))).astype(BF16)
    w2p = jnp.pad(w2, ((0, rp - r), (0, 0))).astype(BF16)
    n = w2.shape[1]
    tm = _tile(m, 256, 16)
    row = pl.BlockSpec((tm, n), lambda i: (i, 0))
    in_specs = [pl.BlockSpec((tm, d), lambda i: (i, 0)), _resident((d, rp), lambda i: (0, 0)),
                _resident((rp, n), lambda i: (0, 0))]
    args = [a, w1p, w2p]
    if bias is not None:
        in_specs.append(pl.BlockSpec((1, n), lambda i: (0, 0)))
        args.append(bias.reshape(1, n))
    for e in extra:
        in_specs.append(row)
        args.append(e)
    return pl.pallas_call(
        functools.partial(_lora_kernel, mid=mid, epilogue=epilogue),
        grid=(m // tm,),
        in_specs=in_specs,
        out_specs=row,
        out_shape=jax.ShapeDtypeStruct((m, n), F32),
        compiler_params=_params("arbitrary"),
        name="lora_" + epilogue,
    )(*args)


def _dot_nt(a, b):
    return lax.dot_general(a, b, (((1,), (1,)), ((), ())), preferred_element_type=F32)


def _dot_tn(a, b):
    return lax.dot_general(a, b, (((0,), (0,)), ((), ())), preferred_element_type=F32)


MLSTM_CHUNK = 256


def _soft_cap(x):
    return GATE_CAP * jnp.tanh(x / GATE_CAP)


def _mlstm_kernel(q_ref, k_ref, v_ref, o_ref, gc_ref, gr_ref, bc_ref, br_ref, ng_ref, c0_ref, n0_ref, m0_ref,
                  h_ref, c_ref, n_ref, m_ref, *, heads, dk, dv, length):
    c_idx = pl.program_id(1)

    @pl.when(c_idx == 0)
    def _():
        c_ref[...] = c0_ref[...]
        n_ref[...] = n0_ref[...]
        m_ref[...] = m0_ref[...]

    gcol = gc_ref[...] + bc_ref[...]
    grow = gr_ref[...] + br_ref[...]
    li_col, lf_col = _soft_cap(gcol[:, :heads]), jax.nn.log_sigmoid(_soft_cap(gcol[:, heads:]))
    li_row, lf_row = _soft_cap(grow[:heads, :]), jax.nn.log_sigmoid(_soft_cap(grow[heads:, :]))
    r_id = lax.broadcasted_iota(jnp.int32, (length, length), 0)
    c_id = lax.broadcasted_iota(jnp.int32, (length, length), 1)
    tril = c_id <= r_id
    scale = dk ** -0.5

    hs = range(heads)
    gates = []
    for h in hs:
        lf_r, li_r = lf_row[h:h + 1, :], li_row[h:h + 1, :]
        lf_c, li_c = lf_col[:, h:h + 1], li_col[:, h:h + 1]
        b_col = jnp.sum(jnp.where(tril, lf_r, 0.0), axis=1, keepdims=True)
        b_row = jnp.sum(jnp.where(r_id <= c_id, lf_c, 0.0), axis=0, keepdims=True)
        b_last = b_col[length - 1:length, :]
        m_prev = m_ref[0, h:h + 1, :]
        log_d = jnp.where(tril, b_col - b_row + li_r, -jnp.inf)
        inter = b_col + m_prev
        m_t = jnp.maximum(inter, jnp.max(log_d, axis=1, keepdims=True))
        m_new = m_t[length - 1:length, :]
        gates.append(dict(p=jnp.exp(log_d - m_t), g=jnp.exp(inter - m_t), m_t=m_t, m_new=m_new,
                          wk_col=jnp.exp(b_last - b_col + li_c - m_new),
                          decay=jnp.exp(b_last + m_prev - m_new)))

    q = [q_ref[:, h * dk:(h + 1) * dk] for h in hs]
    k = [k_ref[:, h * dk:(h + 1) * dk] * scale for h in hs]
    kw = [k[h] * gates[h]["wk_col"] for h in hs]
    qb = [x.astype(BF16) for x in q]
    kb = [x.astype(BF16) for x in k]
    vb = [v_ref[:, h * dv:(h + 1) * dv].astype(BF16) for h in hs]
    c_state = [c_ref[0, h] for h in hs]
    n_state = [n_ref[0, h:h + 1, :] for h in hs]
    s_qk = [_dot_nt(qb[h], kb[h]) for h in hs]
    q_c = [jnp.dot(qb[h], c_state[h].astype(BF16), preferred_element_type=F32) for h in hs]
    kv = [_dot_tn(kw[h].astype(BF16), vb[h]) for h in hs]
    wqk = [gates[h]["p"] * s_qk[h] for h in hs]
    num = [jnp.dot(wqk[h].astype(BF16), vb[h], preferred_element_type=F32) + gates[h]["g"] * q_c[h] for h in hs]

    for h in hs:
        g = gates[h]
        den = jnp.sum(wqk[h], axis=1, keepdims=True) + g["g"] * jnp.sum(q[h] * n_state[h], axis=1, keepdims=True)
        hh = num[h] / jnp.maximum(jnp.abs(den), jnp.exp(-g["m_t"]))
        hh = hh * lax.rsqrt(jnp.mean(hh * hh, axis=1, keepdims=True) + EPS)
        hh = hh * ng_ref[:, h * dv:(h + 1) * dv]
        h_ref[:, h * dv:(h + 1) * dv] = (jax.nn.sigmoid(o_ref[:, h * dv:(h + 1) * dv]) * hh).astype(h_ref.dtype)
        c_ref[0, h] = g["decay"] * c_state[h] + kv[h]
        n_ref[0, h:h + 1, :] = g["decay"] * n_state[h] + jnp.sum(kw[h], axis=0, keepdims=True)
        m_ref[0, h:h + 1, :] = g["m_new"]


def _mlstm(proj, gif, b_if, a_norm_g, c0, n0, m0, *, batch, seq):
    _, heads, dk, dv = c0.shape
    length = _tile(seq, MLSTM_CHUNK, CHUNK)
    nc = seq // length
    wq, wv = heads * dk, heads * dv
    assert wq == wv
    gcol = gif
    grow = gif.reshape(batch, nc, length, 2 * heads).transpose(0, 1, 3, 2)
    row = lambda col: pl.BlockSpec((length, wq), lambda b, c: (b * nc + c, col))
    state4 = pl.BlockSpec((1, heads, dk, dv), lambda b, c: (b, 0, 0, 0))
    state3 = pl.BlockSpec((1, heads, dk), lambda b, c: (b, 0, 0))
    state_m = pl.BlockSpec((1, heads, 1), lambda b, c: (b, 0, 0))
    h, c_out, n_out, m_out = pl.pallas_call(
        functools.partial(_mlstm_kernel, heads=heads, dk=dk, dv=dv, length=length),
        grid=(batch, nc),
        in_specs=[row(0), row(1), row(2), row(3),
                  pl.BlockSpec((length, 2 * heads), lambda b, c: (b * nc + c, 0)),
                  pl.BlockSpec((None, None, 2 * heads, length), lambda b, c: (b, c, 0, 0)),
                  pl.BlockSpec((1, 2 * heads), lambda b, c: (0, 0)),
                  pl.BlockSpec((2 * heads, 1), lambda b, c: (0, 0)),
                  pl.BlockSpec((1, wv), lambda b, c: (0, 0)),
                  state4, state3, state_m],
        out_specs=[pl.BlockSpec((length, wv), lambda b, c: (b * nc + c, 0)), state4, state3, state_m],
        out_shape=[jax.ShapeDtypeStruct((batch * seq, wv), BF16),
                   jax.ShapeDtypeStruct((batch, heads, dk, dv), F32),
                   jax.ShapeDtypeStruct((batch, heads, dk), F32),
                   jax.ShapeDtypeStruct((batch, heads, 1), F32)],
        compiler_params=_params("arbitrary", "arbitrary"),
        name="mlstm",
    )(proj, proj, proj, proj, gcol, grow, b_if.reshape(1, 2 * heads), b_if.reshape(2 * heads, 1),
      a_norm_g.reshape(1, wv), c0, n0, m0.reshape(batch, heads, 1))
    return h, c_out, n_out, m_out.reshape(batch, heads)


def _rel_bias_table(rel_bias):
    heads, rel_size = rel_bias.shape
    assert rel_size == CHUNK + REL_MAX
    band = BAND_PAST + CHUNK
    n_dist = band + CHUNK - 1
    ext = jnp.concatenate([rel_bias, jnp.broadcast_to(rel_bias[:, -1:], (heads, n_dist - rel_size))], axis=1)
    rev = ext[:, ::-1]
    return jnp.stack([rev[:, CHUNK - 1 - i:CHUNK - 1 - i + band] for i in range(CHUNK)], axis=1)


def _band_chunks(qs, ks, vs, biases, scale):
    n = range(len(qs))
    s = [_dot_nt(qs[i], ks[i]) * scale + biases[i] for i in n]
    p = [jnp.exp(s[i] - jnp.max(s[i], axis=1, keepdims=True)) for i in n]
    o = [jnp.dot(p[i].astype(BF16), vs[i], preferred_element_type=F32) for i in n]
    return [o[i] / jnp.sum(p[i], axis=1, keepdims=True) for i in n]


ATTN_CHUNKS_PER_ITER = 8


def _attn_prompt_kernel(q_ref, k_ref, v_ref, bias_ref, o_ref, kb_ref, vb_ref, *, seq, scale):
    kb_ref[...] = k_ref[...].astype(BF16)
    vb_ref[...] = v_ref[...].astype(BF16)
    nc = seq // CHUNK
    band = BAND_PAST + CHUNK
    bias = bias_ref[0]

    lead = list(range(min(N_PREV_CHUNKS, nc)))
    for c0 in range(0, len(lead), ATTN_CHUNKS_PER_ITER):
        cs = lead[c0:c0 + ATTN_CHUNKS_PER_ITER]
        widths = [(c + 1) * CHUNK for c in cs]
        outs = _band_chunks([q_ref[c * CHUNK:(c + 1) * CHUNK, :].astype(BF16) for c in cs],
                            [kb_ref[0:w, :] for w in widths], [vb_ref[0:w, :] for w in widths],
                            [bias[:, band - w:] for w in widths], scale)
        for c, o in zip(cs, outs):
            o_ref[c * CHUNK:(c + 1) * CHUNK, :] = o.astype(o_ref.dtype)

    rest = nc - N_PREV_CHUNKS
    per_iter = ATTN_CHUNKS_PER_ITER if rest % ATTN_CHUNKS_PER_ITER == 0 else 1

    def body(it, carry):
        starts = [pl.multiple_of((N_PREV_CHUNKS + it * per_iter + u) * CHUNK, CHUNK) for u in range(per_iter)]
        k_rows = [pl.ds(pl.multiple_of(s - BAND_PAST, CHUNK), band) for s in starts]
        outs = _band_chunks([q_ref[pl.ds(s, CHUNK), :].astype(BF16) for s in starts],
                            [kb_ref[r, :] for r in k_rows], [vb_ref[r, :] for r in k_rows],
                            [bias] * per_iter, scale)
        for s, o in zip(starts, outs):
            o_ref[pl.ds(s, CHUNK), :] = o.astype(o_ref.dtype)
        return carry

    if rest > 0:
        lax.fori_loop(0, rest // per_iter, body, 0)


def _attn_prompt(proj_b, bias_table, *, batch, seq, heads, dh):
    col = lambda base: pl.BlockSpec((seq, dh), lambda b, h: (b, base * heads + h))
    return pl.pallas_call(
        functools.partial(_attn_prompt_kernel, seq=seq, scale=dh ** -0.5),
        grid=(batch, heads),
        in_specs=[col(0), col(1), col(2),
                  pl.BlockSpec((1, CHUNK, BAND_PAST + CHUNK), lambda b, h: (h, 0, 0))],
        out_specs=pl.BlockSpec((seq, dh), lambda b, h: (b, h)),
        out_shape=jax.ShapeDtypeStruct((batch * seq, heads * dh), BF16),
        scratch_shapes=[pltpu.VMEM((seq, dh), BF16), pltpu.VMEM((seq, dh), BF16)],
        compiler_params=_params("arbitrary", "arbitrary"),
        name="attn_prompt",
    )(proj_b, proj_b, proj_b, bias_table)


def _attn_sample_kernel(q_ref, k_ref, v_ref, pk_ref, pv_ref, bias_ref, o_ref, *, seq, width, scale):
    q = q_ref[...].astype(BF16)
    bias = bias_ref[0]
    off = BAND_PAST - width
    s_past = _dot_nt(q, pk_ref[...].astype(BF16)) * scale + bias[:seq, off:off + width]
    s_new = _dot_nt(q, k_ref[...].astype(BF16)) * scale + bias[:seq, BAND_PAST:BAND_PAST + seq]
    m = jnp.maximum(jnp.max(s_past, axis=1, keepdims=True), jnp.max(s_new, axis=1, keepdims=True))
    p_past, p_new = jnp.exp(s_past - m), jnp.exp(s_new - m)
    o = jnp.dot(p_past.astype(BF16), pv_ref[...].astype(BF16), preferred_element_type=F32) \
        + jnp.dot(p_new.astype(BF16), v_ref[...].astype(BF16), preferred_element_type=F32)
    denom = jnp.sum(p_past, axis=1, keepdims=True) + jnp.sum(p_new, axis=1, keepdims=True)
    o_ref[...] = (o / denom).astype(o_ref.dtype)


def _attn_sample(proj_b, past_k, past_v, layer, bias_table, *, batch, seq, heads, dh):
    width = past_k.shape[2]
    col = lambda base: pl.BlockSpec((seq, dh), lambda b, h: (b, base * heads + h))
    past = pl.BlockSpec((None, None, width, dh), lambda b, h: (layer, b, 0, h))
    return pl.pallas_call(
        functools.partial(_attn_sample_kernel, seq=seq, width=width, scale=dh ** -0.5),
        grid=(batch, heads),
        in_specs=[col(0), col(1), col(2), past, past,
                  pl.BlockSpec((1, CHUNK, BAND_PAST + CHUNK), lambda b, h: (h, 0, 0))],
        out_specs=pl.BlockSpec((seq, dh), lambda b, h: (b, h)),
        out_shape=jax.ShapeDtypeStruct((batch * seq, heads * dh), BF16),
        compiler_params=_params("arbitrary", "arbitrary"),
        name="attn_sample",
    )(proj_b, proj_b, proj_b, past_k, past_v, bias_table)


def _split_dot(a_exact, x):
    hi = x.astype(BF16)
    lo = (x - hi.astype(F32)).astype(BF16)
    return jnp.dot(a_exact, hi, preferred_element_type=F32) + jnp.dot(a_exact, lo, preferred_element_type=F32)


def _block_diag_rows(x, lo_mask):
    return jnp.concatenate([jnp.where(lo_mask, x, 0.0), jnp.where(lo_mask, 0.0, x)], axis=0).astype(BF16)


def _rwkv_kernel(r_ref, lw_ref, k_ref, v_ref, a_ref, g_ref, kk_ref, ka_ref, rk_ref, lg_ref, lb_ref, s0_ref,
                 o_ref, s_ref, s2_ref, *, pairs, dh, length, n_chunks):
    c_idx = pl.program_id(2)
    pw_ = 2 * dh
    ps = range(pairs)

    @pl.when(c_idx == 0)
    def _():
        for p in ps:
            s2_ref[p] = jnp.concatenate([s0_ref[0, 2 * p], s0_ref[0, 2 * p + 1]], axis=1)

    def lane_lo(shape, half):
        return lax.broadcasted_iota(jnp.int32, shape, 1) < half

    f_lo = lane_lo((length, pw_), dh)
    f_lo_s = lane_lo((dh, pw_), dh)
    t_lo = lane_lo((length, 2 * length), length)
    r_id = lax.broadcasted_iota(jnp.int32, (length, length), 0)
    c_id = lax.broadcasted_iota(jnp.int32, (length, length), 1)
    tril_bf = jnp.where(c_id <= r_id, 1.0, 0.0).astype(BF16)
    row2 = lax.broadcasted_iota(jnp.int32, (length, 2 * length), 0)
    col2 = lax.broadcasted_iota(jnp.int32, (length, 2 * length), 1) & (length - 1)
    strict2 = col2 < row2
    row4 = lax.broadcasted_iota(jnp.int32, (length, 4 * length), 0)
    col4 = lax.broadcasted_iota(jnp.int32, (length, 4 * length), 1) & (length - 1)
    incl4 = col4 <= row4
    n_double = max(int(np.ceil(np.log2(length))), 1)

    def head_sums(x, lo_mask):
        s_lo = jnp.sum(jnp.where(lo_mask, x, 0.0), axis=1, keepdims=True)
        s_hi = jnp.sum(jnp.where(lo_mask, 0.0, x), axis=1, keepdims=True)
        return jnp.where(lo_mask, s_lo, s_hi)

    r_all, lw_all, k_all, v_all, a_all = r_ref[...], lw_ref[...], k_ref[...], v_ref[...], a_ref[...]
    cum = _split_dot(tril_bf, lw_all)
    w_in = jnp.exp(cum)
    w_inv = jnp.exp(-cum)
    w_ex = jnp.exp(cum - lw_all)
    kk_all = k_all * kk_ref[...]
    k2_all = k_all * (1.0 + (a_all - 1.0) * ka_ref[...])
    rt_all = r_all * w_in
    kt_all = k2_all * w_inv
    ba_all = a_all * w_inv
    rk2_all = r_all * k2_all * rk_ref[...]
    sl = lambda x, p: x[:, p * pw_:(p + 1) * pw_]

    lhs, rhs, rhs_bd = [], [], []
    for p in ps:
        kk = sl(kk_all, p)
        kk = kk / jnp.maximum(jnp.sqrt(head_sums(kk * kk, f_lo)), 1e-12)
        at = (-kk) * sl(w_ex, p)
        bt = kk * sl(ba_all, p)
        kt = sl(kt_all, p)
        lhs.append(jnp.concatenate([at, sl(rt_all, p)], axis=0).astype(BF16))
        rhs.append(jnp.concatenate([bt, kt], axis=0).astype(BF16))
        rhs_bd.append(jnp.concatenate([_block_diag_rows(bt, f_lo), _block_diag_rows(kt, f_lo)], axis=0))
    s0 = [s2_ref[p] for p in ps]
    vf = [sl(v_all, p) for p in ps]
    v_bd = [_block_diag_rows(vf[p], f_lo) for p in ps]
    aals = [_dot_nt(lhs[p], jnp.concatenate([rhs_bd[p], _block_diag_rows(s0[p], f_lo_s)], axis=0)) for p in ps]
    aa = [z[:, :4 * length] for z in aals]
    ls = [z[:, 4 * length:] for z in aals]
    x = [ls[p][:length] + jnp.dot(jnp.where(strict2, aa[p][:length, 2 * length:], 0.0).astype(BF16), v_bd[p],
                                  preferred_element_type=F32) for p in ps]
    pw = [jnp.where(strict2, aa[p][:length, :2 * length], 0.0) for p in ps]
    for j in range(n_double):
        pwb = [q.astype(BF16) for q in pw]
        if j + 1 < n_double:
            both = [jnp.dot(pwb[p], jnp.concatenate([_block_diag_rows(x[p], f_lo), _block_diag_rows(pw[p], t_lo)],
                                                    axis=1), preferred_element_type=F32) for p in ps]
            x = [x[p] + both[p][:, :pw_] for p in ps]
            pw = [both[p][:, pw_:] for p in ps]
        else:
            x = [x[p] + jnp.dot(pwb[p], _block_diag_rows(x[p], f_lo), preferred_element_type=F32) for p in ps]
    uv_bd = [jnp.concatenate([_block_diag_rows(x[p], f_lo), v_bd[p]], axis=0) for p in ps]
    y = [ls[p][length:] + jnp.dot(jnp.where(incl4, aa[p][length:, :], 0.0).astype(BF16), uv_bd[p],
                                  preferred_element_type=F32) for p in ps]
    uv = [jnp.concatenate([x[p], vf[p]], axis=0).astype(BF16) for p in ps]
    ds = [_dot_tn(uv[p], rhs[p]) for p in ps]
    for p in ps:
        delta = jnp.where(f_lo_s, ds[p][:dh, :], ds[p][dh:, :])
        s2_ref[p] = (s0[p] + delta) * sl(w_in, p)[length - 1:length, :]

    inv_dh = 1.0 / dh
    for p in ps:
        cols = slice(p * pw_, (p + 1) * pw_)
        yc = y[p] - head_sums(y[p], f_lo) * inv_dh
        yn = yc * lax.rsqrt(head_sums(yc * yc, f_lo) * inv_dh + GN_EPS)
        yn = yn * lg_ref[:, cols] + lb_ref[:, cols]
        bonus = head_sums(sl(rk2_all, p), f_lo) * vf[p]
        o_ref[:, cols] = ((yn + bonus) * g_ref[:, cols]).astype(o_ref.dtype)

    @pl.when(c_idx == n_chunks - 1)
    def _():
        for p in ps:
            s_ref[0, 2 * p] = s2_ref[p][:, :dh]
            s_ref[0, 2 * p + 1] = s2_ref[p][:, dh:]


def _rwkv(r, lw, k, v, a, g, k_k, k_a, r_k, lnx_g, lnx_b, s0, *, batch, seq):
    _, heads, dh, _ = s0.shape
    d = heads * dh
    length = min(CHUNK, seq)
    nc = seq // length
    assert 2 * dh == LANES and length & (length - 1) == 0 and heads % 2 == 0
    hg = _tile(heads, 32 if length >= CHUNK else 16, 2)
    wg = hg * dh
    row = pl.BlockSpec((length, wg), lambda b, gi, c: (b * nc + c, gi))
    par = pl.BlockSpec((1, wg), lambda b, gi, c: (0, gi))
    state = pl.BlockSpec((1, hg, dh, dh), lambda b, gi, c: (b, gi, 0, 0))
    return pl.pallas_call(
        functools.partial(_rwkv_kernel, pairs=hg // 2, dh=dh, length=length, n_chunks=nc),
        grid=(batch, heads // hg, nc),
        in_specs=[row] * 6 + [par] * 5 + [state],
        out_specs=[row, state],
        out_shape=[jax.ShapeDtypeStruct((batch * seq, d), BF16), jax.ShapeDtypeStruct(s0.shape, F32)],
        scratch_shapes=[pltpu.VMEM((hg // 2, dh, 2 * dh), F32)],
        compiler_params=_params("arbitrary", "arbitrary", "arbitrary"),
        name="rwkv",
    )(r, lw, k, v, a, g, k_k.reshape(1, d), k_a.reshape(1, d), r_k.reshape(1, d), lnx_g.reshape(1, d),
      lnx_b.reshape(1, d), s0)


def _trunk(x, mod, P, st, wq):
    emit = st is None
    b, t, d = x.shape
    m = b * t
    depth = P["ada_w"].shape[0]
    _, heads_a, dk, dv = P["a_shape"]
    heads_b, dh_b = P["b_shape"]
    _, heads_c, dh_c, _ = P["c_shape"]
    w_a = heads_a * dk
    w_b = heads_b * dh_b
    a_C, a_n, a_m, b_k, b_v, c_S, c_sh = [], [], [], [], [], [], []
    v_first = None

    def gate_rows(gt):
        return gt if b == 1 else jnp.broadcast_to(gt[:, None, :], (b, t, d)).reshape(m, d)

    for l in range(depth):
        sh1, sc1, gt1, sh2, sc2, gt2 = (z.reshape(b, 1, d) for z in jnp.split(mod[l], 6, axis=-1))
        if l % 2 == 0:
            e = l // 2
            h = _norm_mod(x, P["norm1_g"][l], sc1, sh1).reshape(m, d)
            if st is None:
                c0 = jnp.zeros((b, heads_a, dk, dv), F32)
                n0 = jnp.zeros((b, heads_a, dk), F32)
                m0 = jnp.zeros((b, heads_a), F32)
            else:
                c0, n0, m0 = st["a_C"][e], st["a_n"][e], st["a_m"][e]
            gif = _matmul_wt(h, P["ab_wt"], layer=e, row0=4 * w_a, n=2 * heads_a)
            if emit:
                proj_a, wq["a", e] = _matmul_wt(h, P["ab_wt"], layer=e, row0=0, n=4 * w_a, emit=True)
                proj_b, wq["b", e] = _matmul_wt(h, P["ab_wt_b"], layer=e, row0=0, n=3 * w_b, emit=True)
            else:
                proj_a = _matmul_wt(h, wq["a", e][None], layer=0, row0=0, n=4 * w_a)
                proj_b = _matmul_wt(h, wq["b", e][None], layer=0, row0=0, n=3 * w_b)
            ha, C, n, mm = _mlstm(proj_a, gif, P["ab_b_if"][e], P["a_norm_g"][e], c0, n0, m0, batch=b, seq=t)
            if st is None:
                hb = _attn_prompt(proj_b, P["bias_table"][e], batch=b, seq=t, heads=heads_b, dh=dh_b)
                keep = min(BAND_PAST, t)
            else:
                hb = _attn_sample(proj_b, st["b_k"], st["b_v"], e, P["bias_table"][e],
                                  batch=b, seq=t, heads=heads_b, dh=dh_b)
                keep = t
            kept = proj_b.reshape(b, t, 3 * w_b)[:, t - keep:, :]
            a_C.append(C)
            a_n.append(n)
            a_m.append(mm)
            b_k.append(kept[:, :, w_b:2 * w_b].reshape(b, keep, heads_b, dh_b))
            b_v.append(kept[:, :, 2 * w_b:].reshape(b, keep, heads_b, dh_b))
            x = _matmul_resid([ha, hb], P["ab_w_out"], x.reshape(m, d), gate_rows(gt1.reshape(b, d)), layer=e)
        else:
            o = l // 2
            if st is None:
                s0 = jnp.zeros((b, heads_c, dh_c, dh_c), F32)
                shift0 = jnp.zeros((b, d), F32)
            else:
                s0, shift0 = st["c_S"][o], st["c_shift"][o]
            (xr, xw, xk, xv, xa, xg), shift = _norm_mix(x, shift0, P["norm1_g"][l], sc1, sh1, P["c_mu"][o])
            flat = lambda z: z.reshape(m, d)
            if emit:
                r, wq["r", o] = _matmul(flat(xr), P["c_wr"], layer=o, emit=True)
                k, wq["k", o] = _matmul(flat(xk), P["c_wk"], layer=o, emit=True)
                v, wq["v", o] = _matmul(flat(xv), P["c_wv"], layer=o, emit=True)
            else:
                r, k, v = (_matmul(flat(z), wq[name, o]) for z, name in ((xr, "r"), (xk, "k"), (xv, "v")))
            lw = _lora(flat(xw), P["c_w1"][o], P["c_w2"][o], mid="tanh", epilogue="log_decay", bias=P["c_w0"][o])
            a = _lora(flat(xa), P["c_a1"][o], P["c_a2"][o], mid="none", epilogue="sigmoid", bias=P["c_a0"][o])
            g = _lora(flat(xg), P["c_g1"][o], P["c_g2"][o], mid="sigmoid", epilogue="none")
            if v_first is None:
                v_first = v
            else:
                v = _lora(flat(xv), P["c_v1"][o - 1], P["c_v2"][o - 1], mid="none", epilogue="vmix",
                          bias=P["c_v0"][o - 1], extra=(v, v_first))
            y, S = _rwkv(r, lw, k, v, a, g, P["c_k_k"][o], P["c_k_a"][o], P["c_r_k"][o], P["c_lnx_g"][o],
                         P["c_lnx_b"][o], s0, batch=b, seq=t)
            c_S.append(S)
            c_sh.append(shift)
            x = _matmul_resid([y], P["c_wo"], x.reshape(m, d), gate_rows(gt1.reshape(b, d)), layer=o)
        x = x.reshape(b, t, d)
        h = _norm_mod(x, P["norm2_g"][l], sc2, sh2).reshape(m, d)
        if emit:
            hid, wq["w1", l], wq["w3", l] = _matmul_swiglu(h, P["ffn_w1"], P["ffn_w3"], layer=l, emit=True)
        else:
            hid = _matmul_swiglu(h, wq["w1", l], wq["w3", l])
        x = _matmul_resid([hid], P["ffn_w2"], x.reshape(m, d), gate_rows(gt2.reshape(b, d)), layer=l,
                          tm_pref=1024, tn_pref=256).reshape(b, t, d)
    y = _final_norm(x, P["final_g"])
    return (y, jnp.stack(a_C), jnp.stack(a_n), jnp.stack(a_m), jnp.stack(b_k), jnp.stack(b_v),
            jnp.stack(c_S), jnp.stack(c_sh))


def kernel(x_prompt, x_sample, c_prompt, c_sample, state_a_C, state_a_n, state_a_m, cache_b_k, cache_b_v, state_c_S, state_c_shift, ada_w, ada_b, norm1_g, norm2_g, final_g, ab_w_in, ab_b_if, a_norm_g, b_rel_bias, ab_w_out, c_mu, c_wr, c_wk, c_wv, c_wo, c_w0, c_w1, c_w2, c_a0, c_a1, c_a2, c_v0, c_v1, c_v2, c_g1, c_g2, c_k_k, c_k_a, c_r_k, c_lnx_g, c_lnx_b, ffn_w1, ffn_w3, ffn_w2):
    n_ab, dec_b, heads_a, dk, dv = state_a_C.shape
    _, _, width, heads_b, dh_b = cache_b_k.shape
    w_a = heads_a * dk
    gate_lo = 2 * w_a + 2 * heads_a * dv
    gate_hi = gate_lo + 2 * heads_a
    assert gate_lo == 4 * w_a
    ab_wt = jnp.swapaxes(ab_w_in, 1, 2)
    P = dict(ada_w=ada_w, norm1_g=norm1_g, norm2_g=norm2_g, final_g=final_g,
             ab_wt=ab_wt, ab_wt_b=ab_wt[:, gate_hi:, :],
             ab_b_if=ab_b_if, a_norm_g=a_norm_g,
             bias_table=jnp.stack([_rel_bias_table(b_rel_bias[e]) for e in range(n_ab)]),
             ab_w_out=ab_w_out, c_mu=c_mu, c_wr=c_wr, c_wk=c_wk, c_wv=c_wv, c_wo=c_wo,
             c_w0=c_w0, c_w1=c_w1, c_w2=c_w2, c_a0=c_a0, c_a1=c_a1, c_a2=c_a2, c_v0=c_v0, c_v1=c_v1, c_v2=c_v2,
             c_g1=c_g1, c_g2=c_g2, c_k_k=c_k_k, c_k_a=c_k_a, c_r_k=c_r_k, c_lnx_g=c_lnx_g, c_lnx_b=c_lnx_b,
             ffn_w1=ffn_w1, ffn_w3=ffn_w3, ffn_w2=ffn_w2,
             a_shape=state_a_C.shape[1:], b_shape=(heads_b, dh_b), c_shape=state_c_S.shape[1:])
    st = dict(a_C=state_a_C, a_n=state_a_n, a_m=state_a_m,
              b_k=cache_b_k.reshape(n_ab, dec_b, width, heads_b * dh_b),
              b_v=cache_b_v.reshape(n_ab, dec_b, width, heads_b * dh_b),
              c_S=state_c_S, c_shift=state_c_shift)

    n_p, n_s = c_prompt.shape[0], c_sample.shape[0]
    rows = -(-(n_p + n_s) // 16) * 16
    c_all = jnp.concatenate([c_prompt, c_sample, jnp.zeros((rows - n_p - n_s, c_prompt.shape[1]), F32)], axis=0)
    mod = _ada_mod(c_all, ada_w, ada_b)

    wq = {}
    outs_p = _trunk(x_prompt, mod[:, :n_p], P, None, wq)
    outs_s = _trunk(x_sample, mod[:, n_p:n_p + n_s], P, st, wq)
    return (outs_p[0], outs_s[0]) + tuple(outs_p[1:]) + tuple(outs_s[1:])
```

```python
import functools

import numpy as np
import jax
import jax.numpy as jnp
from jax import lax
from jax.experimental import pallas as pl
from jax.experimental.pallas import tpu as pltpu

F32 = jnp.float32
BF16 = jnp.bfloat16

CHUNK = 64
N_PREV_CHUNKS = 8
BAND_PAST = N_PREV_CHUNKS * CHUNK
PAST_LEN = 2048
REL_MAX = 2 * CHUNK
GATE_CAP = 15.0
EPS = 1e-6
GN_EPS = 64e-5

V7X_VMEM_BYTES = 64 * 1024 * 1024
VMEM_LIMIT = V7X_VMEM_BYTES - 6 * 1024 * 1024
LANES = 128
MATMUL_ROW_SUBTILE = 1024


def _params(*sem):
    return pltpu.CompilerParams(dimension_semantics=sem, vmem_limit_bytes=VMEM_LIMIT)


def _tile(n, pref, mult):
    if n <= pref:
        return n
    t = (pref // mult) * mult
    while t >= mult:
        if n % t == 0:
            return t
        t -= mult
    return n


def _silu(x):
    return x * jax.nn.sigmoid(x)


def _resident(block_shape, index_map):
    return pl.BlockSpec(block_shape, index_map, pipeline_mode=pl.Buffered(1))


ADA_TILE = 256


def _ada_tile(c_ref, w_ref, b_ref):
    cs = _silu(c_ref[...]).astype(BF16)
    return jnp.dot(cs, w_ref[...].astype(BF16), preferred_element_type=F32) + b_ref[...]


def _ada_kernel(c_ref, w_ref, b_ref, o_ref):
    o_ref[...] = _ada_tile(c_ref, w_ref, b_ref)


def _ada_mod(c_all, ada_w, ada_b, layer):
    _, d, n = ada_w.shape
    rows = c_all.shape[0]
    tn = _tile(n, 2 * ADA_TILE, LANES)
    return pl.pallas_call(
        _ada_kernel,
        grid=(n // tn,),
        in_specs=[
            pl.BlockSpec((rows, d), lambda j: (0, 0)),
            pl.BlockSpec((None, d, tn), lambda j: (layer, 0, j)),
            pl.BlockSpec((None, 1, tn), lambda j: (layer, 0, j)),
        ],
        out_specs=pl.BlockSpec((rows, tn), lambda j: (0, j)),
        out_shape=jax.ShapeDtypeStruct((rows, n), F32),
        compiler_params=_params("arbitrary"),
        name="ada_mod",
    )(c_all, ada_w, ada_b)


def _norm_mod_value(x, g, sc, sh):
    y = x * lax.rsqrt(jnp.mean(x * x, axis=-1, keepdims=True) + EPS)
    return (y * g) * (1.0 + sc) + sh


def _norm_mod_kernel(x_ref, g_ref, sc_ref, sh_ref, o_ref):
    o_ref[0] = _norm_mod_value(x_ref[0], g_ref[...], sc_ref[0], sh_ref[0]).astype(o_ref.dtype)


def _norm_mod(x, g, sc, sh, out_dtype=BF16):
    b, t, d = x.shape
    tt = _tile(t, 256, 16)
    row = pl.BlockSpec((1, tt, d), lambda bi, i: (bi, i, 0))
    per_batch = pl.BlockSpec((1, 1, d), lambda bi, i: (bi, 0, 0))
    return pl.pallas_call(
        _norm_mod_kernel,
        grid=(b, t // tt),
        in_specs=[row, pl.BlockSpec((1, d), lambda bi, i: (0, 0)), per_batch, per_batch],
        out_specs=row,
        out_shape=jax.ShapeDtypeStruct((b, t, d), out_dtype),
        compiler_params=_params("arbitrary", "arbitrary"),
        name="norm_mod",
    )(x, g.reshape(1, d), sc, sh)


def _final_norm_kernel(x_ref, g_ref, o_ref):
    x = x_ref[0]
    o_ref[0] = (x * lax.rsqrt(jnp.mean(x * x, axis=-1, keepdims=True) + EPS)) * g_ref[...]


def _final_norm(x, g):
    b, t, d = x.shape
    tt = _tile(t, 256, 8)
    row = pl.BlockSpec((1, tt, d), lambda bi, i: (bi, i, 0))
    return pl.pallas_call(
        _final_norm_kernel,
        grid=(b, t // tt),
        in_specs=[row, pl.BlockSpec((1, d), lambda bi, i: (0, 0))],
        out_specs=row,
        out_shape=jax.ShapeDtypeStruct((b, t, d), F32),
        compiler_params=_params("arbitrary", "arbitrary"),
        name="final_norm",
    )(x, g.reshape(1, d))


def _norm_mix_kernel(x_ref, xp_ref, s0_ref, g_ref, sc_ref, sh_ref, mu_ref, *out_refs):
    i = pl.program_id(1)
    mix_refs, last_ref = out_refs[:6], out_refs[6]
    tt, d = x_ref.shape[1], x_ref.shape[2]
    x, xp = x_ref[0], xp_ref[0]
    inv = lax.rsqrt(jnp.mean(x * x, axis=-1, keepdims=True) + EPS)
    inv_p = lax.rsqrt(jnp.mean(xp * xp, axis=-1, keepdims=True) + EPS)
    cw = LANES if d % LANES == 0 else d
    row_id = lax.broadcasted_iota(jnp.int32, (tt, cw), 0)
    for c in range(d // cw):
        cols = slice(c * cw, (c + 1) * cw)
        g, sc, sh = g_ref[:, cols], sc_ref[0, :, cols], sh_ref[0, :, cols]
        h = ((x_ref[0, :, cols] * inv) * g) * (1.0 + sc) + sh
        hp = ((xp_ref[0, :, cols] * inv_p) * g) * (1.0 + sc) + sh
        prev_row = jnp.where(i == 0, s0_ref[0, :, cols], hp[7:8, :])
        shifted = jnp.where(row_id == 0, prev_row, pltpu.roll(h, 1, axis=0))
        xx = shifted - h
        for j in range(6):
            mix_refs[j][0, :, cols] = (h + xx * mu_ref[j:j + 1, cols]).astype(BF16)
        last_ref[0, :, cols] = h[tt - 8:, :]


def _norm_mix(x, shift0, g, sc, sh, mu):
    b, t, d = x.shape
    tt = _tile(t, 256, 16)
    row = pl.BlockSpec((1, tt, d), lambda bi, i: (bi, i, 0))
    prev8 = pl.BlockSpec((1, 8, d), lambda bi, i: (bi, jnp.maximum(i * (tt // 8) - 1, 0), 0))
    per_batch = pl.BlockSpec((1, 1, d), lambda bi, i: (bi, 0, 0))
    outs = pl.pallas_call(
        _norm_mix_kernel,
        grid=(b, t // tt),
        in_specs=[row, prev8, per_batch, pl.BlockSpec((1, d), lambda bi, i: (0, 0)), per_batch, per_batch,
                  pl.BlockSpec((6, d), lambda bi, i: (0, 0))],
        out_specs=[row] * 6 + [pl.BlockSpec((1, 8, d), lambda bi, i: (bi, 0, 0))],
        out_shape=[jax.ShapeDtypeStruct((b, t, d), BF16)] * 6 + [jax.ShapeDtypeStruct((b, 8, d), F32)],
        compiler_params=_params("arbitrary", "arbitrary"),
        name="norm_mix",
    )(x, x, shift0.reshape(b, 1, d), g.reshape(1, d), sc, sh, mu)
    return outs[:6], outs[6][:, 7, :]


def _mm_rows(tm):
    sub = MATMUL_ROW_SUBTILE if tm % MATMUL_ROW_SUBTILE == 0 else tm
    return sub, tm // sub


def _row_slice(m, sub):
    return pl.ds(m * sub, sub) if isinstance(m, int) else pl.ds(pl.multiple_of(m * sub, sub), sub)


def _for_row_subtiles(n_sub, body):
    if n_sub <= 2:
        for m in range(n_sub):
            body(m, 0)
    else:
        lax.fori_loop(0, n_sub, body, 0)


def _emit_bf16_weights(wb_refs, wq_refs):
    @pl.when(pl.program_id(0) == 0)
    def _():
        for wb_ref, wq_ref in zip(wb_refs, wq_refs):
            wq_ref[...] = wb_ref[...]


def _mm_kernel(a_ref, w_ref, o_ref, *rest, tm, w_transposed, emit):
    wb_ref = rest[-1]
    wb_ref[...] = w_ref[...].astype(BF16)
    sub, n_sub = _mm_rows(tm)

    def body(m, carry):
        rows = _row_slice(m, sub)
        if w_transposed:
            acc = lax.dot_general(a_ref[rows, :], wb_ref[...], (((1,), (1,)), ((), ())),
                                  preferred_element_type=F32)
        else:
            acc = jnp.dot(a_ref[rows, :], wb_ref[...], preferred_element_type=F32)
        o_ref[rows, :] = acc.astype(o_ref.dtype)
        return carry

    _for_row_subtiles(n_sub, body)
    if emit:
        _emit_bf16_weights([wb_ref], [rest[0]])


def _mm_resid_kernel(*refs, tm, parts):
    a_refs, w_refs = refs[:parts], refs[parts:2 * parts]
    r_ref, g_ref, o_ref = refs[2 * parts:2 * parts + 3]
    wb_refs = refs[2 * parts + 3:]
    for w_ref, wb_ref in zip(w_refs, wb_refs):
        wb_ref[...] = w_ref[...].astype(BF16)
    sub, n_sub = _mm_rows(tm)
    per_row_gate = g_ref.shape[0] != 1

    def body(m, carry):
        rows = _row_slice(m, sub)
        acc = jnp.dot(a_refs[0][rows, :], wb_refs[0][...], preferred_element_type=F32)
        for a_ref, wb_ref in zip(a_refs[1:], wb_refs[1:]):
            acc = acc + jnp.dot(a_ref[rows, :], wb_ref[...], preferred_element_type=F32)
        gate = g_ref[rows, :] if per_row_gate else g_ref[...]
        o_ref[rows, :] = r_ref[rows, :] + gate * acc
        return carry

    _for_row_subtiles(n_sub, body)


def _mm_swiglu_kernel(a_ref, w1_ref, w3_ref, *rest, tm, emit, ada_tiles):
    if ada_tiles:
        c_ref, aw_ref, ab_ref = rest[:3]
        rest = rest[3:]
    o_ref, rest = rest[0], rest[1:]
    w1b_ref, w3b_ref = rest[-2:]
    w1b_ref[...] = w1_ref[...].astype(BF16)
    w3b_ref[...] = w3_ref[...].astype(BF16)
    sub, n_sub = _mm_rows(tm)

    def body(m, carry):
        rows = _row_slice(m, sub)
        a = a_ref[rows, :]
        u = jnp.dot(a, w1b_ref[...], preferred_element_type=F32)
        v = jnp.dot(a, w3b_ref[...], preferred_element_type=F32)
        o_ref[rows, :] = (_silu(u) * v).astype(o_ref.dtype)
        return carry

    _for_row_subtiles(n_sub, body)
    if emit:
        _emit_bf16_weights([w1b_ref, w3b_ref], rest[:2])
    if ada_tiles:
        mod_ref = rest[2] if emit else rest[0]
        step = pl.program_id(0) * pl.num_programs(1) + pl.program_id(1)

        @pl.when(step < ada_tiles)
        def _():
            mod_ref[...] = _ada_tile(c_ref, aw_ref, ab_ref)


def _parked(n_tiles):
    return lambda i, j: jnp.where(i == 0, j, n_tiles - 1)


def _weight_spec(w, layer, k, tn):
    if w.ndim == 3:
        return pl.BlockSpec((None, k, tn), lambda i, j: (layer, 0, j))
    return pl.BlockSpec((k, tn), lambda i, j: (0, j))


def _matmul(a, w, *, layer=0, emit=False, tm_pref=2048, tn_pref=512):
    m, k = a.shape
    n = w.shape[-1]
    tm = _tile(m, tm_pref, MATMUL_ROW_SUBTILE)
    tn = _tile(n, tn_pref, LANES)
    park = _parked(n // tn)
    out_specs = [pl.BlockSpec((tm, tn), lambda i, j: (i, j))]
    out_shape = [jax.ShapeDtypeStruct((m, n), F32)]
    if emit:
        out_specs.append(pl.BlockSpec((k, tn), lambda i, j: (0, park(i, j))))
        out_shape.append(jax.ShapeDtypeStruct((k, n), BF16))
    outs = pl.pallas_call(
        functools.partial(_mm_kernel, tm=tm, w_transposed=False, emit=emit),
        grid=(m // tm, n // tn),
        in_specs=[_resident((tm, k), lambda i, j: (i, 0)), _weight_spec(w, layer, k, tn)],
        out_specs=out_specs,
        out_shape=out_shape,
        scratch_shapes=[pltpu.VMEM((k, tn), BF16)],
        compiler_params=_params("arbitrary", "arbitrary"),
        name="matmul",
    )(a, w)
    return tuple(outs) if emit else outs[0]


def _matmul_wt(a, wt, *, layer, row0, n, emit=False, tm_pref=2048, tn_pref=512):
    m, k = a.shape
    tm = _tile(m, tm_pref, MATMUL_ROW_SUBTILE)
    tn = _tile(n, tn_pref, LANES)
    assert row0 % tn == 0 and n % tn == 0
    park = _parked(n // tn)
    out_specs = [pl.BlockSpec((tm, tn), lambda i, j: (i, j))]
    out_shape = [jax.ShapeDtypeStruct((m, n), F32)]
    if emit:
        out_specs.append(pl.BlockSpec((tn, k), lambda i, j: (park(i, j), 0)))
        out_shape.append(jax.ShapeDtypeStruct((n, k), BF16))
    outs = pl.pallas_call(
        functools.partial(_mm_kernel, tm=tm, w_transposed=True, emit=emit),
        grid=(m // tm, n // tn),
        in_specs=[_resident((tm, k), lambda i, j: (i, 0)),
                  pl.BlockSpec((None, tn, k), lambda i, j: (layer, row0 // tn + j, 0))],
        out_specs=out_specs,
        out_shape=out_shape,
        scratch_shapes=[pltpu.VMEM((tn, k), BF16)],
        compiler_params=_params("arbitrary", "arbitrary"),
        name="matmul_wt",
    )(a, wt)
    return tuple(outs) if emit else outs[0]


def _matmul_resid(a_parts, w, res, gate, *, layer, tm_pref=2048, tn_pref=512):
    parts = len(a_parts)
    m = a_parts[0].shape[0]
    ks = [a.shape[1] for a in a_parts]
    offs = [sum(ks[:p]) for p in range(parts)]
    n = w.shape[-1]
    assert w.shape[1] == sum(ks) and all(off % k == 0 for off, k in zip(offs, ks))
    tm = _tile(m, tm_pref, MATMUL_ROW_SUBTILE)
    tn = _tile(n, tn_pref, LANES)
    if gate.shape[0] == 1:
        gate_spec = pl.BlockSpec((1, tn), lambda i, j: (0, j))
    else:
        gate_spec = pl.BlockSpec((tm, tn), lambda i, j: (i, j))
    w_specs = [pl.BlockSpec((None, k, tn), functools.partial(lambda i, j, blk: (layer, blk, j), blk=off // k))
               for off, k in zip(offs, ks)]
    return pl.pallas_call(
        functools.partial(_mm_resid_kernel, tm=tm, parts=parts),
        grid=(m // tm, n // tn),
        in_specs=[_resident((tm, k), lambda i, j: (i, 0)) for k in ks] + w_specs
                 + [pl.BlockSpec((tm, tn), lambda i, j: (i, j)), gate_spec],
        out_specs=pl.BlockSpec((tm, tn), lambda i, j: (i, j)),
        out_shape=jax.ShapeDtypeStruct((m, n), F32),
        scratch_shapes=[pltpu.VMEM((k, tn), BF16) for k in ks],
        compiler_params=_params("arbitrary", "arbitrary"),
        name="matmul_resid",
    )(*a_parts, *([w] * parts), res, gate)


def _matmul_swiglu(a, w1, w3, *, layer=0, emit=False, ada=None, tm_pref=2048, tn_pref=256):
    m, k = a.shape
    n = w1.shape[-1]
    tm = _tile(m, tm_pref, MATMUL_ROW_SUBTILE)
    tn = _tile(n, tn_pref, LANES)
    n_i, n_j = m // tm, n // tn
    park = _parked(n_j)
    in_specs = [_resident((tm, k), lambda i, j: (i, 0)), _weight_spec(w1, layer, k, tn),
                _weight_spec(w3, layer, k, tn)]
    args = [a, w1, w3]
    out_specs = [pl.BlockSpec((tm, tn), lambda i, j: (i, j))]
    out_shape = [jax.ShapeDtypeStruct((m, n), BF16)]
    if emit:
        out_specs += [pl.BlockSpec((k, tn), lambda i, j: (0, park(i, j)))] * 2
        out_shape += [jax.ShapeDtypeStruct((k, n), BF16)] * 2
    ada_tiles = 0
    if ada is not None:
        c_all, ada_w, ada_b, next_layer = ada
        rows, d = c_all.shape
        n_mod = ada_w.shape[-1]
        ada_tiles = n_mod // ADA_TILE
    if ada is not None and (n_mod % ADA_TILE != 0 or ada_tiles > n_i * n_j):
        outs = _matmul_swiglu(a, w1, w3, layer=layer, emit=emit, tm_pref=tm_pref, tn_pref=tn_pref)
        outs = outs if isinstance(outs, tuple) else (outs,)
        return outs + (_ada_mod(c_all, ada_w, ada_b, next_layer),)
    if ada is not None:
        tile_of = lambda i, j: jnp.minimum(i * n_j + j, ada_tiles - 1)
        in_specs += [pl.BlockSpec((rows, d), lambda i, j: (0, 0)),
                     pl.BlockSpec((None, d, ADA_TILE), lambda i, j: (next_layer, 0, tile_of(i, j))),
                     pl.BlockSpec((None, 1, ADA_TILE), lambda i, j: (next_layer, 0, tile_of(i, j)))]
        args += [c_all, ada_w, ada_b]
        out_specs.append(pl.BlockSpec((rows, ADA_TILE), lambda i, j: (0, tile_of(i, j))))
        out_shape.append(jax.ShapeDtypeStruct((rows, n_mod), F32))
    outs = pl.pallas_call(
        functools.partial(_mm_swiglu_kernel, tm=tm, emit=emit, ada_tiles=ada_tiles),
        grid=(n_i, n_j),
        in_specs=in_specs,
        out_specs=out_specs,
        out_shape=out_shape,
        scratch_shapes=[pltpu.VMEM((k, tn), BF16), pltpu.VMEM((k, tn), BF16)],
        compiler_params=_params("arbitrary", "arbitrary"),
        name="matmul_swiglu",
    )(*args)
    return tuple(outs) if len(outs) > 1 else outs[0]


def _lora_kernel(a_ref, w1_ref, w2_ref, *rest, mid, epilogue):
    o_ref = rest[-1]
    z = jnp.dot(a_ref[...], w1_ref[...], preferred_element_type=F32)
    if mid == "tanh":
        z = jnp.tanh(z)
    elif mid == "sigmoid":
        z = jax.nn.sigmoid(z)
    z = jnp.dot(z.astype(BF16), w2_ref[...], preferred_element_type=F32)
    if epilogue == "log_decay":
        o_ref[...] = (-float(np.exp(-0.5))) * jax.nn.sigmoid(rest[0][...] + z)
    elif epilogue == "sigmoid":
        o_ref[...] = jax.nn.sigmoid(rest[0][...] + z)
    elif epilogue == "vmix":
        v, vf = rest[1][...], rest[2][...]
        o_ref[...] = v + (vf - v) * jax.nn.sigmoid(rest[0][...] + z)
    else:
        o_ref[...] = z


def _lora(a, w1, w2, *, mid, epilogue, bias=None, extra=()):
    m, d = a.shape
    r = w1.shape[1]
    rp = -(-r // LANES) * LANES
    w1p = jnp.pad(w1, ((0, 0), (0, rp - r))).astype(BF16)
    w2p = jnp.pad(w2, ((0, rp - r), (0, 0))).astype(BF16)
    n = w2.shape[1]
    tm = _tile(m, 256, 16)
    row = pl.BlockSpec((tm, n), lambda i: (i, 0))
    in_specs = [pl.BlockSpec((tm, d), lambda i: (i, 0)), _resident((d, rp), lambda i: (0, 0)),
                _resident((rp, n), lambda i: (0, 0))]
    args = [a, w1p, w2p]
    if bias is not None:
        in_specs.append(pl.BlockSpec((1, n), lambda i: (0, 0)))
        args.append(bias.reshape(1, n))
    for e in extra:
        in_specs.append(row)
        args.append(e)
    return pl.pallas_call(
        functools.partial(_lora_kernel, mid=mid, epilogue=epilogue),
        grid=(m // tm,),
        in_specs=in_specs,
        out_specs=row,
        out_shape=jax.ShapeDtypeStruct((m, n), F32),
        compiler_params=_params("arbitrary"),
        name="lora_" + epilogue,
    )(*args)


def _dot_nt(a, b):
    return lax.dot_general(a, b, (((1,), (1,)), ((), ())), preferred_element_type=F32)


def _dot_tn(a, b):
    return lax.dot_general(a, b, (((0,), (0,)), ((), ())), preferred_element_type=F32)


MLSTM_CHUNK = 256


def _soft_cap(x):
    return GATE_CAP * jnp.tanh(x / GATE_CAP)


def _mlstm_kernel(q_ref, k_ref, v_ref, o_ref, gc_ref, gr_ref, bc_ref, br_ref, ng_ref, c0_ref, n0_ref, m0_ref,
                  h_ref, c_ref, n_ref, m_ref, *, heads, dk, dv, length):
    c_idx = pl.program_id(1)

    @pl.when(c_idx == 0)
    def _():
        c_ref[...] = c0_ref[...]
        n_ref[...] = n0_ref[...]
        m_ref[...] = m0_ref[...]

    gcol = gc_ref[...] + bc_ref[...]
    grow = gr_ref[...] + br_ref[...]
    li_col, lf_col = _soft_cap(gcol[:, :heads]), jax.nn.log_sigmoid(_soft_cap(gcol[:, heads:]))
    li_row, lf_row = _soft_cap(grow[:heads, :]), jax.nn.log_sigmoid(_soft_cap(grow[heads:, :]))
    r_id = lax.broadcasted_iota(jnp.int32, (length, length), 0)
    c_id = lax.broadcasted_iota(jnp.int32, (length, length), 1)
    tril = c_id <= r_id
    scale = dk ** -0.5

    hs = range(heads)
    gates = []
    for h in hs:
        lf_r, li_r = lf_row[h:h + 1, :], li_row[h:h + 1, :]
        lf_c, li_c = lf_col[:, h:h + 1], li_col[:, h:h + 1]
        b_col = jnp.sum(jnp.where(tril, lf_r, 0.0), axis=1, keepdims=True)
        b_row = jnp.sum(jnp.where(r_id <= c_id, lf_c, 0.0), axis=0, keepdims=True)
        b_last = b_col[length - 1:length, :]
        m_prev = m_ref[0, h:h + 1, :]
        log_d = jnp.where(tril, b_col - b_row + li_r, -jnp.inf)
        inter = b_col + m_prev
        m_t = jnp.maximum(inter, jnp.max(log_d, axis=1, keepdims=True))
        m_new = m_t[length - 1:length, :]
        gates.append(dict(p=jnp.exp(log_d - m_t), g=jnp.exp(inter - m_t), m_t=m_t, m_new=m_new,
                          wk_col=jnp.exp(b_last - b_col + li_c - m_new),
                          decay=jnp.exp(b_last + m_prev - m_new)))

    q = [q_ref[:, h * dk:(h + 1) * dk] for h in hs]
    k = [k_ref[:, h * dk:(h + 1) * dk] * scale for h in hs]
    kw = [k[h] * gates[h]["wk_col"] for h in hs]
    qb = [x.astype(BF16) for x in q]
    kb = [x.astype(BF16) for x in k]
    vb = [v_ref[:, h * dv:(h + 1) * dv].astype(BF16) for h in hs]
    c_state = [c_ref[0, h] for h in hs]
    n_state = [n_ref[0, h:h + 1, :] for h in hs]
    s_qk = [_dot_nt(qb[h], kb[h]) for h in hs]
    q_c = [jnp.dot(qb[h], c_state[h].astype(BF16), preferred_element_type=F32) for h in hs]
    kv = [_dot_tn(kw[h].astype(BF16), vb[h]) for h in hs]
    wqk = [gates[h]["p"] * s_qk[h] for h in hs]
    num = [jnp.dot(wqk[h].astype(BF16), vb[h], preferred_element_type=F32) + gates[h]["g"] * q_c[h] for h in hs]

    for h in hs:
        g = gates[h]
        den = jnp.sum(wqk[h], axis=1, keepdims=True) + g["g"] * jnp.sum(q[h] * n_state[h], axis=1, keepdims=True)
        hh = num[h] / jnp.maximum(jnp.abs(den), jnp.exp(-g["m_t"]))
        hh = hh * lax.rsqrt(jnp.mean(hh * hh, axis=1, keepdims=True) + EPS)
        hh = hh * ng_ref[:, h * dv:(h + 1) * dv]
        h_ref[:, h * dv:(h + 1) * dv] = (jax.nn.sigmoid(o_ref[:, h * dv:(h + 1) * dv]) * hh).astype(h_ref.dtype)
        c_ref[0, h] = g["decay"] * c_state[h] + kv[h]
        n_ref[0, h:h + 1, :] = g["decay"] * n_state[h] + jnp.sum(kw[h], axis=0, keepdims=True)
        m_ref[0, h:h + 1, :] = g["m_new"]


def _mlstm(proj, gif, b_if, a_norm_g, c0, n0, m0, *, batch, seq):
    _, heads, dk, dv = c0.shape
    length = _tile(seq, MLSTM_CHUNK, CHUNK)
    nc = seq // length
    wq, wv = heads * dk, heads * dv
    assert wq == wv
    gcol = gif
    grow = gif.reshape(batch, nc, length, 2 * heads).transpose(0, 1, 3, 2)
    row = lambda col: pl.BlockSpec((length, wq), lambda b, c: (b * nc + c, col))
    state4 = pl.BlockSpec((1, heads, dk, dv), lambda b, c: (b, 0, 0, 0))
    state3 = pl.BlockSpec((1, heads, dk), lambda b, c: (b, 0, 0))
    state_m = pl.BlockSpec((1, heads, 1), lambda b, c: (b, 0, 0))
    h, c_out, n_out, m_out = pl.pallas_call(
        functools.partial(_mlstm_kernel, heads=heads, dk=dk, dv=dv, length=length),
        grid=(batch, nc),
        in_specs=[row(0), row(1), row(2), row(3),
                  pl.BlockSpec((length, 2 * heads), lambda b, c: (b * nc + c, 0)),
                  pl.BlockSpec((None, None, 2 * heads, length), lambda b, c: (b, c, 0, 0)),
                  pl.BlockSpec((1, 2 * heads), lambda b, c: (0, 0)),
                  pl.BlockSpec((2 * heads, 1), lambda b, c: (0, 0)),
                  pl.BlockSpec((1, wv), lambda b, c: (0, 0)),
                  state4, state3, state_m],
        out_specs=[pl.BlockSpec((length, wv), lambda b, c: (b * nc + c, 0)), state4, state3, state_m],
        out_shape=[jax.ShapeDtypeStruct((batch * seq, wv), BF16),
                   jax.ShapeDtypeStruct((batch, heads, dk, dv), F32),
                   jax.ShapeDtypeStruct((batch, heads, dk), F32),
                   jax.ShapeDtypeStruct((batch, heads, 1), F32)],
        compiler_params=_params("arbitrary", "arbitrary"),
        name="mlstm",
    )(proj, proj, proj, proj, gcol, grow, b_if.reshape(1, 2 * heads), b_if.reshape(2 * heads, 1),
      a_norm_g.reshape(1, wv), c0, n0, m0.reshape(batch, heads, 1))
    return h, c_out, n_out, m_out.reshape(batch, heads)


def _rel_bias_table(rel_bias):
    n_e, heads, rel_size = rel_bias.shape
    band = BAND_PAST + CHUNK
    i = np.arange(CHUNK)[:, None]
    j = np.arange(band)[None, :]
    rel = (np.clip(i - j + BAND_PAST, -(CHUNK - 1), REL_MAX) + (CHUNK - 1)).reshape(-1)
    onehot = (jnp.asarray(rel, jnp.int32)[None, :] == jnp.arange(rel_size, dtype=jnp.int32)[:, None]).astype(F32)
    table = jnp.dot(rel_bias.reshape(n_e * heads, rel_size), onehot, precision=lax.Precision.HIGHEST)
    return table.reshape(n_e, heads, CHUNK, band)


def _band_chunks(qs, ks, vs, biases, scale):
    n = range(len(qs))
    s = [_dot_nt(qs[i], ks[i]) * scale + biases[i] for i in n]
    p = [jnp.exp(s[i] - jnp.max(s[i], axis=1, keepdims=True)) for i in n]
    o = [jnp.dot(p[i].astype(BF16), vs[i], preferred_element_type=F32) for i in n]
    return [o[i] / jnp.sum(p[i], axis=1, keepdims=True) for i in n]


ATTN_CHUNKS_PER_ITER = 8


def _attn_prompt_kernel(q_ref, k_ref, v_ref, bias_ref, o_ref, kb_ref, vb_ref, *, seq, scale):
    kb_ref[...] = k_ref[...].astype(BF16)
    vb_ref[...] = v_ref[...].astype(BF16)
    nc = seq // CHUNK
    band = BAND_PAST + CHUNK
    bias = bias_ref[0]

    lead = list(range(min(N_PREV_CHUNKS, nc)))
    for c0 in range(0, len(lead), ATTN_CHUNKS_PER_ITER):
        cs = lead[c0:c0 + ATTN_CHUNKS_PER_ITER]
        widths = [(c + 1) * CHUNK for c in cs]
        outs = _band_chunks([q_ref[c * CHUNK:(c + 1) * CHUNK, :].astype(BF16) for c in cs],
                            [kb_ref[0:w, :] for w in widths], [vb_ref[0:w, :] for w in widths],
                            [bias[:, band - w:] for w in widths], scale)
        for c, o in zip(cs, outs):
            o_ref[c * CHUNK:(c + 1) * CHUNK, :] = o.astype(o_ref.dtype)

    rest = nc - N_PREV_CHUNKS
    per_iter = ATTN_CHUNKS_PER_ITER if rest % ATTN_CHUNKS_PER_ITER == 0 else 1

    def body(it, carry):
        starts = [pl.multiple_of((N_PREV_CHUNKS + it * per_iter + u) * CHUNK, CHUNK) for u in range(per_iter)]
        k_rows = [pl.ds(pl.multiple_of(s - BAND_PAST, CHUNK), band) for s in starts]
        outs = _band_chunks([q_ref[pl.ds(s, CHUNK), :].astype(BF16) for s in starts],
                            [kb_ref[r, :] for r in k_rows], [vb_ref[r, :] for r in k_rows],
                            [bias] * per_iter, scale)
        for s, o in zip(starts, outs):
            o_ref[pl.ds(s, CHUNK), :] = o.astype(o_ref.dtype)
        return carry

    if rest > 0:
        lax.fori_loop(0, rest // per_iter, body, 0)


def _attn_prompt(proj_b, bias_table, *, batch, seq, heads, dh):
    col = lambda base: pl.BlockSpec((seq, dh), lambda b, h: (b, base * heads + h))
    return pl.pallas_call(
        functools.partial(_attn_prompt_kernel, seq=seq, scale=dh ** -0.5),
        grid=(batch, heads),
        in_specs=[col(0), col(1), col(2),
                  pl.BlockSpec((1, CHUNK, BAND_PAST + CHUNK), lambda b, h: (h, 0, 0))],
        out_specs=pl.BlockSpec((seq, dh), lambda b, h: (b, h)),
        out_shape=jax.ShapeDtypeStruct((batch * seq, heads * dh), BF16),
        scratch_shapes=[pltpu.VMEM((seq, dh), BF16), pltpu.VMEM((seq, dh), BF16)],
        compiler_params=_params("arbitrary", "arbitrary"),
        name="attn_prompt",
    )(proj_b, proj_b, proj_b, bias_table)


def _attn_sample_kernel(q_ref, k_ref, v_ref, pk_ref, pv_ref, bias_ref, o_ref, *, seq, width, heads, dh, scale):
    off = BAND_PAST - width
    hs = range(heads)
    cols = [slice(h * dh, (h + 1) * dh) for h in hs]
    rows = [pl.ds(h, width, stride=heads) for h in hs]
    q = [q_ref[:, c].astype(BF16) for c in cols]
    s_past = [_dot_nt(q[h], pk_ref[rows[h], :].astype(BF16)) * scale + bias_ref[h, :seq, off:off + width] for h in hs]
    s_new = [_dot_nt(q[h], k_ref[:, cols[h]].astype(BF16)) * scale + bias_ref[h, :seq, BAND_PAST:BAND_PAST + seq]
             for h in hs]
    m = [jnp.maximum(jnp.max(s_past[h], axis=1, keepdims=True), jnp.max(s_new[h], axis=1, keepdims=True)) for h in hs]
    p_past = [jnp.exp(s_past[h] - m[h]) for h in hs]
    p_new = [jnp.exp(s_new[h] - m[h]) for h in hs]
    o = [jnp.dot(p_past[h].astype(BF16), pv_ref[rows[h], :].astype(BF16), preferred_element_type=F32)
         + jnp.dot(p_new[h].astype(BF16), v_ref[:, cols[h]].astype(BF16), preferred_element_type=F32) for h in hs]
    for h in hs:
        denom = jnp.sum(p_past[h], axis=1, keepdims=True) + jnp.sum(p_new[h], axis=1, keepdims=True)
        o_ref[:, cols[h]] = (o[h] / denom).astype(o_ref.dtype)


def _attn_sample(proj_b, past_k, past_v, layer, bias_table, *, batch, seq, heads, dh):
    width = past_k.shape[2] // heads
    wb = heads * dh
    col = lambda base: pl.BlockSpec((seq, wb), lambda b: (b, base))
    past = pl.BlockSpec((None, None, width * heads, dh), lambda b: (layer, b, 0, 0))
    return pl.pallas_call(
        functools.partial(_attn_sample_kernel, seq=seq, width=width, heads=heads, dh=dh, scale=dh ** -0.5),
        grid=(batch,),
        in_specs=[col(0), col(1), col(2), past, past,
                  pl.BlockSpec((heads, CHUNK, BAND_PAST + CHUNK), lambda b: (0, 0, 0))],
        out_specs=pl.BlockSpec((seq, wb), lambda b: (b, 0)),
        out_shape=jax.ShapeDtypeStruct((batch * seq, wb), BF16),
        compiler_params=_params("arbitrary"),
        name="attn_sample",
    )(proj_b, proj_b, proj_b, past_k, past_v, bias_table)


def _split_dot(a_exact, x):
    hi = x.astype(BF16)
    lo = (x - hi.astype(F32)).astype(BF16)
    return jnp.dot(a_exact, hi, preferred_element_type=F32) + jnp.dot(a_exact, lo, preferred_element_type=F32)


def _block_diag_rows(x, lo_mask):
    return jnp.concatenate([jnp.where(lo_mask, x, 0.0), jnp.where(lo_mask, 0.0, x)], axis=0).astype(BF16)


def _rwkv_kernel(r_ref, lw_ref, k_ref, v_ref, a_ref, g_ref, kk_ref, ka_ref, rk_ref, lg_ref, lb_ref, s0_ref,
                 o_ref, s_ref, s2_ref, *, pairs, dh, length, n_chunks):
    c_idx = pl.program_id(2)
    pw_ = 2 * dh
    ps = range(pairs)

    @pl.when(c_idx == 0)
    def _():
        for p in ps:
            s2_ref[p] = jnp.concatenate([s0_ref[0, 2 * p], s0_ref[0, 2 * p + 1]], axis=1)

    def lane_lo(shape, half):
        return lax.broadcasted_iota(jnp.int32, shape, 1) < half

    f_lo = lane_lo((length, pw_), dh)
    f_lo_s = lane_lo((dh, pw_), dh)
    t_lo = lane_lo((length, 2 * length), length)
    r_id = lax.broadcasted_iota(jnp.int32, (length, length), 0)
    c_id = lax.broadcasted_iota(jnp.int32, (length, length), 1)
    tril_bf = jnp.where(c_id <= r_id, 1.0, 0.0).astype(BF16)
    row2 = lax.broadcasted_iota(jnp.int32, (length, 2 * length), 0)
    col2 = lax.broadcasted_iota(jnp.int32, (length, 2 * length), 1) & (length - 1)
    strict2 = col2 < row2
    row4 = lax.broadcasted_iota(jnp.int32, (length, 4 * length), 0)
    col4 = lax.broadcasted_iota(jnp.int32, (length, 4 * length), 1) & (length - 1)
    incl4 = col4 <= row4
    n_double = max(int(np.ceil(np.log2(length))), 1)

    def head_sums(x, lo_mask):
        s_lo = jnp.sum(jnp.where(lo_mask, x, 0.0), axis=1, keepdims=True)
        s_hi = jnp.sum(jnp.where(lo_mask, 0.0, x), axis=1, keepdims=True)
        return jnp.where(lo_mask, s_lo, s_hi)

    r_all, lw_all, k_all, v_all, a_all = r_ref[...], lw_ref[...], k_ref[...], v_ref[...], a_ref[...]
    cum = _split_dot(tril_bf, lw_all)
    w_in = jnp.exp(cum)
    w_inv = jnp.exp(-cum)
    w_ex = jnp.exp(cum - lw_all)
    kk_all = k_all * kk_ref[...]
    k2_all = k_all * (1.0 + (a_all - 1.0) * ka_ref[...])
    rt_all = r_all * w_in
    kt_all = k2_all * w_inv
    ba_all = a_all * w_inv
    rk2_all = r_all * k2_all * rk_ref[...]
    sl = lambda x, p: x[:, p * pw_:(p + 1) * pw_]

    lhs, rhs, rhs_bd = [], [], []
    for p in ps:
        kk = sl(kk_all, p)
        kk = kk / jnp.maximum(jnp.sqrt(head_sums(kk * kk, f_lo)), 1e-12)
        at = (-kk) * sl(w_ex, p)
        bt = kk * sl(ba_all, p)
        kt = sl(kt_all, p)
        lhs.append(jnp.concatenate([at, sl(rt_all, p)], axis=0).astype(BF16))
        rhs.append(jnp.concatenate([bt, kt], axis=0).astype(BF16))
        rhs_bd.append(jnp.concatenate([_block_diag_rows(bt, f_lo), _block_diag_rows(kt, f_lo)], axis=0))
    s0 = [s2_ref[p] for p in ps]
    vf = [sl(v_all, p) for p in ps]
    v_bd = [_block_diag_rows(vf[p], f_lo) for p in ps]
    aals = [_dot_nt(lhs[p], jnp.concatenate([rhs_bd[p], _block_diag_rows(s0[p], f_lo_s)], axis=0)) for p in ps]
    aa = [z[:, :4 * length] for z in aals]
    ls = [z[:, 4 * length:] for z in aals]
    x = [ls[p][:length] + jnp.dot(jnp.where(strict2, aa[p][:length, 2 * length:], 0.0).astype(BF16), v_bd[p],
                                  preferred_element_type=F32) for p in ps]
    pw = [jnp.where(strict2, aa[p][:length, :2 * length], 0.0) for p in ps]
    for j in range(n_double):
        pwb = [q.astype(BF16) for q in pw]
        if j + 1 < n_double:
            both = [jnp.dot(pwb[p], jnp.concatenate([_block_diag_rows(x[p], f_lo), _block_diag_rows(pw[p], t_lo)],
                                                    axis=1), preferred_element_type=F32) for p in ps]
            x = [x[p] + both[p][:, :pw_] for p in ps]
            pw = [both[p][:, pw_:] for p in ps]
        else:
            x = [x[p] + jnp.dot(pwb[p], _block_diag_rows(x[p], f_lo), preferred_element_type=F32) for p in ps]
    uv_bd = [jnp.concatenate([_block_diag_rows(x[p], f_lo), v_bd[p]], axis=0) for p in ps]
    y = [ls[p][length:] + jnp.dot(jnp.where(incl4, aa[p][length:, :], 0.0).astype(BF16), uv_bd[p],
                                  preferred_element_type=F32) for p in ps]
    uv = [jnp.concatenate([x[p], vf[p]], axis=0).astype(BF16) for p in ps]
    ds = [_dot_tn(uv[p], rhs[p]) for p in ps]
    for p in ps:
        delta = jnp.where(f_lo_s, ds[p][:dh, :], ds[p][dh:, :])
        s2_ref[p] = (s0[p] + delta) * sl(w_in, p)[length - 1:length, :]

    inv_dh = 1.0 / dh
    for p in ps:
        cols = slice(p * pw_, (p + 1) * pw_)
        yc = y[p] - head_sums(y[p], f_lo) * inv_dh
        yn = yc * lax.rsqrt(head_sums(yc * yc, f_lo) * inv_dh + GN_EPS)
        yn = yn * lg_ref[:, cols] + lb_ref[:, cols]
        bonus = head_sums(sl(rk2_all, p), f_lo) * vf[p]
        o_ref[:, cols] = ((yn + bonus) * g_ref[:, cols]).astype(o_ref.dtype)

    @pl.when(c_idx == n_chunks - 1)
    def _():
        for p in ps:
            s_ref[0, 2 * p] = s2_ref[p][:, :dh]
            s_ref[0, 2 * p + 1] = s2_ref[p][:, dh:]


def _rwkv(r, lw, k, v, a, g, k_k, k_a, r_k, lnx_g, lnx_b, s0, *, batch, seq):
    _, heads, dh, _ = s0.shape
    d = heads * dh
    length = min(CHUNK, seq)
    nc = seq // length
    assert 2 * dh == LANES and length & (length - 1) == 0 and heads % 2 == 0
    hg = _tile(heads, 32 if length >= CHUNK else 16, 2)
    wg = hg * dh
    row = pl.BlockSpec((length, wg), lambda b, gi, c: (b * nc + c, gi))
    par = pl.BlockSpec((1, wg), lambda b, gi, c: (0, gi))
    state = pl.BlockSpec((1, hg, dh, dh), lambda b, gi, c: (b, gi, 0, 0))
    return pl.pallas_call(
        functools.partial(_rwkv_kernel, pairs=hg // 2, dh=dh, length=length, n_chunks=nc),
        grid=(batch, heads // hg, nc),
        in_specs=[row] * 6 + [par] * 5 + [state],
        out_specs=[row, state],
        out_shape=[jax.ShapeDtypeStruct((batch * seq, d), BF16), jax.ShapeDtypeStruct(s0.shape, F32)],
        scratch_shapes=[pltpu.VMEM((hg // 2, dh, 2 * dh), F32)],
        compiler_params=_params("arbitrary", "arbitrary", "arbitrary"),
        name="rwkv",
    )(r, lw, k, v, a, g, k_k.reshape(1, d), k_a.reshape(1, d), r_k.reshape(1, d), lnx_g.reshape(1, d),
      lnx_b.reshape(1, d), s0)


def _trunk(x, rows, ada, P, st, shared):
    emit = st is None
    c_all, ada_w, ada_b = ada
    wq = shared
    b, t, d = x.shape
    m = b * t
    depth = P["ada_w"].shape[0]
    _, heads_a, dk, dv = P["a_shape"]
    heads_b, dh_b = P["b_shape"]
    _, heads_c, dh_c, _ = P["c_shape"]
    w_a = heads_a * dk
    w_b = heads_b * dh_b
    a_C, a_n, a_m, b_k, b_v, c_S, c_sh = [], [], [], [], [], [], []
    v_first = None

    def gate_rows(gt):
        return gt if b == 1 else jnp.broadcast_to(gt[:, None, :], (b, t, d)).reshape(m, d)

    for l in range(depth):
        if emit and l == 0:
            shared["mod", 0] = _ada_mod(c_all, ada_w, ada_b, 0)
        sh1, sc1, gt1, sh2, sc2, gt2 = (z.reshape(b, 1, d) for z in jnp.split(shared["mod", l][rows], 6, axis=-1))
        if l % 2 == 0:
            e = l // 2
            h = _norm_mod(x, P["norm1_g"][l], sc1, sh1).reshape(m, d)
            if st is None:
                c0 = jnp.zeros((b, heads_a, dk, dv), F32)
                n0 = jnp.zeros((b, heads_a, dk), F32)
                m0 = jnp.zeros((b, heads_a), F32)
            else:
                c0, n0, m0 = st["a_C"][e], st["a_n"][e], st["a_m"][e]
            gif = _matmul_wt(h, P["ab_wt"], layer=e, row0=4 * w_a, n=2 * heads_a)
            if emit:
                proj_a, wq["a", e] = _matmul_wt(h, P["ab_wt"], layer=e, row0=0, n=4 * w_a, emit=True)
                proj_b, wq["b", e] = _matmul_wt(h, P["ab_wt_b"], layer=e, row0=0, n=3 * w_b, emit=True)
            else:
                proj_a = _matmul_wt(h, wq["a", e][None], layer=0, row0=0, n=4 * w_a)
                proj_b = _matmul_wt(h, wq["b", e][None], layer=0, row0=0, n=3 * w_b)
            ha, C, n, mm = _mlstm(proj_a, gif, P["ab_b_if"][e], P["a_norm_g"][e], c0, n0, m0, batch=b, seq=t)
            if st is None:
                hb = _attn_prompt(proj_b, P["bias_table"][e], batch=b, seq=t, heads=heads_b, dh=dh_b)
                keep = min(BAND_PAST, t)
            else:
                hb = _attn_sample(proj_b, st["b_k"], st["b_v"], e, P["bias_table"][e],
                                  batch=b, seq=t, heads=heads_b, dh=dh_b)
                keep = t
            kept = proj_b.reshape(b, t, 3 * w_b)[:, t - keep:, :]
            a_C.append(C)
            a_n.append(n)
            a_m.append(mm)
            b_k.append(kept[:, :, w_b:2 * w_b].reshape(b, keep, heads_b, dh_b))
            b_v.append(kept[:, :, 2 * w_b:].reshape(b, keep, heads_b, dh_b))
            x = _matmul_resid([ha, hb], P["ab_w_out"], x.reshape(m, d), gate_rows(gt1.reshape(b, d)), layer=e)
        else:
            o = l // 2
            if st is None:
                s0 = jnp.zeros((b, heads_c, dh_c, dh_c), F32)
                shift0 = jnp.zeros((b, d), F32)
            else:
                s0, shift0 = st["c_S"][o], st["c_shift"][o]
            (xr, xw, xk, xv, xa, xg), shift = _norm_mix(x, shift0, P["norm1_g"][l], sc1, sh1, P["c_mu"][o])
            flat = lambda z: z.reshape(m, d)
            if emit:
                r, wq["r", o] = _matmul(flat(xr), P["c_wr"], layer=o, emit=True)
                k, wq["k", o] = _matmul(flat(xk), P["c_wk"], layer=o, emit=True)
                v, wq["v", o] = _matmul(flat(xv), P["c_wv"], layer=o, emit=True)
            else:
                r, k, v = (_matmul(flat(z), wq[name, o]) for z, name in ((xr, "r"), (xk, "k"), (xv, "v")))
            lw = _lora(flat(xw), P["c_w1"][o], P["c_w2"][o], mid="tanh", epilogue="log_decay", bias=P["c_w0"][o])
            a = _lora(flat(xa), P["c_a1"][o], P["c_a2"][o], mid="none", epilogue="sigmoid", bias=P["c_a0"][o])
            g = _lora(flat(xg), P["c_g1"][o], P["c_g2"][o], mid="sigmoid", epilogue="none")
            if v_first is None:
                v_first = v
            else:
                v = _lora(flat(xv), P["c_v1"][o - 1], P["c_v2"][o - 1], mid="none", epilogue="vmix",
                          bias=P["c_v0"][o - 1], extra=(v, v_first))
            y, S = _rwkv(r, lw, k, v, a, g, P["c_k_k"][o], P["c_k_a"][o], P["c_r_k"][o], P["c_lnx_g"][o],
                         P["c_lnx_b"][o], s0, batch=b, seq=t)
            c_S.append(S)
            c_sh.append(shift)
            x = _matmul_resid([y], P["c_wo"], x.reshape(m, d), gate_rows(gt1.reshape(b, d)), layer=o)
        x = x.reshape(b, t, d)
        h = _norm_mod(x, P["norm2_g"][l], sc2, sh2).reshape(m, d)
        if emit and l + 1 < depth:
            hid, wq["w1", l], wq["w3", l], shared["mod", l + 1] = _matmul_swiglu(
                h, P["ffn_w1"], P["ffn_w3"], layer=l, emit=True, ada=(c_all, ada_w, ada_b, l + 1))
        elif emit:
            hid, wq["w1", l], wq["w3", l] = _matmul_swiglu(h, P["ffn_w1"], P["ffn_w3"], layer=l, emit=True)
        else:
            hid = _matmul_swiglu(h, wq["w1", l], wq["w3", l])
        x = _matmul_resid([hid], P["ffn_w2"], x.reshape(m, d), gate_rows(gt2.reshape(b, d)), layer=l,
                          tm_pref=1024, tn_pref=256).reshape(b, t, d)
    y = _final_norm(x, P["final_g"])
    return (y, jnp.stack(a_C), jnp.stack(a_n), jnp.stack(a_m), jnp.stack(b_k), jnp.stack(b_v),
            jnp.stack(c_S), jnp.stack(c_sh))


def kernel(x_prompt, x_sample, c_prompt, c_sample, state_a_C, state_a_n, state_a_m, cache_b_k, cache_b_v, state_c_S, state_c_shift, ada_w, ada_b, norm1_g, norm2_g, final_g, ab_w_in, ab_b_if, a_norm_g, b_rel_bias, ab_w_out, c_mu, c_wr, c_wk, c_wv, c_wo, c_w0, c_w1, c_w2, c_a0, c_a1, c_a2, c_v0, c_v1, c_v2, c_g1, c_g2, c_k_k, c_k_a, c_r_k, c_lnx_g, c_lnx_b, ffn_w1, ffn_w3, ffn_w2):
    n_ab, dec_b, heads_a, dk, dv = state_a_C.shape
    _, _, width, heads_b, dh_b = cache_b_k.shape
    w_a = heads_a * dk
    gate_lo = 2 * w_a + 2 * heads_a * dv
    gate_hi = gate_lo + 2 * heads_a
    assert gate_lo == 4 * w_a
    ab_wt = jnp.swapaxes(ab_w_in, 1, 2)
    P = dict(ada_w=ada_w, norm1_g=norm1_g, norm2_g=norm2_g, final_g=final_g,
             ab_wt=ab_wt, ab_wt_b=ab_wt[:, gate_hi:, :],
             ab_b_if=ab_b_if, a_norm_g=a_norm_g,
             bias_table=_rel_bias_table(b_rel_bias),
             ab_w_out=ab_w_out, c_mu=c_mu, c_wr=c_wr, c_wk=c_wk, c_wv=c_wv, c_wo=c_wo,
             c_w0=c_w0, c_w1=c_w1, c_w2=c_w2, c_a0=c_a0, c_a1=c_a1, c_a2=c_a2, c_v0=c_v0, c_v1=c_v1, c_v2=c_v2,
             c_g1=c_g1, c_g2=c_g2, c_k_k=c_k_k, c_k_a=c_k_a, c_r_k=c_r_k, c_lnx_g=c_lnx_g, c_lnx_b=c_lnx_b,
             ffn_w1=ffn_w1, ffn_w3=ffn_w3, ffn_w2=ffn_w2,
             a_shape=state_a_C.shape[1:], b_shape=(heads_b, dh_b), c_shape=state_c_S.shape[1:])
    st = dict(a_C=state_a_C, a_n=state_a_n, a_m=state_a_m,
              b_k=cache_b_k.reshape(n_ab, dec_b, width * heads_b, dh_b),
              b_v=cache_b_v.reshape(n_ab, dec_b, width * heads_b, dh_b),
              c_S=state_c_S, c_shift=state_c_shift)

    n_p, n_s = c_prompt.shape[0], c_sample.shape[0]
    rows = -(-(n_p + n_s) // 16) * 16
    c_all = jnp.concatenate([c_prompt, c_sample, jnp.zeros((rows - n_p - n_s, c_prompt.shape[1]), F32)], axis=0)
    ada = (c_all, ada_w, ada_b.reshape(ada_b.shape[0], 1, ada_b.shape[1]))

    shared = {}
    outs_p = _trunk(x_prompt, slice(0, n_p), ada, P, None, shared)
    outs_s = _trunk(x_sample, slice(n_p, n_p + n_s), ada, P, st, shared)
    return (outs_p[0], outs_s[0]) + tuple(outs_p[1:]) + tuple(outs_s[1:])
```

```python
import functools

import numpy as np
import jax
import jax.numpy as jnp
from jax import lax
from jax.experimental import pallas as pl
from jax.experimental.pallas import tpu as pltpu

F32 = jnp.float32
BF16 = jnp.bfloat16

CHUNK = 64
N_PREV_CHUNKS = 8
BAND_PAST = N_PREV_CHUNKS * CHUNK
PAST_LEN = 2048
REL_MAX = 2 * CHUNK
GATE_CAP = 15.0
EPS = 1e-6
GN_EPS = 64e-5

V7X_VMEM_BYTES = 64 * 1024 * 1024
VMEM_LIMIT = V7X_VMEM_BYTES - 4 * 1024 * 1024
LANES = 128
MATMUL_ROW_SUBTILE = 1024


def _params(*sem):
    return pltpu.CompilerParams(dimension_semantics=sem, vmem_limit_bytes=VMEM_LIMIT)


def _tile(n, pref, mult):
    if n <= pref:
        return n
    t = (pref // mult) * mult
    while t >= mult:
        if n % t == 0:
            return t
        t -= mult
    return n


def _silu(x):
    return x * jax.nn.sigmoid(x)


def _resident(block_shape, index_map):
    return pl.BlockSpec(block_shape, index_map, pipeline_mode=pl.Buffered(1))


ADA_TILE = 256


def _ada_tile(c_ref, w_ref, b_ref):
    cs = _silu(c_ref[...]).astype(BF16)
    return jnp.dot(cs, w_ref[...].astype(BF16), preferred_element_type=F32) + b_ref[...]


def _ada_kernel(c_ref, w_ref, b_ref, o_ref):
    o_ref[...] = _ada_tile(c_ref, w_ref, b_ref)


def _ada_mod(c_all, ada_w, ada_b, layer):
    _, d, n = ada_w.shape
    rows = c_all.shape[0]
    tn = _tile(n, 2 * ADA_TILE, LANES)
    return pl.pallas_call(
        _ada_kernel,
        grid=(n // tn,),
        in_specs=[
            pl.BlockSpec((rows, d), lambda j: (0, 0)),
            pl.BlockSpec((None, d, tn), lambda j: (layer, 0, j)),
            pl.BlockSpec((None, 1, tn), lambda j: (layer, 0, j)),
        ],
        out_specs=pl.BlockSpec((rows, tn), lambda j: (0, j)),
        out_shape=jax.ShapeDtypeStruct((rows, n), F32),
        compiler_params=_params("arbitrary"),
        name="ada_mod",
    )(c_all, ada_w, ada_b)


def _norm_mod_value(x, g, sc, sh):
    y = x * lax.rsqrt(jnp.mean(x * x, axis=-1, keepdims=True) + EPS)
    return (y * g) * (1.0 + sc) + sh


def _norm_mod_kernel(x_ref, g_ref, sc_ref, sh_ref, o_ref):
    o_ref[0] = _norm_mod_value(x_ref[0], g_ref[...], sc_ref[0], sh_ref[0]).astype(o_ref.dtype)


def _norm_mod(x, g, sc, sh, out_dtype=BF16):
    b, t, d = x.shape
    tt = _tile(t, 256, 16)
    row = pl.BlockSpec((1, tt, d), lambda bi, i: (bi, i, 0))
    per_batch = pl.BlockSpec((1, 1, d), lambda bi, i: (bi, 0, 0))
    return pl.pallas_call(
        _norm_mod_kernel,
        grid=(b, t // tt),
        in_specs=[row, pl.BlockSpec((1, d), lambda bi, i: (0, 0)), per_batch, per_batch],
        out_specs=row,
        out_shape=jax.ShapeDtypeStruct((b, t, d), out_dtype),
        compiler_params=_params("arbitrary", "arbitrary"),
        name="norm_mod",
    )(x, g.reshape(1, d), sc, sh)


def _final_norm_kernel(x_ref, g_ref, o_ref):
    x = x_ref[0]
    o_ref[0] = (x * lax.rsqrt(jnp.mean(x * x, axis=-1, keepdims=True) + EPS)) * g_ref[...]


def _final_norm(x, g):
    b, t, d = x.shape
    tt = _tile(t, 256, 8)
    row = pl.BlockSpec((1, tt, d), lambda bi, i: (bi, i, 0))
    return pl.pallas_call(
        _final_norm_kernel,
        grid=(b, t // tt),
        in_specs=[row, pl.BlockSpec((1, d), lambda bi, i: (0, 0))],
        out_specs=row,
        out_shape=jax.ShapeDtypeStruct((b, t, d), F32),
        compiler_params=_params("arbitrary", "arbitrary"),
        name="final_norm",
    )(x, g.reshape(1, d))


def _norm_mix_kernel(x_ref, xp_ref, s0_ref, g_ref, sc_ref, sh_ref, mu_ref, *out_refs):
    i = pl.program_id(1)
    mix_refs, last_ref = out_refs[:6], out_refs[6]
    tt, d = x_ref.shape[1], x_ref.shape[2]
    x, xp = x_ref[0], xp_ref[0]
    inv = lax.rsqrt(jnp.mean(x * x, axis=-1, keepdims=True) + EPS)
    inv_p = lax.rsqrt(jnp.mean(xp * xp, axis=-1, keepdims=True) + EPS)
    cw = LANES if d % LANES == 0 else d
    row_id = lax.broadcasted_iota(jnp.int32, (tt, cw), 0)
    for c in range(d // cw):
        cols = slice(c * cw, (c + 1) * cw)
        g, sc, sh = g_ref[:, cols], sc_ref[0, :, cols], sh_ref[0, :, cols]
        h = ((x_ref[0, :, cols] * inv) * g) * (1.0 + sc) + sh
        hp = ((xp_ref[0, :, cols] * inv_p) * g) * (1.0 + sc) + sh
        prev_row = jnp.where(i == 0, s0_ref[0, :, cols], hp[7:8, :])
        shifted = jnp.where(row_id == 0, prev_row, pltpu.roll(h, 1, axis=0))
        xx = shifted - h
        for j in range(6):
            mix_refs[j][0, :, cols] = (h + xx * mu_ref[j:j + 1, cols]).astype(BF16)
        last_ref[0, :, cols] = h[tt - 8:, :]


def _norm_mix(x, shift0, g, sc, sh, mu):
    b, t, d = x.shape
    tt = _tile(t, 256, 16)
    row = pl.BlockSpec((1, tt, d), lambda bi, i: (bi, i, 0))
    prev8 = pl.BlockSpec((1, 8, d), lambda bi, i: (bi, jnp.maximum(i * (tt // 8) - 1, 0), 0))
    per_batch = pl.BlockSpec((1, 1, d), lambda bi, i: (bi, 0, 0))
    outs = pl.pallas_call(
        _norm_mix_kernel,
        grid=(b, t // tt),
        in_specs=[row, prev8, per_batch, pl.BlockSpec((1, d), lambda bi, i: (0, 0)), per_batch, per_batch,
                  pl.BlockSpec((6, d), lambda bi, i: (0, 0))],
        out_specs=[row] * 6 + [pl.BlockSpec((1, 8, d), lambda bi, i: (bi, 0, 0))],
        out_shape=[jax.ShapeDtypeStruct((b, t, d), BF16)] * 6 + [jax.ShapeDtypeStruct((b, 8, d), F32)],
        compiler_params=_params("arbitrary", "arbitrary"),
        name="norm_mix",
    )(x, x, shift0.reshape(b, 1, d), g.reshape(1, d), sc, sh, mu)
    return outs[:6], outs[6][:, 7, :]


def _mm_rows(tm):
    sub = MATMUL_ROW_SUBTILE if tm % MATMUL_ROW_SUBTILE == 0 else tm
    return sub, tm // sub


def _row_slice(m, sub):
    return pl.ds(m * sub, sub) if isinstance(m, int) else pl.ds(pl.multiple_of(m * sub, sub), sub)


def _for_row_subtiles(n_sub, body):
    if n_sub <= 2:
        for m in range(n_sub):
            body(m, 0)
    else:
        lax.fori_loop(0, n_sub, body, 0)


def _emit_bf16_weights(wb_refs, wq_refs):
    @pl.when(pl.program_id(0) == 0)
    def _():
        for wb_ref, wq_ref in zip(wb_refs, wq_refs):
            wq_ref[...] = wb_ref[...]


def _mm_kernel(a_ref, w_ref, o_ref, *rest, tm, w_transposed, emit):
    if w_ref.dtype == BF16:
        wb_ref = w_ref
    else:
        wb_ref = rest[-1]
        wb_ref[...] = w_ref[...].astype(BF16)
    sub, n_sub = _mm_rows(tm)

    def body(m, carry):
        rows = _row_slice(m, sub)
        if w_transposed:
            acc = lax.dot_general(a_ref[rows, :], wb_ref[...], (((1,), (1,)), ((), ())),
                                  preferred_element_type=F32)
        else:
            acc = jnp.dot(a_ref[rows, :], wb_ref[...], preferred_element_type=F32)
        o_ref[rows, :] = acc.astype(o_ref.dtype)
        return carry

    _for_row_subtiles(n_sub, body)
    if emit:
        _emit_bf16_weights([wb_ref], [rest[0]])


def _mm_resid_kernel(*refs, tm, parts):
    a_refs, w_refs = refs[:parts], refs[parts:2 * parts]
    r_ref, g_ref, o_ref = refs[2 * parts:2 * parts + 3]
    wb_refs = refs[2 * parts + 3:]
    if wb_refs:
        for w_ref, wb_ref in zip(w_refs, wb_refs):
            wb_ref[...] = w_ref[...].astype(BF16)
    else:
        wb_refs = w_refs
    sub, n_sub = _mm_rows(tm)
    per_row_gate = g_ref.shape[0] != 1

    def body(m, carry):
        rows = _row_slice(m, sub)
        acc = jnp.dot(a_refs[0][rows, :], wb_refs[0][...], preferred_element_type=F32)
        for a_ref, wb_ref in zip(a_refs[1:], wb_refs[1:]):
            acc = acc + jnp.dot(a_ref[rows, :], wb_ref[...], preferred_element_type=F32)
        gate = g_ref[rows, :] if per_row_gate else g_ref[...]
        o_ref[rows, :] = r_ref[rows, :] + gate * acc
        return carry

    _for_row_subtiles(n_sub, body)


def _mm_swiglu_kernel(a_ref, w1_ref, w3_ref, *rest, tm, emit, ada_tiles, cast_w2):
    rest = list(rest)
    if ada_tiles:
        c_ref, aw_ref, ab_ref = rest[:3]
        del rest[:3]
    if cast_w2:
        w2_ref = rest.pop(0)
    o_ref = rest.pop(0)
    if emit:
        wq_refs = rest[:2]
        del rest[:2]
        w1b_ref, w3b_ref = rest[-2:]
        w1b_ref[...] = w1_ref[...].astype(BF16)
        w3b_ref[...] = w3_ref[...].astype(BF16)
    else:
        w1b_ref, w3b_ref = w1_ref, w3_ref
    sub, n_sub = _mm_rows(tm)

    def body(m, carry):
        rows = _row_slice(m, sub)
        a = a_ref[rows, :]
        u = jnp.dot(a, w1b_ref[...], preferred_element_type=F32)
        v = jnp.dot(a, w3b_ref[...], preferred_element_type=F32)
        o_ref[rows, :] = (_silu(u) * v).astype(o_ref.dtype)
        return carry

    _for_row_subtiles(n_sub, body)
    if ada_tiles:
        mod_ref = rest.pop(0)
    if cast_w2:
        rest.pop(0)[...] = w2_ref[...].astype(BF16)
    if emit:
        _emit_bf16_weights([w1b_ref, w3b_ref], wq_refs)
    if ada_tiles:
        step = pl.program_id(0) * pl.num_programs(1) + pl.program_id(1)

        @pl.when(step < ada_tiles)
        def _():
            mod_ref[...] = _ada_tile(c_ref, aw_ref, ab_ref)


def _parked(n_tiles):
    return lambda i, j: jnp.where(i == 0, j, n_tiles - 1)


def _weight_spec(w, layer, k, tn):
    if w.ndim == 3:
        return pl.BlockSpec((None, k, tn), lambda i, j: (layer, 0, j))
    return pl.BlockSpec((k, tn), lambda i, j: (0, j))


def _matmul(a, w, *, layer=0, emit=False, tm_pref=2048, tn_pref=512):
    m, k = a.shape
    n = w.shape[-1]
    tm = _tile(m, tm_pref, MATMUL_ROW_SUBTILE)
    tn = _tile(n, tn_pref, LANES)
    park = _parked(n // tn)
    out_specs = [pl.BlockSpec((tm, tn), lambda i, j: (i, j))]
    out_shape = [jax.ShapeDtypeStruct((m, n), F32)]
    if emit:
        out_specs.append(pl.BlockSpec((k, tn), lambda i, j: (0, park(i, j))))
        out_shape.append(jax.ShapeDtypeStruct((k, n), BF16))
    outs = pl.pallas_call(
        functools.partial(_mm_kernel, tm=tm, w_transposed=False, emit=emit),
        grid=(m // tm, n // tn),
        in_specs=[_resident((tm, k), lambda i, j: (i, 0)), _weight_spec(w, layer, k, tn)],
        out_specs=out_specs,
        out_shape=out_shape,
        scratch_shapes=[pltpu.VMEM((k, tn), BF16)] if w.dtype != BF16 else [],
        compiler_params=_params("arbitrary", "arbitrary"),
        name="matmul",
    )(a, w)
    return tuple(outs) if emit else outs[0]


def _matmul_wt(a, wt, *, layer, row0, n, emit=False, tm_pref=2048, tn_pref=512):
    m, k = a.shape
    tm = _tile(m, tm_pref, MATMUL_ROW_SUBTILE)
    tn = _tile(n, tn_pref, LANES)
    assert row0 % tn == 0 and n % tn == 0
    park = _parked(n // tn)
    out_specs = [pl.BlockSpec((tm, tn), lambda i, j: (i, j))]
    out_shape = [jax.ShapeDtypeStruct((m, n), F32)]
    if emit:
        out_specs.append(pl.BlockSpec((tn, k), lambda i, j: (park(i, j), 0)))
        out_shape.append(jax.ShapeDtypeStruct((n, k), BF16))
    outs = pl.pallas_call(
        functools.partial(_mm_kernel, tm=tm, w_transposed=True, emit=emit),
        grid=(m // tm, n // tn),
        in_specs=[_resident((tm, k), lambda i, j: (i, 0)),
                  pl.BlockSpec((None, tn, k), lambda i, j: (layer, row0 // tn + j, 0))],
        out_specs=out_specs,
        out_shape=out_shape,
        scratch_shapes=[pltpu.VMEM((tn, k), BF16)] if wt.dtype != BF16 else [],
        compiler_params=_params("arbitrary", "arbitrary"),
        name="matmul_wt",
    )(a, wt)
    return tuple(outs) if emit else outs[0]


def _matmul_resid(a_parts, w, res, gate, *, layer=0, tm_pref=2048, tn_pref=512):
    parts = len(a_parts)
    m = a_parts[0].shape[0]
    ks = [a.shape[1] for a in a_parts]
    offs = [sum(ks[:p]) for p in range(parts)]
    n = w.shape[-1]
    assert w.shape[-2] == sum(ks) and all(off % k == 0 for off, k in zip(offs, ks))
    tm = _tile(m, tm_pref, MATMUL_ROW_SUBTILE)
    tn = _tile(n, tn_pref, LANES)
    if gate.shape[0] == 1:
        gate_spec = pl.BlockSpec((1, tn), lambda i, j: (0, j))
    else:
        gate_spec = pl.BlockSpec((tm, tn), lambda i, j: (i, j))
    if w.ndim == 3:
        w_specs = [pl.BlockSpec((None, k, tn), functools.partial(lambda i, j, blk: (layer, blk, j), blk=off // k))
                   for off, k in zip(offs, ks)]
    else:
        w_specs = [pl.BlockSpec((k, tn), functools.partial(lambda i, j, blk: (blk, j), blk=off // k))
                   for off, k in zip(offs, ks)]
    staged = w.dtype != BF16
    return pl.pallas_call(
        functools.partial(_mm_resid_kernel, tm=tm, parts=parts),
        grid=(m // tm, n // tn),
        in_specs=[_resident((tm, k), lambda i, j: (i, 0)) for k in ks] + w_specs
                 + [pl.BlockSpec((tm, tn), lambda i, j: (i, j)), gate_spec],
        out_specs=pl.BlockSpec((tm, tn), lambda i, j: (i, j)),
        out_shape=jax.ShapeDtypeStruct((m, n), F32),
        scratch_shapes=[pltpu.VMEM((k, tn), BF16) for k in ks] if staged else [],
        compiler_params=_params("arbitrary", "arbitrary"),
        name="matmul_resid",
    )(*a_parts, *([w] * parts), res, gate)


def _matmul_swiglu(a, w1, w3, *, layer=0, emit=False, ada=None, w2=None, tm_pref=2048, tn_pref=256):
    m, k = a.shape
    n = w1.shape[-1]
    assert emit == (w1.dtype != BF16)
    tm = _tile(m, tm_pref, MATMUL_ROW_SUBTILE)
    tn = _tile(n, tn_pref, LANES)
    n_i, n_j = m // tm, n // tn
    steps = n_i * n_j
    if ada is not None:
        c_all, ada_w, ada_b, next_layer = ada
        rows, d = c_all.shape
        n_mod = ada_w.shape[-1]
        if n_mod % ADA_TILE != 0 or n_mod // ADA_TILE > steps:
            outs = _matmul_swiglu(a, w1, w3, layer=layer, emit=emit, w2=w2, tm_pref=tm_pref, tn_pref=tn_pref)
            outs = list(outs) if isinstance(outs, tuple) else [outs]
            outs.insert(3 if emit else 1, _ada_mod(c_all, ada_w, ada_b, next_layer))
            return tuple(outs)
    if w2 is not None and (w2.shape[1] % steps != 0 or (w2.shape[1] // steps) % 16 != 0):
        outs = _matmul_swiglu(a, w1, w3, layer=layer, emit=emit, ada=ada, tm_pref=tm_pref, tn_pref=tn_pref)
        outs = outs if isinstance(outs, tuple) else (outs,)
        return outs + (w2[layer].astype(BF16),)
    step_of = lambda i, j: i * n_j + j
    in_specs = [_resident((tm, k), lambda i, j: (i, 0)), _weight_spec(w1, layer, k, tn),
                _weight_spec(w3, layer, k, tn)]
    args = [a, w1, w3]
    out_specs = [pl.BlockSpec((tm, tn), lambda i, j: (i, j))]
    out_shape = [jax.ShapeDtypeStruct((m, n), BF16)]
    if emit:
        park = _parked(n_j)
        out_specs += [pl.BlockSpec((k, tn), lambda i, j: (0, park(i, j)))] * 2
        out_shape += [jax.ShapeDtypeStruct((k, n), BF16)] * 2
    ada_tiles = 0
    if ada is not None:
        ada_tiles = n_mod // ADA_TILE
        tile_of = lambda i, j: jnp.minimum(step_of(i, j), ada_tiles - 1)
        in_specs += [pl.BlockSpec((rows, d), lambda i, j: (0, 0)),
                     pl.BlockSpec((None, d, ADA_TILE), lambda i, j: (next_layer, 0, tile_of(i, j))),
                     pl.BlockSpec((None, 1, ADA_TILE), lambda i, j: (next_layer, 0, tile_of(i, j)))]
        args += [c_all, ada_w, ada_b]
        out_specs.append(pl.BlockSpec((rows, ADA_TILE), lambda i, j: (0, tile_of(i, j))))
        out_shape.append(jax.ShapeDtypeStruct((rows, n_mod), F32))
    if w2 is not None:
        k2, n2 = w2.shape[1:]
        rows2 = k2 // steps
        in_specs.append(pl.BlockSpec((None, rows2, n2), lambda i, j: (layer, step_of(i, j), 0)))
        args.append(w2)
        out_specs.append(pl.BlockSpec((rows2, n2), lambda i, j: (step_of(i, j), 0)))
        out_shape.append(jax.ShapeDtypeStruct((k2, n2), BF16))
    outs = pl.pallas_call(
        functools.partial(_mm_swiglu_kernel, tm=tm, emit=emit, ada_tiles=ada_tiles, cast_w2=w2 is not None),
        grid=(n_i, n_j),
        in_specs=in_specs,
        out_specs=out_specs,
        out_shape=out_shape,
        scratch_shapes=[pltpu.VMEM((k, tn), BF16)] * 2 if emit else [],
        compiler_params=_params("arbitrary", "arbitrary"),
        name="matmul_swiglu",
    )(*args)
    return tuple(outs) if len(outs) > 1 else outs[0]


def _lora_kernel(a_ref, w1_ref, w2_ref, *rest, mid, epilogue):
    o_ref = rest[-1]
    z = jnp.dot(a_ref[...], w1_ref[...], preferred_element_type=F32)
    if mid == "tanh":
        z = jnp.tanh(z)
    elif mid == "sigmoid":
        z = jax.nn.sigmoid(z)
    z = jnp.dot(z.astype(BF16), w2_ref[...], preferred_element_type=F32)
    if epilogue == "log_decay":
        o_ref[...] = (-float(np.exp(-0.5))) * jax.nn.sigmoid(rest[0][...] + z)
    elif epilogue == "sigmoid":
        o_ref[...] = jax.nn.sigmoid(rest[0][...] + z)
    elif epilogue == "vmix":
        v, vf = rest[1][...], rest[2][...]
        o_ref[...] = v + (vf - v) * jax.nn.sigmoid(rest[0][...] + z)
    else:
        o_ref[...] = z


def _lora(a, w1, w2, *, mid, epilogue, bias=None, extra=()):
    m, d = a.shape
    r = w1.shape[1]
    rp = -(-r // LANES) * LANES
    w1p = jnp.pad(w1, ((0, 0), (0, rp - r))).astype(BF16)
    w2p = jnp.pad(w2, ((0, rp - r), (0, 0))).astype(BF16)
    n = w2.shape[1]
    tm = _tile(m, 256, 16)
    row = pl.BlockSpec((tm, n), lambda i: (i, 0))
    in_specs = [pl.BlockSpec((tm, d), lambda i: (i, 0)), _resident((d, rp), lambda i: (0, 0)),
                _resident((rp, n), lambda i: (0, 0))]
    args = [a, w1p, w2p]
    if bias is not None:
        in_specs.append(pl.BlockSpec((1, n), lambda i: (0, 0)))
        args.append(bias.reshape(1, n))
    for e in extra:
        in_specs.append(row)
        args.append(e)
    return pl.pallas_call(
        functools.partial(_lora_kernel, mid=mid, epilogue=epilogue),
        grid=(m // tm,),
        in_specs=in_specs,
        out_specs=row,
        out_shape=jax.ShapeDtypeStruct((m, n), F32),
        compiler_params=_params("arbitrary"),
        name="lora_" + epilogue,
    )(*args)


def _dot_nt(a, b):
    return lax.dot_general(a, b, (((1,), (1,)), ((), ())), preferred_element_type=F32)


def _dot_tn(a, b):
    return lax.dot_general(a, b, (((0,), (0,)), ((), ())), preferred_element_type=F32)


MLSTM_CHUNK = 256


def _soft_cap(x):
    return GATE_CAP * jnp.tanh(x / GATE_CAP)


def _mlstm_kernel(q_ref, k_ref, v_ref, o_ref, gc_ref, gr_ref, bc_ref, br_ref, ng_ref, c0_ref, n0_ref, m0_ref,
                  h_ref, c_ref, n_ref, m_ref, *, heads, dk, dv, length):
    c_idx = pl.program_id(1)

    @pl.when(c_idx == 0)
    def _():
        c_ref[...] = c0_ref[...]
        n_ref[...] = n0_ref[...]
        m_ref[...] = m0_ref[...]

    gcol = gc_ref[...] + bc_ref[...]
    grow = gr_ref[...] + br_ref[...]
    li_col, lf_col = _soft_cap(gcol[:, :heads]), jax.nn.log_sigmoid(_soft_cap(gcol[:, heads:]))
    li_row, lf_row = _soft_cap(grow[:heads, :]), jax.nn.log_sigmoid(_soft_cap(grow[heads:, :]))
    r_id = lax.broadcasted_iota(jnp.int32, (length, length), 0)
    c_id = lax.broadcasted_iota(jnp.int32, (length, length), 1)
    tril = c_id <= r_id
    scale = dk ** -0.5

    hs = range(heads)
    gates = []
    for h in hs:
        lf_r, li_r = lf_row[h:h + 1, :], li_row[h:h + 1, :]
        lf_c, li_c = lf_col[:, h:h + 1], li_col[:, h:h + 1]
        b_col = jnp.sum(jnp.where(tril, lf_r, 0.0), axis=1, keepdims=True)
        b_row = jnp.sum(jnp.where(r_id <= c_id, lf_c, 0.0), axis=0, keepdims=True)
        b_last = b_col[length - 1:length, :]
        m_prev = m_ref[0, h:h + 1, :]
        log_d = jnp.where(tril, b_col - b_row + li_r, -jnp.inf)
        inter = b_col + m_prev
        m_t = jnp.maximum(inter, jnp.max(log_d, axis=1, keepdims=True))
        m_new = m_t[length - 1:length, :]
        gates.append(dict(p=jnp.exp(log_d - m_t), g=jnp.exp(inter - m_t), m_t=m_t, m_new=m_new,
                          wk_col=jnp.exp(b_last - b_col + li_c - m_new),
                          decay=jnp.exp(b_last + m_prev - m_new)))

    q = [q_ref[:, h * dk:(h + 1) * dk] for h in hs]
    k = [k_ref[:, h * dk:(h + 1) * dk] * scale for h in hs]
    kw = [k[h] * gates[h]["wk_col"] for h in hs]
    qb = [x.astype(BF16) for x in q]
    kb = [x.astype(BF16) for x in k]
    vb = [v_ref[:, h * dv:(h + 1) * dv].astype(BF16) for h in hs]
    c_state = [c_ref[0, h] for h in hs]
    n_state = [n_ref[0, h:h + 1, :] for h in hs]
    s_qk = [_dot_nt(qb[h], kb[h]) for h in hs]
    q_c = [jnp.dot(qb[h], c_state[h].astype(BF16), preferred_element_type=F32) for h in hs]
    kv = [_dot_tn(kw[h].astype(BF16), vb[h]) for h in hs]
    wqk = [gates[h]["p"] * s_qk[h] for h in hs]
    num = [jnp.dot(wqk[h].astype(BF16), vb[h], preferred_element_type=F32) + gates[h]["g"] * q_c[h] for h in hs]

    for h in hs:
        g = gates[h]
        den = jnp.sum(wqk[h], axis=1, keepdims=True) + g["g"] * jnp.sum(q[h] * n_state[h], axis=1, keepdims=True)
        hh = num[h] / jnp.maximum(jnp.abs(den), jnp.exp(-g["m_t"]))
        hh = hh * lax.rsqrt(jnp.mean(hh * hh, axis=1, keepdims=True) + EPS)
        hh = hh * ng_ref[:, h * dv:(h + 1) * dv]
        h_ref[:, h * dv:(h + 1) * dv] = (jax.nn.sigmoid(o_ref[:, h * dv:(h + 1) * dv]) * hh).astype(h_ref.dtype)
        c_ref[0, h] = g["decay"] * c_state[h] + kv[h]
        n_ref[0, h:h + 1, :] = g["decay"] * n_state[h] + jnp.sum(kw[h], axis=0, keepdims=True)
        m_ref[0, h:h + 1, :] = g["m_new"]


def _mlstm(proj, gif, b_if, a_norm_g, c0, n0, m0, *, batch, seq):
    _, heads, dk, dv = c0.shape
    length = _tile(seq, MLSTM_CHUNK, CHUNK)
    nc = seq // length
    wq, wv = heads * dk, heads * dv
    assert wq == wv
    gcol = gif
    grow = gif.reshape(batch, nc, length, 2 * heads).transpose(0, 1, 3, 2)
    row = lambda col: pl.BlockSpec((length, wq), lambda b, c: (b * nc + c, col))
    state4 = pl.BlockSpec((1, heads, dk, dv), lambda b, c: (b, 0, 0, 0))
    state3 = pl.BlockSpec((1, heads, dk), lambda b, c: (b, 0, 0))
    state_m = pl.BlockSpec((1, heads, 1), lambda b, c: (b, 0, 0))
    h, c_out, n_out, m_out = pl.pallas_call(
        functools.partial(_mlstm_kernel, heads=heads, dk=dk, dv=dv, length=length),
        grid=(batch, nc),
        in_specs=[row(0), row(1), row(2), row(3),
                  pl.BlockSpec((length, 2 * heads), lambda b, c: (b * nc + c, 0)),
                  pl.BlockSpec((None, None, 2 * heads, length), lambda b, c: (b, c, 0, 0)),
                  pl.BlockSpec((1, 2 * heads), lambda b, c: (0, 0)),
                  pl.BlockSpec((2 * heads, 1), lambda b, c: (0, 0)),
                  pl.BlockSpec((1, wv), lambda b, c: (0, 0)),
                  state4, state3, state_m],
        out_specs=[pl.BlockSpec((length, wv), lambda b, c: (b * nc + c, 0)), state4, state3, state_m],
        out_shape=[jax.ShapeDtypeStruct((batch * seq, wv), BF16),
                   jax.ShapeDtypeStruct((batch, heads, dk, dv), F32),
                   jax.ShapeDtypeStruct((batch, heads, dk), F32),
                   jax.ShapeDtypeStruct((batch, heads, 1), F32)],
        compiler_params=_params("arbitrary", "arbitrary"),
        name="mlstm",
    )(proj, proj, proj, proj, gcol, grow, b_if.reshape(1, 2 * heads), b_if.reshape(2 * heads, 1),
      a_norm_g.reshape(1, wv), c0, n0, m0.reshape(batch, heads, 1))
    return h, c_out, n_out, m_out.reshape(batch, heads)


def _rel_bias_table(rel_bias):
    n_e, heads, rel_size = rel_bias.shape
    band = BAND_PAST + CHUNK
    i = np.arange(CHUNK)[:, None]
    j = np.arange(band)[None, :]
    rel = (np.clip(i - j + BAND_PAST, -(CHUNK - 1), REL_MAX) + (CHUNK - 1)).reshape(-1)
    onehot = (jnp.asarray(rel, jnp.int32)[None, :] == jnp.arange(rel_size, dtype=jnp.int32)[:, None]).astype(F32)
    table = jnp.dot(rel_bias.reshape(n_e * heads, rel_size), onehot, precision=lax.Precision.HIGHEST)
    return table.reshape(n_e, heads, CHUNK, band)


def _band_chunks(qs, ks, vs, biases, scale):
    n = range(len(qs))
    s = [_dot_nt(qs[i], ks[i]) * scale + biases[i] for i in n]
    p = [jnp.exp(s[i] - jnp.max(s[i], axis=1, keepdims=True)) for i in n]
    o = [jnp.dot(p[i].astype(BF16), vs[i], preferred_element_type=F32) for i in n]
    return [o[i] / jnp.sum(p[i], axis=1, keepdims=True) for i in n]


ATTN_CHUNKS_PER_ITER = 8


def _attn_prompt_kernel(q_ref, k_ref, v_ref, bias_ref, o_ref, kb_ref, vb_ref, *, seq, scale):
    kb_ref[...] = k_ref[...].astype(BF16)
    vb_ref[...] = v_ref[...].astype(BF16)
    nc = seq // CHUNK
    band = BAND_PAST + CHUNK
    bias = bias_ref[0]

    lead = list(range(min(N_PREV_CHUNKS, nc)))
    for c0 in range(0, len(lead), ATTN_CHUNKS_PER_ITER):
        cs = lead[c0:c0 + ATTN_CHUNKS_PER_ITER]
        widths = [(c + 1) * CHUNK for c in cs]
        outs = _band_chunks([q_ref[c * CHUNK:(c + 1) * CHUNK, :].astype(BF16) for c in cs],
                            [kb_ref[0:w, :] for w in widths], [vb_ref[0:w, :] for w in widths],
                            [bias[:, band - w:] for w in widths], scale)
        for c, o in zip(cs, outs):
            o_ref[c * CHUNK:(c + 1) * CHUNK, :] = o.astype(o_ref.dtype)

    rest = nc - N_PREV_CHUNKS
    per_iter = ATTN_CHUNKS_PER_ITER if rest % ATTN_CHUNKS_PER_ITER == 0 else 1

    def body(it, carry):
        starts = [pl.multiple_of((N_PREV_CHUNKS + it * per_iter + u) * CHUNK, CHUNK) for u in range(per_iter)]
        k_rows = [pl.ds(pl.multiple_of(s - BAND_PAST, CHUNK), band) for s in starts]
        outs = _band_chunks([q_ref[pl.ds(s, CHUNK), :].astype(BF16) for s in starts],
                            [kb_ref[r, :] for r in k_rows], [vb_ref[r, :] for r in k_rows],
                            [bias] * per_iter, scale)
        for s, o in zip(starts, outs):
            o_ref[pl.ds(s, CHUNK), :] = o.astype(o_ref.dtype)
        return carry

    if rest > 0:
        lax.fori_loop(0, rest // per_iter, body, 0)


def _attn_prompt(proj_b, bias_table, *, batch, seq, heads, dh):
    col = lambda base: pl.BlockSpec((seq, dh), lambda b, h: (b, base * heads + h))
    return pl.pallas_call(
        functools.partial(_attn_prompt_kernel, seq=seq, scale=dh ** -0.5),
        grid=(batch, heads),
        in_specs=[col(0), col(1), col(2),
                  pl.BlockSpec((1, CHUNK, BAND_PAST + CHUNK), lambda b, h: (h, 0, 0))],
        out_specs=pl.BlockSpec((seq, dh), lambda b, h: (b, h)),
        out_shape=jax.ShapeDtypeStruct((batch * seq, heads * dh), BF16),
        scratch_shapes=[pltpu.VMEM((seq, dh), BF16), pltpu.VMEM((seq, dh), BF16)],
        compiler_params=_params("arbitrary", "arbitrary"),
        name="attn_prompt",
    )(proj_b, proj_b, proj_b, bias_table)


def _attn_sample_kernel(q_ref, k_ref, v_ref, pk_ref, pv_ref, bias_ref, o_ref, *, seq, width, heads, dh, scale):
    off = BAND_PAST - width
    hs = range(heads)
    cols = [slice(h * dh, (h + 1) * dh) for h in hs]
    rows = [pl.ds(h, width, stride=heads) for h in hs]
    q = [q_ref[:, c].astype(BF16) for c in cols]
    s_past = [_dot_nt(q[h], pk_ref[rows[h], :].astype(BF16)) * scale + bias_ref[h, :seq, off:off + width] for h in hs]
    s_new = [_dot_nt(q[h], k_ref[:, cols[h]].astype(BF16)) * scale + bias_ref[h, :seq, BAND_PAST:BAND_PAST + seq]
             for h in hs]
    m = [jnp.maximum(jnp.max(s_past[h], axis=1, keepdims=True), jnp.max(s_new[h], axis=1, keepdims=True)) for h in hs]
    p_past = [jnp.exp(s_past[h] - m[h]) for h in hs]
    p_new = [jnp.exp(s_new[h] - m[h]) for h in hs]
    o = [jnp.dot(p_past[h].astype(BF16), pv_ref[rows[h], :].astype(BF16), preferred_element_type=F32)
         + jnp.dot(p_new[h].astype(BF16), v_ref[:, cols[h]].astype(BF16), preferred_element_type=F32) for h in hs]
    for h in hs:
        denom = jnp.sum(p_past[h], axis=1, keepdims=True) + jnp.sum(p_new[h], axis=1, keepdims=True)
        o_ref[:, cols[h]] = (o[h] / denom).astype(o_ref.dtype)


def _attn_sample(proj_b, past_k, past_v, layer, bias_table, *, batch, seq, heads, dh):
    width = past_k.shape[2] // heads
    wb = heads * dh
    col = lambda base: pl.BlockSpec((seq, wb), lambda b: (b, base))
    past = pl.BlockSpec((None, None, width * heads, dh), lambda b: (layer, b, 0, 0))
    return pl.pallas_call(
        functools.partial(_attn_sample_kernel, seq=seq, width=width, heads=heads, dh=dh, scale=dh ** -0.5),
        grid=(batch,),
        in_specs=[col(0), col(1), col(2), past, past,
                  pl.BlockSpec((heads, CHUNK, BAND_PAST + CHUNK), lambda b: (0, 0, 0))],
        out_specs=pl.BlockSpec((seq, wb), lambda b: (b, 0)),
        out_shape=jax.ShapeDtypeStruct((batch * seq, wb), BF16),
        compiler_params=_params("arbitrary"),
        name="attn_sample",
    )(proj_b, proj_b, proj_b, past_k, past_v, bias_table)


def _split_dot(a_exact, x):
    hi = x.astype(BF16)
    lo = (x - hi.astype(F32)).astype(BF16)
    return jnp.dot(a_exact, hi, preferred_element_type=F32) + jnp.dot(a_exact, lo, preferred_element_type=F32)


def _block_diag_rows(x, lo_mask):
    return jnp.concatenate([jnp.where(lo_mask, x, 0.0), jnp.where(lo_mask, 0.0, x)], axis=0).astype(BF16)


def _rwkv_kernel(r_ref, lw_ref, k_ref, v_ref, a_ref, g_ref, kk_ref, ka_ref, rk_ref, lg_ref, lb_ref, s0_ref,
                 o_ref, s_ref, s2_ref, *, pairs, dh, length, n_chunks):
    c_idx = pl.program_id(2)
    pw_ = 2 * dh
    ps = range(pairs)

    @pl.when(c_idx == 0)
    def _():
        for p in ps:
            s2_ref[p] = jnp.concatenate([s0_ref[0, 2 * p], s0_ref[0, 2 * p + 1]], axis=1)

    def lane_lo(shape, half):
        return lax.broadcasted_iota(jnp.int32, shape, 1) < half

    f_lo = lane_lo((length, pw_), dh)
    f_lo_s = lane_lo((dh, pw_), dh)
    t_lo = lane_lo((length, 2 * length), length)
    r_id = lax.broadcasted_iota(jnp.int32, (length, length), 0)
    c_id = lax.broadcasted_iota(jnp.int32, (length, length), 1)
    tril_bf = jnp.where(c_id <= r_id, 1.0, 0.0).astype(BF16)
    row2 = lax.broadcasted_iota(jnp.int32, (length, 2 * length), 0)
    col2 = lax.broadcasted_iota(jnp.int32, (length, 2 * length), 1) & (length - 1)
    strict2 = col2 < row2
    row4 = lax.broadcasted_iota(jnp.int32, (length, 4 * length), 0)
    col4 = lax.broadcasted_iota(jnp.int32, (length, 4 * length), 1) & (length - 1)
    incl4 = col4 <= row4
    n_double = max(int(np.ceil(np.log2(length))), 1)

    def head_sums(x, lo_mask):
        s_lo = jnp.sum(jnp.where(lo_mask, x, 0.0), axis=1, keepdims=True)
        s_hi = jnp.sum(jnp.where(lo_mask, 0.0, x), axis=1, keepdims=True)
        return jnp.where(lo_mask, s_lo, s_hi)

    r_all, lw_all, k_all, v_all, a_all = r_ref[...], lw_ref[...], k_ref[...], v_ref[...], a_ref[...]
    cum = _split_dot(tril_bf, lw_all)
    w_in = jnp.exp(cum)
    w_inv = jnp.exp(-cum)
    w_ex = jnp.exp(cum - lw_all)
    kk_all = k_all * kk_ref[...]
    k2_all = k_all * (1.0 + (a_all - 1.0) * ka_ref[...])
    rt_all = r_all * w_in
    kt_all = k2_all * w_inv
    ba_all = a_all * w_inv
    rk2_all = r_all * k2_all * rk_ref[...]
    sl = lambda x, p: x[:, p * pw_:(p + 1) * pw_]

    lhs, rhs, rhs_bd = [], [], []
    for p in ps:
        kk = sl(kk_all, p)
        kk = kk / jnp.maximum(jnp.sqrt(head_sums(kk * kk, f_lo)), 1e-12)
        at = (-kk) * sl(w_ex, p)
        bt = kk * sl(ba_all, p)
        kt = sl(kt_all, p)
        lhs.append(jnp.concatenate([at, sl(rt_all, p)], axis=0).astype(BF16))
        rhs.append(jnp.concatenate([bt, kt], axis=0).astype(BF16))
        rhs_bd.append(jnp.concatenate([_block_diag_rows(bt, f_lo), _block_diag_rows(kt, f_lo)], axis=0))
    s0 = [s2_ref[p] for p in ps]
    vf = [sl(v_all, p) for p in ps]
    v_bd = [_block_diag_rows(vf[p], f_lo) for p in ps]
    aals = [_dot_nt(lhs[p], jnp.concatenate([rhs_bd[p], _block_diag_rows(s0[p], f_lo_s)], axis=0)) for p in ps]
    aa = [z[:, :4 * length] for z in aals]
    ls = [z[:, 4 * length:] for z in aals]
    x = [ls[p][:length] + jnp.dot(jnp.where(strict2, aa[p][:length, 2 * length:], 0.0).astype(BF16), v_bd[p],
                                  preferred_element_type=F32) for p in ps]
    pw = [jnp.where(strict2, aa[p][:length, :2 * length], 0.0) for p in ps]
    for j in range(n_double):
        pwb = [q.astype(BF16) for q in pw]
        if j + 1 < n_double:
            both = [jnp.dot(pwb[p], jnp.concatenate([_block_diag_rows(x[p], f_lo), _block_diag_rows(pw[p], t_lo)],
                                                    axis=1), preferred_element_type=F32) for p in ps]
            x = [x[p] + both[p][:, :pw_] for p in ps]
            pw = [both[p][:, pw_:] for p in ps]
        else:
            x = [x[p] + jnp.dot(pwb[p], _block_diag_rows(x[p], f_lo), preferred_element_type=F32) for p in ps]
    uv_bd = [jnp.concatenate([_block_diag_rows(x[p], f_lo), v_bd[p]], axis=0) for p in ps]
    y = [ls[p][length:] + jnp.dot(jnp.where(incl4, aa[p][length:, :], 0.0).astype(BF16), uv_bd[p],
                                  preferred_element_type=F32) for p in ps]
    uv = [jnp.concatenate([x[p], vf[p]], axis=0).astype(BF16) for p in ps]
    ds = [_dot_tn(uv[p], rhs[p]) for p in ps]
    for p in ps:
        delta = jnp.where(f_lo_s, ds[p][:dh, :], ds[p][dh:, :])
        s2_ref[p] = (s0[p] + delta) * sl(w_in, p)[length - 1:length, :]

    inv_dh = 1.0 / dh
    for p in ps:
        cols = slice(p * pw_, (p + 1) * pw_)
        yc = y[p] - head_sums(y[p], f_lo) * inv_dh
        yn = yc * lax.rsqrt(head_sums(yc * yc, f_lo) * inv_dh + GN_EPS)
        yn = yn * lg_ref[:, cols] + lb_ref[:, cols]
        bonus = head_sums(sl(rk2_all, p), f_lo) * vf[p]
        o_ref[:, cols] = ((yn + bonus) * g_ref[:, cols]).astype(o_ref.dtype)

    @pl.when(c_idx == n_chunks - 1)
    def _():
        for p in ps:
            s_ref[0, 2 * p] = s2_ref[p][:, :dh]
            s_ref[0, 2 * p + 1] = s2_ref[p][:, dh:]


def _rwkv(r, lw, k, v, a, g, k_k, k_a, r_k, lnx_g, lnx_b, s0, *, batch, seq):
    _, heads, dh, _ = s0.shape
    d = heads * dh
    length = min(CHUNK, seq)
    nc = seq // length
    assert 2 * dh == LANES and length & (length - 1) == 0 and heads % 2 == 0
    hg = _tile(heads, 32 if length >= CHUNK else 16, 2)
    wg = hg * dh
    row = pl.BlockSpec((length, wg), lambda b, gi, c: (b * nc + c, gi))
    par = pl.BlockSpec((1, wg), lambda b, gi, c: (0, gi))
    state = pl.BlockSpec((1, hg, dh, dh), lambda b, gi, c: (b, gi, 0, 0))
    return pl.pallas_call(
        functools.partial(_rwkv_kernel, pairs=hg // 2, dh=dh, length=length, n_chunks=nc),
        grid=(batch, heads // hg, nc),
        in_specs=[row] * 6 + [par] * 5 + [state],
        out_specs=[row, state],
        out_shape=[jax.ShapeDtypeStruct((batch * seq, d), BF16), jax.ShapeDtypeStruct(s0.shape, F32)],
        scratch_shapes=[pltpu.VMEM((hg // 2, dh, 2 * dh), F32)],
        compiler_params=_params("arbitrary", "arbitrary", "arbitrary"),
        name="rwkv",
    )(r, lw, k, v, a, g, k_k.reshape(1, d), k_a.reshape(1, d), r_k.reshape(1, d), lnx_g.reshape(1, d),
      lnx_b.reshape(1, d), s0)


def _trunk(x, rows, ada, P, st, shared):
    emit = st is None
    c_all, ada_w, ada_b = ada
    wq = shared
    b, t, d = x.shape
    m = b * t
    depth = P["ada_w"].shape[0]
    _, heads_a, dk, dv = P["a_shape"]
    heads_b, dh_b = P["b_shape"]
    _, heads_c, dh_c, _ = P["c_shape"]
    w_a = heads_a * dk
    w_b = heads_b * dh_b
    a_C, a_n, a_m, b_k, b_v, c_S, c_sh = [], [], [], [], [], [], []
    v_first = None

    def gate_rows(gt):
        return gt if b == 1 else jnp.broadcast_to(gt[:, None, :], (b, t, d)).reshape(m, d)

    for l in range(depth):
        if emit and l == 0:
            shared["mod", 0] = _ada_mod(c_all, ada_w, ada_b, 0)
        sh1, sc1, gt1, sh2, sc2, gt2 = (z.reshape(b, 1, d) for z in jnp.split(shared["mod", l][rows], 6, axis=-1))
        if l % 2 == 0:
            e = l // 2
            h = _norm_mod(x, P["norm1_g"][l], sc1, sh1).reshape(m, d)
            if st is None:
                c0 = jnp.zeros((b, heads_a, dk, dv), F32)
                n0 = jnp.zeros((b, heads_a, dk), F32)
                m0 = jnp.zeros((b, heads_a), F32)
            else:
                c0, n0, m0 = st["a_C"][e], st["a_n"][e], st["a_m"][e]
            gif = _matmul_wt(h, P["ab_wt"], layer=e, row0=4 * w_a, n=2 * heads_a)
            if emit:
                proj_a, wq["a", e] = _matmul_wt(h, P["ab_wt"], layer=e, row0=0, n=4 * w_a, emit=True)
                proj_b, wq["b", e] = _matmul_wt(h, P["ab_wt_b"], layer=e, row0=0, n=3 * w_b, emit=True)
            else:
                proj_a = _matmul_wt(h, wq["a", e][None], layer=0, row0=0, n=4 * w_a)
                proj_b = _matmul_wt(h, wq["b", e][None], layer=0, row0=0, n=3 * w_b)
            ha, C, n, mm = _mlstm(proj_a, gif, P["ab_b_if"][e], P["a_norm_g"][e], c0, n0, m0, batch=b, seq=t)
            if st is None:
                hb = _attn_prompt(proj_b, P["bias_table"][e], batch=b, seq=t, heads=heads_b, dh=dh_b)
                keep = min(BAND_PAST, t)
            else:
                hb = _attn_sample(proj_b, st["b_k"], st["b_v"], e, P["bias_table"][e],
                                  batch=b, seq=t, heads=heads_b, dh=dh_b)
                keep = t
            kept = proj_b.reshape(b, t, 3 * w_b)[:, t - keep:, :]
            a_C.append(C)
            a_n.append(n)
            a_m.append(mm)
            b_k.append(kept[:, :, w_b:2 * w_b].reshape(b, keep, heads_b, dh_b))
            b_v.append(kept[:, :, 2 * w_b:].reshape(b, keep, heads_b, dh_b))
            x = _matmul_resid([ha, hb], P["ab_w_out"], x.reshape(m, d), gate_rows(gt1.reshape(b, d)), layer=e)
        else:
            o = l // 2
            if st is None:
                s0 = jnp.zeros((b, heads_c, dh_c, dh_c), F32)
                shift0 = jnp.zeros((b, d), F32)
            else:
                s0, shift0 = st["c_S"][o], st["c_shift"][o]
            (xr, xw, xk, xv, xa, xg), shift = _norm_mix(x, shift0, P["norm1_g"][l], sc1, sh1, P["c_mu"][o])
            flat = lambda z: z.reshape(m, d)
            if emit:
                r, wq["r", o] = _matmul(flat(xr), P["c_wr"], layer=o, emit=True)
                k, wq["k", o] = _matmul(flat(xk), P["c_wk"], layer=o, emit=True)
                v, wq["v", o] = _matmul(flat(xv), P["c_wv"], layer=o, emit=True)
            else:
                r, k, v = (_matmul(flat(z), wq[name, o]) for z, name in ((xr, "r"), (xk, "k"), (xv, "v")))
            lw = _lora(flat(xw), P["c_w1"][o], P["c_w2"][o], mid="tanh", epilogue="log_decay", bias=P["c_w0"][o])
            a = _lora(flat(xa), P["c_a1"][o], P["c_a2"][o], mid="none", epilogue="sigmoid", bias=P["c_a0"][o])
            g = _lora(flat(xg), P["c_g1"][o], P["c_g2"][o], mid="sigmoid", epilogue="none")
            if v_first is None:
                v_first = v
            else:
                v = _lora(flat(xv), P["c_v1"][o - 1], P["c_v2"][o - 1], mid="none", epilogue="vmix",
                          bias=P["c_v0"][o - 1], extra=(v, v_first))
            y, S = _rwkv(r, lw, k, v, a, g, P["c_k_k"][o], P["c_k_a"][o], P["c_r_k"][o], P["c_lnx_g"][o],
                         P["c_lnx_b"][o], s0, batch=b, seq=t)
            c_S.append(S)
            c_sh.append(shift)
            x = _matmul_resid([y], P["c_wo"], x.reshape(m, d), gate_rows(gt1.reshape(b, d)), layer=o)
        x = x.reshape(b, t, d)
        h = _norm_mod(x, P["norm2_g"][l], sc2, sh2).reshape(m, d)
        if emit:
            ada_next = (c_all, ada_w, ada_b, l + 1) if l + 1 < depth else None
            outs = _matmul_swiglu(h, P["ffn_w1"], P["ffn_w3"], layer=l, emit=True, ada=ada_next, w2=P["ffn_w2"])
            hid, wq["w1", l], wq["w3", l] = outs[:3]
            if ada_next is not None:
                shared["mod", l + 1] = outs[3]
            wq["w2", l] = outs[-1]
        else:
            hid = _matmul_swiglu(h, wq["w1", l], wq["w3", l])
        x = _matmul_resid([hid], wq["w2", l], x.reshape(m, d), gate_rows(gt2.reshape(b, d)),
                          tm_pref=1024, tn_pref=512).reshape(b, t, d)
    y = _final_norm(x, P["final_g"])
    return (y, jnp.stack(a_C), jnp.stack(a_n), jnp.stack(a_m), jnp.stack(b_k), jnp.stack(b_v),
            jnp.stack(c_S), jnp.stack(c_sh))


def kernel(x_prompt, x_sample, c_prompt, c_sample, state_a_C, state_a_n, state_a_m, cache_b_k, cache_b_v, state_c_S, state_c_shift, ada_w, ada_b, norm1_g, norm2_g, final_g, ab_w_in, ab_b_if, a_norm_g, b_rel_bias, ab_w_out, c_mu, c_wr, c_wk, c_wv, c_wo, c_w0, c_w1, c_w2, c_a0, c_a1, c_a2, c_v0, c_v1, c_v2, c_g1, c_g2, c_k_k, c_k_a, c_r_k, c_lnx_g, c_lnx_b, ffn_w1, ffn_w3, ffn_w2):
    n_ab, dec_b, heads_a, dk, dv = state_a_C.shape
    _, _, width, heads_b, dh_b = cache_b_k.shape
    w_a = heads_a * dk
    gate_lo = 2 * w_a + 2 * heads_a * dv
    gate_hi = gate_lo + 2 * heads_a
    assert gate_lo == 4 * w_a
    ab_wt = jnp.swapaxes(ab_w_in, 1, 2)
    P = dict(ada_w=ada_w, norm1_g=norm1_g, norm2_g=norm2_g, final_g=final_g,
             ab_wt=ab_wt, ab_wt_b=ab_wt[:, gate_hi:, :],
             ab_b_if=ab_b_if, a_norm_g=a_norm_g,
             bias_table=_rel_bias_table(b_rel_bias),
             ab_w_out=ab_w_out, c_mu=c_mu, c_wr=c_wr, c_wk=c_wk, c_wv=c_wv, c_wo=c_wo,
             c_w0=c_w0, c_w1=c_w1, c_w2=c_w2, c_a0=c_a0, c_a1=c_a1, c_a2=c_a2, c_v0=c_v0, c_v1=c_v1, c_v2=c_v2,
             c_g1=c_g1, c_g2=c_g2, c_k_k=c_k_k, c_k_a=c_k_a, c_r_k=c_r_k, c_lnx_g=c_lnx_g, c_lnx_b=c_lnx_b,
             ffn_w1=ffn_w1, ffn_w3=ffn_w3, ffn_w2=ffn_w2,
             a_shape=state_a_C.shape[1:], b_shape=(heads_b, dh_b), c_shape=state_c_S.shape[1:])
    st = dict(a_C=state_a_C, a_n=state_a_n, a_m=state_a_m,
              b_k=cache_b_k.reshape(n_ab, dec_b, width * heads_b, dh_b),
              b_v=cache_b_v.reshape(n_ab, dec_b, width * heads_b, dh_b),
              c_S=state_c_S, c_shift=state_c_shift)

    n_p, n_s = c_prompt.shape[0], c_sample.shape[0]
    rows = -(-(n_p + n_s) // 16) * 16
    c_all = jnp.concatenate([c_prompt, c_sample, jnp.zeros((rows - n_p - n_s, c_prompt.shape[1]), F32)], axis=0)
    ada = (c_all, ada_w, ada_b.reshape(ada_b.shape[0], 1, ada_b.shape[1]))

    shared = {}
    outs_p = _trunk(x_prompt, slice(0, n_p), ada, P, None, shared)
    outs_s = _trunk(x_sample, slice(n_p, n_p + n_s), ada, P, st, shared)
    return (outs_p[0], outs_s[0]) + tuple(outs_p[1:]) + tuple(outs_s[1:])
```

```python
import functools

import numpy as np
import jax
import jax.numpy as jnp
from jax import lax
from jax.experimental import pallas as pl
from jax.experimental.pallas import tpu as pltpu

F32 = jnp.float32
BF16 = jnp.bfloat16

CHUNK = 64
N_PREV_CHUNKS = 8
BAND_PAST = N_PREV_CHUNKS * CHUNK
PAST_LEN = 2048
REL_MAX = 2 * CHUNK
GATE_CAP = 15.0
EPS = 1e-6
GN_EPS = 64e-5

V7X_VMEM_BYTES = 64 * 1024 * 1024
VMEM_LIMIT = V7X_VMEM_BYTES - 4 * 1024 * 1024
LANES = 128
MATMUL_ROW_SUBTILE = 1024


def _params(*sem):
    return pltpu.CompilerParams(dimension_semantics=sem, vmem_limit_bytes=VMEM_LIMIT)


def _tile(n, pref, mult):
    if n <= pref:
        return n
    t = (pref // mult) * mult
    while t >= mult:
        if n % t == 0:
            return t
        t -= mult
    return n


def _silu(x):
    return x * jax.nn.sigmoid(x)


def _resident(block_shape, index_map):
    return pl.BlockSpec(block_shape, index_map, pipeline_mode=pl.Buffered(1))


ADA_TILE = 256


def _ada_tile(c_ref, w_ref, b_ref):
    cs = _silu(c_ref[...]).astype(BF16)
    return jnp.dot(cs, w_ref[...].astype(BF16), preferred_element_type=F32) + b_ref[...]


def _ada_kernel(c_ref, w_ref, b_ref, o_ref):
    o_ref[...] = _ada_tile(c_ref, w_ref, b_ref)


def _ada_mod(c_all, ada_w, ada_b, layer):
    _, d, n = ada_w.shape
    rows = c_all.shape[0]
    tn = _tile(n, 2 * ADA_TILE, LANES)
    return pl.pallas_call(
        _ada_kernel,
        grid=(n // tn,),
        in_specs=[
            pl.BlockSpec((rows, d), lambda j: (0, 0)),
            pl.BlockSpec((None, d, tn), lambda j: (layer, 0, j)),
            pl.BlockSpec((None, 1, tn), lambda j: (layer, 0, j)),
        ],
        out_specs=pl.BlockSpec((rows, tn), lambda j: (0, j)),
        out_shape=jax.ShapeDtypeStruct((rows, n), F32),
        compiler_params=_params("arbitrary"),
        name="ada_mod",
    )(c_all, ada_w, ada_b)


def _norm_mod_value(x, g, sc, sh):
    y = x * lax.rsqrt(jnp.mean(x * x, axis=-1, keepdims=True) + EPS)
    return (y * g) * (1.0 + sc) + sh


def _norm_mod_kernel(x_ref, g_ref, sc_ref, sh_ref, o_ref):
    o_ref[0] = _norm_mod_value(x_ref[0], g_ref[...], sc_ref[0], sh_ref[0]).astype(o_ref.dtype)


def _norm_mod(x, g, sc, sh, out_dtype=BF16):
    b, t, d = x.shape
    tt = _tile(t, 256, 16)
    row = pl.BlockSpec((1, tt, d), lambda bi, i: (bi, i, 0))
    per_batch = pl.BlockSpec((1, 1, d), lambda bi, i: (bi, 0, 0))
    return pl.pallas_call(
        _norm_mod_kernel,
        grid=(b, t // tt),
        in_specs=[row, pl.BlockSpec((1, d), lambda bi, i: (0, 0)), per_batch, per_batch],
        out_specs=row,
        out_shape=jax.ShapeDtypeStruct((b, t, d), out_dtype),
        compiler_params=_params("arbitrary", "arbitrary"),
        name="norm_mod",
    )(x, g.reshape(1, d), sc, sh)


def _final_norm_kernel(x_ref, g_ref, o_ref):
    x = x_ref[0]
    o_ref[0] = (x * lax.rsqrt(jnp.mean(x * x, axis=-1, keepdims=True) + EPS)) * g_ref[...]


def _final_norm(x, g):
    b, t, d = x.shape
    tt = _tile(t, 256, 8)
    row = pl.BlockSpec((1, tt, d), lambda bi, i: (bi, i, 0))
    return pl.pallas_call(
        _final_norm_kernel,
        grid=(b, t // tt),
        in_specs=[row, pl.BlockSpec((1, d), lambda bi, i: (0, 0))],
        out_specs=row,
        out_shape=jax.ShapeDtypeStruct((b, t, d), F32),
        compiler_params=_params("arbitrary", "arbitrary"),
        name="final_norm",
    )(x, g.reshape(1, d))


def _norm_mix_kernel(x_ref, xp_ref, s0_ref, g_ref, sc_ref, sh_ref, mu_ref, *out_refs):
    i = pl.program_id(1)
    mix_refs, last_ref = out_refs[:6], out_refs[6]
    tt, d = x_ref.shape[1], x_ref.shape[2]
    x, xp = x_ref[0], xp_ref[0]
    inv = lax.rsqrt(jnp.mean(x * x, axis=-1, keepdims=True) + EPS)
    inv_p = lax.rsqrt(jnp.mean(xp * xp, axis=-1, keepdims=True) + EPS)
    cw = LANES if d % LANES == 0 else d
    row_id = lax.broadcasted_iota(jnp.int32, (tt, cw), 0)
    for c in range(d // cw):
        cols = slice(c * cw, (c + 1) * cw)
        g, sc, sh = g_ref[:, cols], sc_ref[0, :, cols], sh_ref[0, :, cols]
        h = ((x_ref[0, :, cols] * inv) * g) * (1.0 + sc) + sh
        hp = ((xp_ref[0, :, cols] * inv_p) * g) * (1.0 + sc) + sh
        prev_row = jnp.where(i == 0, s0_ref[0, :, cols], hp[7:8, :])
        shifted = jnp.where(row_id == 0, prev_row, pltpu.roll(h, 1, axis=0))
        xx = shifted - h
        for j in range(6):
            mix_refs[j][0, :, cols] = (h + xx * mu_ref[j:j + 1, cols]).astype(BF16)
        last_ref[0, :, cols] = h[tt - 8:, :]


def _norm_mix(x, shift0, g, sc, sh, mu):
    b, t, d = x.shape
    tt = _tile(t, 256, 16)
    row = pl.BlockSpec((1, tt, d), lambda bi, i: (bi, i, 0))
    prev8 = pl.BlockSpec((1, 8, d), lambda bi, i: (bi, jnp.maximum(i * (tt // 8) - 1, 0), 0))
    per_batch = pl.BlockSpec((1, 1, d), lambda bi, i: (bi, 0, 0))
    outs = pl.pallas_call(
        _norm_mix_kernel,
        grid=(b, t // tt),
        in_specs=[row, prev8, per_batch, pl.BlockSpec((1, d), lambda bi, i: (0, 0)), per_batch, per_batch,
                  pl.BlockSpec((6, d), lambda bi, i: (0, 0))],
        out_specs=[row] * 6 + [pl.BlockSpec((1, 8, d), lambda bi, i: (bi, 0, 0))],
        out_shape=[jax.ShapeDtypeStruct((b, t, d), BF16)] * 6 + [jax.ShapeDtypeStruct((b, 8, d), F32)],
        compiler_params=_params("arbitrary", "arbitrary"),
        name="norm_mix",
    )(x, x, shift0.reshape(b, 1, d), g.reshape(1, d), sc, sh, mu)
    return outs[:6], outs[6][:, 7, :]


def _mm_rows(tm):
    sub = MATMUL_ROW_SUBTILE if tm % MATMUL_ROW_SUBTILE == 0 else tm
    return sub, tm // sub


def _row_slice(m, sub):
    return pl.ds(m * sub, sub) if isinstance(m, int) else pl.ds(pl.multiple_of(m * sub, sub), sub)


def _for_row_subtiles(n_sub, body):
    if n_sub <= 2:
        for m in range(n_sub):
            body(m, 0)
    else:
        lax.fori_loop(0, n_sub, body, 0)


def _emit_bf16_weights(wb_refs, wq_refs):
    @pl.when(pl.program_id(0) == 0)
    def _():
        for wb_ref, wq_ref in zip(wb_refs, wq_refs):
            wq_ref[...] = wb_ref[...]


def _mm_kernel(a_ref, w_ref, o_ref, *rest, tm, w_transposed, emit, bf16_copy):
    rest = list(rest)
    ob_ref = rest.pop(0) if bf16_copy else None
    if w_ref.dtype == BF16:
        wb_ref = w_ref
    else:
        wb_ref = rest[-1]
        wb_ref[...] = w_ref[...].astype(BF16)
    sub, n_sub = _mm_rows(tm)

    def body(m, carry):
        rows = _row_slice(m, sub)
        if w_transposed:
            acc = lax.dot_general(a_ref[rows, :], wb_ref[...], (((1,), (1,)), ((), ())),
                                  preferred_element_type=F32)
        else:
            acc = jnp.dot(a_ref[rows, :], wb_ref[...], preferred_element_type=F32)
        o_ref[rows, :] = acc
        if bf16_copy:
            ob_ref[rows, :] = acc.astype(BF16)
        return carry

    _for_row_subtiles(n_sub, body)
    if emit:
        _emit_bf16_weights([wb_ref], [rest[0]])


def _mm_resid_kernel(*refs, tm, parts):
    a_refs, w_refs = refs[:parts], refs[parts:2 * parts]
    r_ref, g_ref, o_ref = refs[2 * parts:2 * parts + 3]
    wb_refs = refs[2 * parts + 3:]
    if wb_refs:
        for w_ref, wb_ref in zip(w_refs, wb_refs):
            wb_ref[...] = w_ref[...].astype(BF16)
    else:
        wb_refs = w_refs
    sub, n_sub = _mm_rows(tm)
    per_row_gate = g_ref.shape[0] != 1

    def body(m, carry):
        rows = _row_slice(m, sub)
        acc = jnp.dot(a_refs[0][rows, :], wb_refs[0][...], preferred_element_type=F32)
        for a_ref, wb_ref in zip(a_refs[1:], wb_refs[1:]):
            acc = acc + jnp.dot(a_ref[rows, :], wb_ref[...], preferred_element_type=F32)
        gate = g_ref[rows, :] if per_row_gate else g_ref[...]
        o_ref[rows, :] = r_ref[rows, :] + gate * acc
        return carry

    _for_row_subtiles(n_sub, body)


def _mm_swiglu_kernel(a_ref, w1_ref, w3_ref, *rest, tm, emit, ada_tiles, cast_w2):
    rest = list(rest)
    if ada_tiles:
        c_ref, aw_ref, ab_ref = rest[:3]
        del rest[:3]
    if cast_w2:
        w2_ref = rest.pop(0)
    o_ref = rest.pop(0)
    if emit:
        wq_refs = rest[:2]
        del rest[:2]
        w1b_ref, w3b_ref = rest[-2:]
        w1b_ref[...] = w1_ref[...].astype(BF16)
        w3b_ref[...] = w3_ref[...].astype(BF16)
    else:
        w1b_ref, w3b_ref = w1_ref, w3_ref
    sub, n_sub = _mm_rows(tm)

    def body(m, carry):
        rows = _row_slice(m, sub)
        a = a_ref[rows, :]
        u = jnp.dot(a, w1b_ref[...], preferred_element_type=F32)
        v = jnp.dot(a, w3b_ref[...], preferred_element_type=F32)
        o_ref[rows, :] = (_silu(u) * v).astype(o_ref.dtype)
        return carry

    _for_row_subtiles(n_sub, body)
    if ada_tiles:
        mod_ref = rest.pop(0)
    if cast_w2:
        rest.pop(0)[...] = w2_ref[...].astype(BF16)
    if emit:
        _emit_bf16_weights([w1b_ref, w3b_ref], wq_refs)
    if ada_tiles:
        step = pl.program_id(0) * pl.num_programs(1) + pl.program_id(1)

        @pl.when(step < ada_tiles)
        def _():
            mod_ref[...] = _ada_tile(c_ref, aw_ref, ab_ref)


def _parked(n_tiles):
    return lambda i, j: jnp.where(i == 0, j, n_tiles - 1)


def _weight_spec(w, layer, k, tn):
    if w.ndim == 3:
        return pl.BlockSpec((None, k, tn), lambda i, j: (layer, 0, j))
    return pl.BlockSpec((k, tn), lambda i, j: (0, j))


def _matmul(a, w, *, layer=0, emit=False, tm_pref=2048, tn_pref=512):
    m, k = a.shape
    n = w.shape[-1]
    tm = _tile(m, tm_pref, MATMUL_ROW_SUBTILE)
    tn = _tile(n, tn_pref, LANES)
    park = _parked(n // tn)
    out_specs = [pl.BlockSpec((tm, tn), lambda i, j: (i, j))]
    out_shape = [jax.ShapeDtypeStruct((m, n), F32)]
    if emit:
        out_specs.append(pl.BlockSpec((k, tn), lambda i, j: (0, park(i, j))))
        out_shape.append(jax.ShapeDtypeStruct((k, n), BF16))
    outs = pl.pallas_call(
        functools.partial(_mm_kernel, tm=tm, w_transposed=False, emit=emit, bf16_copy=False),
        grid=(m // tm, n // tn),
        in_specs=[_resident((tm, k), lambda i, j: (i, 0)), _weight_spec(w, layer, k, tn)],
        out_specs=out_specs,
        out_shape=out_shape,
        scratch_shapes=[pltpu.VMEM((k, tn), BF16)] if w.dtype != BF16 else [],
        compiler_params=_params("arbitrary", "arbitrary"),
        name="matmul",
    )(a, w)
    return tuple(outs) if emit else outs[0]


def _matmul_wt(a, wt, *, layer, row0, n, emit=False, bf16_copy=False, tm_pref=2048, tn_pref=512):
    m, k = a.shape
    tm = _tile(m, tm_pref, MATMUL_ROW_SUBTILE)
    tn = _tile(n, tn_pref, LANES)
    assert row0 % tn == 0 and n % tn == 0
    park = _parked(n // tn)
    out_specs = [pl.BlockSpec((tm, tn), lambda i, j: (i, j))]
    out_shape = [jax.ShapeDtypeStruct((m, n), F32)]
    if bf16_copy:
        out_specs.append(pl.BlockSpec((tm, tn), lambda i, j: (i, j)))
        out_shape.append(jax.ShapeDtypeStruct((m, n), BF16))
    if emit:
        out_specs.append(pl.BlockSpec((tn, k), lambda i, j: (park(i, j), 0)))
        out_shape.append(jax.ShapeDtypeStruct((n, k), BF16))
    outs = pl.pallas_call(
        functools.partial(_mm_kernel, tm=tm, w_transposed=True, emit=emit, bf16_copy=bf16_copy),
        grid=(m // tm, n // tn),
        in_specs=[_resident((tm, k), lambda i, j: (i, 0)),
                  pl.BlockSpec((None, tn, k), lambda i, j: (layer, row0 // tn + j, 0))],
        out_specs=out_specs,
        out_shape=out_shape,
        scratch_shapes=[pltpu.VMEM((tn, k), BF16)] if wt.dtype != BF16 else [],
        compiler_params=_params("arbitrary", "arbitrary"),
        name="matmul_wt",
    )(a, wt)
    return tuple(outs) if len(outs) > 1 else outs[0]


def _matmul_resid(a_parts, w, res, gate, *, layer=0, tm_pref=2048, tn_pref=512):
    parts = len(a_parts)
    m = a_parts[0].shape[0]
    ks = [a.shape[1] for a in a_parts]
    offs = [sum(ks[:p]) for p in range(parts)]
    n = w.shape[-1]
    assert w.shape[-2] == sum(ks) and all(off % k == 0 for off, k in zip(offs, ks))
    tm = _tile(m, tm_pref, MATMUL_ROW_SUBTILE)
    tn = _tile(n, tn_pref, LANES)
    if gate.shape[0] == 1:
        gate_spec = pl.BlockSpec((1, tn), lambda i, j: (0, j))
    else:
        gate_spec = pl.BlockSpec((tm, tn), lambda i, j: (i, j))
    if w.ndim == 3:
        w_specs = [pl.BlockSpec((None, k, tn), functools.partial(lambda i, j, blk: (layer, blk, j), blk=off // k))
                   for off, k in zip(offs, ks)]
    else:
        w_specs = [pl.BlockSpec((k, tn), functools.partial(lambda i, j, blk: (blk, j), blk=off // k))
                   for off, k in zip(offs, ks)]
    staged = w.dtype != BF16
    return pl.pallas_call(
        functools.partial(_mm_resid_kernel, tm=tm, parts=parts),
        grid=(m // tm, n // tn),
        in_specs=[_resident((tm, k), lambda i, j: (i, 0)) for k in ks] + w_specs
                 + [pl.BlockSpec((tm, tn), lambda i, j: (i, j)), gate_spec],
        out_specs=pl.BlockSpec((tm, tn), lambda i, j: (i, j)),
        out_shape=jax.ShapeDtypeStruct((m, n), F32),
        scratch_shapes=[pltpu.VMEM((k, tn), BF16) for k in ks] if staged else [],
        compiler_params=_params("arbitrary", "arbitrary"),
        name="matmul_resid",
    )(*a_parts, *([w] * parts), res, gate)


def _matmul_swiglu(a, w1, w3, *, layer=0, emit=False, ada=None, w2=None, tm_pref=2048, tn_pref=256):
    m, k = a.shape
    n = w1.shape[-1]
    assert emit == (w1.dtype != BF16)
    tm = _tile(m, tm_pref, MATMUL_ROW_SUBTILE)
    tn = _tile(n, tn_pref, LANES)
    n_i, n_j = m // tm, n // tn
    steps = n_i * n_j
    if ada is not None:
        c_all, ada_w, ada_b, next_layer = ada
        rows, d = c_all.shape
        n_mod = ada_w.shape[-1]
        if n_mod % ADA_TILE != 0 or n_mod // ADA_TILE > steps:
            outs = _matmul_swiglu(a, w1, w3, layer=layer, emit=emit, w2=w2, tm_pref=tm_pref, tn_pref=tn_pref)
            outs = list(outs) if isinstance(outs, tuple) else [outs]
            outs.insert(3 if emit else 1, _ada_mod(c_all, ada_w, ada_b, next_layer))
            return tuple(outs)
    if w2 is not None and (w2.shape[1] % steps != 0 or (w2.shape[1] // steps) % 16 != 0):
        outs = _matmul_swiglu(a, w1, w3, layer=layer, emit=emit, ada=ada, tm_pref=tm_pref, tn_pref=tn_pref)
        outs = outs if isinstance(outs, tuple) else (outs,)
        return outs + (w2[layer].astype(BF16),)
    step_of = lambda i, j: i * n_j + j
    in_specs = [_resident((tm, k), lambda i, j: (i, 0)), _weight_spec(w1, layer, k, tn),
                _weight_spec(w3, layer, k, tn)]
    args = [a, w1, w3]
    out_specs = [pl.BlockSpec((tm, tn), lambda i, j: (i, j))]
    out_shape = [jax.ShapeDtypeStruct((m, n), BF16)]
    if emit:
        park = _parked(n_j)
        out_specs += [pl.BlockSpec((k, tn), lambda i, j: (0, park(i, j)))] * 2
        out_shape += [jax.ShapeDtypeStruct((k, n), BF16)] * 2
    ada_tiles = 0
    if ada is not None:
        ada_tiles = n_mod // ADA_TILE
        tile_of = lambda i, j: jnp.minimum(step_of(i, j), ada_tiles - 1)
        in_specs += [pl.BlockSpec((rows, d), lambda i, j: (0, 0)),
                     pl.BlockSpec((None, d, ADA_TILE), lambda i, j: (next_layer, 0, tile_of(i, j))),
                     pl.BlockSpec((None, 1, ADA_TILE), lambda i, j: (next_layer, 0, tile_of(i, j)))]
        args += [c_all, ada_w, ada_b]
        out_specs.append(pl.BlockSpec((rows, ADA_TILE), lambda i, j: (0, tile_of(i, j))))
        out_shape.append(jax.ShapeDtypeStruct((rows, n_mod), F32))
    if w2 is not None:
        k2, n2 = w2.shape[1:]
        rows2 = k2 // steps
        in_specs.append(pl.BlockSpec((None, rows2, n2), lambda i, j: (layer, step_of(i, j), 0)))
        args.append(w2)
        out_specs.append(pl.BlockSpec((rows2, n2), lambda i, j: (step_of(i, j), 0)))
        out_shape.append(jax.ShapeDtypeStruct((k2, n2), BF16))
    outs = pl.pallas_call(
        functools.partial(_mm_swiglu_kernel, tm=tm, emit=emit, ada_tiles=ada_tiles, cast_w2=w2 is not None),
        grid=(n_i, n_j),
        in_specs=in_specs,
        out_specs=out_specs,
        out_shape=out_shape,
        scratch_shapes=[pltpu.VMEM((k, tn), BF16)] * 2 if emit else [],
        compiler_params=_params("arbitrary", "arbitrary"),
        name="matmul_swiglu",
    )(*args)
    return tuple(outs) if len(outs) > 1 else outs[0]


def _lora_kernel(a_ref, w1_ref, w2_ref, *rest, mid, epilogue):
    o_ref = rest[-1]
    z = jnp.dot(a_ref[...], w1_ref[...], preferred_element_type=F32)
    if mid == "tanh":
        z = jnp.tanh(z)
    elif mid == "sigmoid":
        z = jax.nn.sigmoid(z)
    z = jnp.dot(z.astype(BF16), w2_ref[...], preferred_element_type=F32)
    if epilogue == "log_decay":
        o_ref[...] = (-float(np.exp(-0.5))) * jax.nn.sigmoid(rest[0][...] + z)
    elif epilogue == "sigmoid":
        o_ref[...] = jax.nn.sigmoid(rest[0][...] + z)
    elif epilogue == "vmix":
        v, vf = rest[1][...], rest[2][...]
        o_ref[...] = v + (vf - v) * jax.nn.sigmoid(rest[0][...] + z)
    else:
        o_ref[...] = z


def _lora(a, w1, w2, *, mid, epilogue, bias=None, extra=()):
    m, d = a.shape
    r = w1.shape[1]
    rp = -(-r // LANES) * LANES
    w1p = jnp.pad(w1, ((0, 0), (0, rp - r))).astype(BF16)
    w2p = jnp.pad(w2, ((0, rp - r), (0, 0))).astype(BF16)
    n = w2.shape[1]
    tm = _tile(m, 256, 16)
    row = pl.BlockSpec((tm, n), lambda i: (i, 0))
    in_specs = [pl.BlockSpec((tm, d), lambda i: (i, 0)), _resident((d, rp), lambda i: (0, 0)),
                _resident((rp, n), lambda i: (0, 0))]
    args = [a, w1p, w2p]
    if bias is not None:
        in_specs.append(pl.BlockSpec((1, n), lambda i: (0, 0)))
        args.append(bias.reshape(1, n))
    for e in extra:
        in_specs.append(row)
        args.append(e)
    return pl.pallas_call(
        functools.partial(_lora_kernel, mid=mid, epilogue=epilogue),
        grid=(m // tm,),
        in_specs=in_specs,
        out_specs=row,
        out_shape=jax.ShapeDtypeStruct((m, n), F32),
        compiler_params=_params("arbitrary"),
        name="lora_" + epilogue,
    )(*args)


def _dot_nt(a, b):
    return lax.dot_general(a, b, (((1,), (1,)), ((), ())), preferred_element_type=F32)


def _dot_tn(a, b):
    return lax.dot_general(a, b, (((0,), (0,)), ((), ())), preferred_element_type=F32)


MLSTM_CHUNK = 256


def _soft_cap(x):
    return GATE_CAP * jnp.tanh(x / GATE_CAP)


def _mlstm_kernel(q_ref, k_ref, v_ref, o_ref, gc_ref, gr_ref, bc_ref, br_ref, ng_ref, c0_ref, n0_ref, m0_ref,
                  h_ref, c_ref, n_ref, m_ref, *, heads, dk, dv, length):
    c_idx = pl.program_id(1)

    @pl.when(c_idx == 0)
    def _():
        c_ref[...] = c0_ref[...]
        n_ref[...] = n0_ref[...]
        m_ref[...] = m0_ref[...]

    gcol = gc_ref[...] + bc_ref[...]
    grow = gr_ref[...] + br_ref[...]
    li_col, lf_col = _soft_cap(gcol[:, :heads]), jax.nn.log_sigmoid(_soft_cap(gcol[:, heads:]))
    li_row, lf_row = _soft_cap(grow[:heads, :]), jax.nn.log_sigmoid(_soft_cap(grow[heads:, :]))
    r_id = lax.broadcasted_iota(jnp.int32, (length, length), 0)
    c_id = lax.broadcasted_iota(jnp.int32, (length, length), 1)
    tril = c_id <= r_id
    scale = dk ** -0.5

    hs = range(heads)
    gates = []
    for h in hs:
        lf_r, li_r = lf_row[h:h + 1, :], li_row[h:h + 1, :]
        lf_c, li_c = lf_col[:, h:h + 1], li_col[:, h:h + 1]
        b_col = jnp.sum(jnp.where(tril, lf_r, 0.0), axis=1, keepdims=True)
        b_row = jnp.sum(jnp.where(r_id <= c_id, lf_c, 0.0), axis=0, keepdims=True)
        b_last = b_col[length - 1:length, :]
        m_prev = m_ref[0, h:h + 1, :]
        log_d = jnp.where(tril, b_col - b_row + li_r, -jnp.inf)
        inter = b_col + m_prev
        m_t = jnp.maximum(inter, jnp.max(log_d, axis=1, keepdims=True))
        m_new = m_t[length - 1:length, :]
        gates.append(dict(p=jnp.exp(log_d - m_t), g=jnp.exp(inter - m_t), m_t=m_t, m_new=m_new,
                          wk_col=jnp.exp(b_last - b_col + li_c - m_new),
                          decay=jnp.exp(b_last + m_prev - m_new)))

    q = [q_ref[:, h * dk:(h + 1) * dk] for h in hs]
    k = [k_ref[:, h * dk:(h + 1) * dk] * scale for h in hs]
    kw = [k[h] * gates[h]["wk_col"] for h in hs]
    qb = [x.astype(BF16) for x in q]
    kb = [x.astype(BF16) for x in k]
    vb = [v_ref[:, h * dv:(h + 1) * dv].astype(BF16) for h in hs]
    c_state = [c_ref[0, h] for h in hs]
    n_state = [n_ref[0, h:h + 1, :] for h in hs]
    s_qk = [_dot_nt(qb[h], kb[h]) for h in hs]
    q_c = [jnp.dot(qb[h], c_state[h].astype(BF16), preferred_element_type=F32) for h in hs]
    kv = [_dot_tn(kw[h].astype(BF16), vb[h]) for h in hs]
    wqk = [gates[h]["p"] * s_qk[h] for h in hs]
    num = [jnp.dot(wqk[h].astype(BF16), vb[h], preferred_element_type=F32) + gates[h]["g"] * q_c[h] for h in hs]

    for h in hs:
        g = gates[h]
        den = jnp.sum(wqk[h], axis=1, keepdims=True) + g["g"] * jnp.sum(q[h] * n_state[h], axis=1, keepdims=True)
        hh = num[h] / jnp.maximum(jnp.abs(den), jnp.exp(-g["m_t"]))
        hh = hh * lax.rsqrt(jnp.mean(hh * hh, axis=1, keepdims=True) + EPS)
        hh = hh * ng_ref[:, h * dv:(h + 1) * dv]
        h_ref[:, h * dv:(h + 1) * dv] = (jax.nn.sigmoid(o_ref[:, h * dv:(h + 1) * dv]) * hh).astype(h_ref.dtype)
        c_ref[0, h] = g["decay"] * c_state[h] + kv[h]
        n_ref[0, h:h + 1, :] = g["decay"] * n_state[h] + jnp.sum(kw[h], axis=0, keepdims=True)
        m_ref[0, h:h + 1, :] = g["m_new"]


def _mlstm(proj, gif, b_if, a_norm_g, c0, n0, m0, *, batch, seq):
    _, heads, dk, dv = c0.shape
    length = _tile(seq, MLSTM_CHUNK, CHUNK)
    nc = seq // length
    wq, wv = heads * dk, heads * dv
    assert wq == wv
    gcol = gif
    grow = gif.reshape(batch, nc, length, 2 * heads).transpose(0, 1, 3, 2)
    row = lambda col: pl.BlockSpec((length, wq), lambda b, c: (b * nc + c, col))
    state4 = pl.BlockSpec((1, heads, dk, dv), lambda b, c: (b, 0, 0, 0))
    state3 = pl.BlockSpec((1, heads, dk), lambda b, c: (b, 0, 0))
    state_m = pl.BlockSpec((1, heads, 1), lambda b, c: (b, 0, 0))
    h, c_out, n_out, m_out = pl.pallas_call(
        functools.partial(_mlstm_kernel, heads=heads, dk=dk, dv=dv, length=length),
        grid=(batch, nc),
        in_specs=[row(0), row(1), row(2), row(3),
                  pl.BlockSpec((length, 2 * heads), lambda b, c: (b * nc + c, 0)),
                  pl.BlockSpec((None, None, 2 * heads, length), lambda b, c: (b, c, 0, 0)),
                  pl.BlockSpec((1, 2 * heads), lambda b, c: (0, 0)),
                  pl.BlockSpec((2 * heads, 1), lambda b, c: (0, 0)),
                  pl.BlockSpec((1, wv), lambda b, c: (0, 0)),
                  state4, state3, state_m],
        out_specs=[pl.BlockSpec((length, wv), lambda b, c: (b * nc + c, 0)), state4, state3, state_m],
        out_shape=[jax.ShapeDtypeStruct((batch * seq, wv), BF16),
                   jax.ShapeDtypeStruct((batch, heads, dk, dv), F32),
                   jax.ShapeDtypeStruct((batch, heads, dk), F32),
                   jax.ShapeDtypeStruct((batch, heads, 1), F32)],
        compiler_params=_params("arbitrary", "arbitrary"),
        name="mlstm",
    )(proj, proj, proj, proj, gcol, grow, b_if.reshape(1, 2 * heads), b_if.reshape(2 * heads, 1),
      a_norm_g.reshape(1, wv), c0, n0, m0.reshape(batch, heads, 1))
    return h, c_out, n_out, m_out.reshape(batch, heads)


def _rel_bias_table(rel_bias):
    n_e, heads, rel_size = rel_bias.shape
    band = BAND_PAST + CHUNK
    i = np.arange(CHUNK)[:, None]
    j = np.arange(band)[None, :]
    rel = (np.clip(i - j + BAND_PAST, -(CHUNK - 1), REL_MAX) + (CHUNK - 1)).reshape(-1)
    onehot = (jnp.asarray(rel, jnp.int32)[None, :] == jnp.arange(rel_size, dtype=jnp.int32)[:, None]).astype(F32)
    table = jnp.dot(rel_bias.reshape(n_e * heads, rel_size), onehot, precision=lax.Precision.HIGHEST)
    return table.reshape(n_e, heads, CHUNK, band)


def _band_chunks(qs, ks, vs, biases, scale):
    n = range(len(qs))
    s = [_dot_nt(qs[i], ks[i]) * scale + biases[i] for i in n]
    p = [jnp.exp(s[i] - jnp.max(s[i], axis=1, keepdims=True)) for i in n]
    o = [jnp.dot(p[i].astype(BF16), vs[i], preferred_element_type=F32) for i in n]
    return [o[i] / jnp.sum(p[i], axis=1, keepdims=True) for i in n]


ATTN_CHUNKS_PER_ITER = 8


def _attn_prompt_kernel(q_ref, k_ref, v_ref, bias_ref, o_ref, *, seq, scale):
    kb_ref, vb_ref = k_ref, v_ref
    nc = seq // CHUNK
    band = BAND_PAST + CHUNK
    bias = bias_ref[0]

    lead = list(range(min(N_PREV_CHUNKS, nc)))
    for c0 in range(0, len(lead), ATTN_CHUNKS_PER_ITER):
        cs = lead[c0:c0 + ATTN_CHUNKS_PER_ITER]
        widths = [(c + 1) * CHUNK for c in cs]
        outs = _band_chunks([q_ref[c * CHUNK:(c + 1) * CHUNK, :].astype(BF16) for c in cs],
                            [kb_ref[0:w, :] for w in widths], [vb_ref[0:w, :] for w in widths],
                            [bias[:, band - w:] for w in widths], scale)
        for c, o in zip(cs, outs):
            o_ref[c * CHUNK:(c + 1) * CHUNK, :] = o.astype(o_ref.dtype)

    rest = nc - N_PREV_CHUNKS
    per_iter = ATTN_CHUNKS_PER_ITER if rest % ATTN_CHUNKS_PER_ITER == 0 else 1

    def body(it, carry):
        starts = [pl.multiple_of((N_PREV_CHUNKS + it * per_iter + u) * CHUNK, CHUNK) for u in range(per_iter)]
        k_rows = [pl.ds(pl.multiple_of(s - BAND_PAST, CHUNK), band) for s in starts]
        outs = _band_chunks([q_ref[pl.ds(s, CHUNK), :].astype(BF16) for s in starts],
                            [kb_ref[r, :] for r in k_rows], [vb_ref[r, :] for r in k_rows],
                            [bias] * per_iter, scale)
        for s, o in zip(starts, outs):
            o_ref[pl.ds(s, CHUNK), :] = o.astype(o_ref.dtype)
        return carry

    if rest > 0:
        lax.fori_loop(0, rest // per_iter, body, 0)


def _attn_prompt(proj_b, bias_table, *, batch, seq, heads, dh):
    assert proj_b.dtype == BF16
    col = lambda base: pl.BlockSpec((seq, dh), lambda b, h: (b, base * heads + h))
    return pl.pallas_call(
        functools.partial(_attn_prompt_kernel, seq=seq, scale=dh ** -0.5),
        grid=(batch, heads),
        in_specs=[col(0), col(1), col(2),
                  pl.BlockSpec((1, CHUNK, BAND_PAST + CHUNK), lambda b, h: (h, 0, 0))],
        out_specs=pl.BlockSpec((seq, dh), lambda b, h: (b, h)),
        out_shape=jax.ShapeDtypeStruct((batch * seq, heads * dh), BF16),
        compiler_params=_params("arbitrary", "arbitrary"),
        name="attn_prompt",
    )(proj_b, proj_b, proj_b, bias_table)


def _attn_sample_kernel(q_ref, k_ref, v_ref, pk_ref, pv_ref, bias_ref, o_ref, *, seq, width, heads, dh, scale):
    off = BAND_PAST - width
    hs = range(heads)
    cols = [slice(h * dh, (h + 1) * dh) for h in hs]
    rows = [pl.ds(h, width, stride=heads) for h in hs]
    q = [q_ref[:, c].astype(BF16) for c in cols]
    s_past = [_dot_nt(q[h], pk_ref[rows[h], :].astype(BF16)) * scale + bias_ref[h, :seq, off:off + width] for h in hs]
    s_new = [_dot_nt(q[h], k_ref[:, cols[h]].astype(BF16)) * scale + bias_ref[h, :seq, BAND_PAST:BAND_PAST + seq]
             for h in hs]
    m = [jnp.maximum(jnp.max(s_past[h], axis=1, keepdims=True), jnp.max(s_new[h], axis=1, keepdims=True)) for h in hs]
    p_past = [jnp.exp(s_past[h] - m[h]) for h in hs]
    p_new = [jnp.exp(s_new[h] - m[h]) for h in hs]
    o = [jnp.dot(p_past[h].astype(BF16), pv_ref[rows[h], :].astype(BF16), preferred_element_type=F32)
         + jnp.dot(p_new[h].astype(BF16), v_ref[:, cols[h]].astype(BF16), preferred_element_type=F32) for h in hs]
    for h in hs:
        denom = jnp.sum(p_past[h], axis=1, keepdims=True) + jnp.sum(p_new[h], axis=1, keepdims=True)
        o_ref[:, cols[h]] = (o[h] / denom).astype(o_ref.dtype)


def _attn_sample(proj_b, past_k, past_v, layer, bias_table, *, batch, seq, heads, dh):
    width = past_k.shape[2] // heads
    wb = heads * dh
    col = lambda base: pl.BlockSpec((seq, wb), lambda b: (b, base))
    past = pl.BlockSpec((None, None, width * heads, dh), lambda b: (layer, b, 0, 0))
    return pl.pallas_call(
        functools.partial(_attn_sample_kernel, seq=seq, width=width, heads=heads, dh=dh, scale=dh ** -0.5),
        grid=(batch,),
        in_specs=[col(0), col(1), col(2), past, past,
                  pl.BlockSpec((heads, CHUNK, BAND_PAST + CHUNK), lambda b: (0, 0, 0))],
        out_specs=pl.BlockSpec((seq, wb), lambda b: (b, 0)),
        out_shape=jax.ShapeDtypeStruct((batch * seq, wb), BF16),
        compiler_params=_params("arbitrary"),
        name="attn_sample",
    )(proj_b, proj_b, proj_b, past_k, past_v, bias_table)


def _split_dot(a_exact, x):
    hi = x.astype(BF16)
    lo = (x - hi.astype(F32)).astype(BF16)
    return jnp.dot(a_exact, hi, preferred_element_type=F32) + jnp.dot(a_exact, lo, preferred_element_type=F32)


def _block_diag_rows(x, lo_mask):
    return jnp.concatenate([jnp.where(lo_mask, x, 0.0), jnp.where(lo_mask, 0.0, x)], axis=0).astype(BF16)


def _rwkv_kernel(r_ref, lw_ref, k_ref, v_ref, a_ref, g_ref, kk_ref, ka_ref, rk_ref, lg_ref, lb_ref, s0_ref,
                 o_ref, s_ref, s2_ref, *, pairs, dh, length, n_chunks):
    c_idx = pl.program_id(2)
    pw_ = 2 * dh
    ps = range(pairs)

    @pl.when(c_idx == 0)
    def _():
        for p in ps:
            s2_ref[p] = jnp.concatenate([s0_ref[0, 2 * p], s0_ref[0, 2 * p + 1]], axis=1)

    def lane_lo(shape, half):
        return lax.broadcasted_iota(jnp.int32, shape, 1) < half

    f_lo = lane_lo((length, pw_), dh)
    f_lo_s = lane_lo((dh, pw_), dh)
    t_lo = lane_lo((length, 2 * length), length)
    r_id = lax.broadcasted_iota(jnp.int32, (length, length), 0)
    c_id = lax.broadcasted_iota(jnp.int32, (length, length), 1)
    tril_bf = jnp.where(c_id <= r_id, 1.0, 0.0).astype(BF16)
    row2 = lax.broadcasted_iota(jnp.int32, (length, 2 * length), 0)
    col2 = lax.broadcasted_iota(jnp.int32, (length, 2 * length), 1) & (length - 1)
    strict2 = col2 < row2
    row4 = lax.broadcasted_iota(jnp.int32, (length, 4 * length), 0)
    col4 = lax.broadcasted_iota(jnp.int32, (length, 4 * length), 1) & (length - 1)
    incl4 = col4 <= row4
    n_double = max(int(np.ceil(np.log2(length))), 1)

    def head_sums(x, lo_mask):
        s_lo = jnp.sum(jnp.where(lo_mask, x, 0.0), axis=1, keepdims=True)
        s_hi = jnp.sum(jnp.where(lo_mask, 0.0, x), axis=1, keepdims=True)
        return jnp.where(lo_mask, s_lo, s_hi)

    r_all, lw_all, k_all, v_all, a_all = r_ref[...], lw_ref[...], k_ref[...], v_ref[...], a_ref[...]
    cum = _split_dot(tril_bf, lw_all)
    w_in = jnp.exp(cum)
    w_inv = jnp.exp(-cum)
    w_ex = jnp.exp(cum - lw_all)
    kk_all = k_all * kk_ref[...]
    k2_all = k_all * (1.0 + (a_all - 1.0) * ka_ref[...])
    rt_all = r_all * w_in
    kt_all = k2_all * w_inv
    ba_all = a_all * w_inv
    rk2_all = r_all * k2_all * rk_ref[...]
    sl = lambda x, p: x[:, p * pw_:(p + 1) * pw_]

    lhs, rhs, rhs_bd = [], [], []
    for p in ps:
        kk = sl(kk_all, p)
        kk = kk / jnp.maximum(jnp.sqrt(head_sums(kk * kk, f_lo)), 1e-12)
        at = (-kk) * sl(w_ex, p)
        bt = kk * sl(ba_all, p)
        kt = sl(kt_all, p)
        lhs.append(jnp.concatenate([at, sl(rt_all, p)], axis=0).astype(BF16))
        rhs.append(jnp.concatenate([bt, kt], axis=0).astype(BF16))
        rhs_bd.append(jnp.concatenate([_block_diag_rows(bt, f_lo), _block_diag_rows(kt, f_lo)], axis=0))
    s0 = [s2_ref[p] for p in ps]
    vf = [sl(v_all, p) for p in ps]
    v_bd = [_block_diag_rows(vf[p], f_lo) for p in ps]
    aals = [_dot_nt(lhs[p], jnp.concatenate([rhs_bd[p], _block_diag_rows(s0[p], f_lo_s)], axis=0)) for p in ps]
    aa = [z[:, :4 * length] for z in aals]
    ls = [z[:, 4 * length:] for z in aals]
    x = [ls[p][:length] + jnp.dot(jnp.where(strict2, aa[p][:length, 2 * length:], 0.0).astype(BF16), v_bd[p],
                                  preferred_element_type=F32) for p in ps]
    pw = [jnp.where(strict2, aa[p][:length, :2 * length], 0.0) for p in ps]
    for j in range(n_double):
        pwb = [q.astype(BF16) for q in pw]
        if j + 1 < n_double:
            both = [jnp.dot(pwb[p], jnp.concatenate([_block_diag_rows(x[p], f_lo), _block_diag_rows(pw[p], t_lo)],
                                                    axis=1), preferred_element_type=F32) for p in ps]
            x = [x[p] + both[p][:, :pw_] for p in ps]
            pw = [both[p][:, pw_:] for p in ps]
        else:
            x = [x[p] + jnp.dot(pwb[p], _block_diag_rows(x[p], f_lo), preferred_element_type=F32) for p in ps]
    uv_bd = [jnp.concatenate([_block_diag_rows(x[p], f_lo), v_bd[p]], axis=0) for p in ps]
    y = [ls[p][length:] + jnp.dot(jnp.where(incl4, aa[p][length:, :], 0.0).astype(BF16), uv_bd[p],
                                  preferred_element_type=F32) for p in ps]
    uv = [jnp.concatenate([x[p], vf[p]], axis=0).astype(BF16) for p in ps]
    ds = [_dot_tn(uv[p], rhs[p]) for p in ps]
    for p in ps:
        delta = jnp.where(f_lo_s, ds[p][:dh, :], ds[p][dh:, :])
        s2_ref[p] = (s0[p] + delta) * sl(w_in, p)[length - 1:length, :]

    inv_dh = 1.0 / dh
    for p in ps:
        cols = slice(p * pw_, (p + 1) * pw_)
        yc = y[p] - head_sums(y[p], f_lo) * inv_dh
        yn = yc * lax.rsqrt(head_sums(yc * yc, f_lo) * inv_dh + GN_EPS)
        yn = yn * lg_ref[:, cols] + lb_ref[:, cols]
        bonus = head_sums(sl(rk2_all, p), f_lo) * vf[p]
        o_ref[:, cols] = ((yn + bonus) * g_ref[:, cols]).astype(o_ref.dtype)

    @pl.when(c_idx == n_chunks - 1)
    def _():
        for p in ps:
            s_ref[0, 2 * p] = s2_ref[p][:, :dh]
            s_ref[0, 2 * p + 1] = s2_ref[p][:, dh:]


def _rwkv(r, lw, k, v, a, g, k_k, k_a, r_k, lnx_g, lnx_b, s0, *, batch, seq):
    _, heads, dh, _ = s0.shape
    d = heads * dh
    length = min(CHUNK, seq)
    nc = seq // length
    assert 2 * dh == LANES and length & (length - 1) == 0 and heads % 2 == 0
    hg = _tile(heads, 32 if length >= CHUNK else 16, 2)
    wg = hg * dh
    row = pl.BlockSpec((length, wg), lambda b, gi, c: (b * nc + c, gi))
    par = pl.BlockSpec((1, wg), lambda b, gi, c: (0, gi))
    state = pl.BlockSpec((1, hg, dh, dh), lambda b, gi, c: (b, gi, 0, 0))
    return pl.pallas_call(
        functools.partial(_rwkv_kernel, pairs=hg // 2, dh=dh, length=length, n_chunks=nc),
        grid=(batch, heads // hg, nc),
        in_specs=[row] * 6 + [par] * 5 + [state],
        out_specs=[row, state],
        out_shape=[jax.ShapeDtypeStruct((batch * seq, d), BF16), jax.ShapeDtypeStruct(s0.shape, F32)],
        scratch_shapes=[pltpu.VMEM((hg // 2, dh, 2 * dh), F32)],
        compiler_params=_params("arbitrary", "arbitrary", "arbitrary"),
        name="rwkv",
    )(r, lw, k, v, a, g, k_k.reshape(1, d), k_a.reshape(1, d), r_k.reshape(1, d), lnx_g.reshape(1, d),
      lnx_b.reshape(1, d), s0)


def _trunk(x, rows, ada, P, st, shared):
    emit = st is None
    c_all, ada_w, ada_b = ada
    wq = shared
    b, t, d = x.shape
    m = b * t
    depth = P["ada_w"].shape[0]
    _, heads_a, dk, dv = P["a_shape"]
    heads_b, dh_b = P["b_shape"]
    _, heads_c, dh_c, _ = P["c_shape"]
    w_a = heads_a * dk
    w_b = heads_b * dh_b
    a_C, a_n, a_m, b_k, b_v, c_S, c_sh = [], [], [], [], [], [], []
    v_first = None

    def gate_rows(gt):
        return gt if b == 1 else jnp.broadcast_to(gt[:, None, :], (b, t, d)).reshape(m, d)

    for l in range(depth):
        if emit and l == 0:
            shared["mod", 0] = _ada_mod(c_all, ada_w, ada_b, 0)
        sh1, sc1, gt1, sh2, sc2, gt2 = (z.reshape(b, 1, d) for z in jnp.split(shared["mod", l][rows], 6, axis=-1))
        if l % 2 == 0:
            e = l // 2
            h = _norm_mod(x, P["norm1_g"][l], sc1, sh1).reshape(m, d)
            if st is None:
                c0 = jnp.zeros((b, heads_a, dk, dv), F32)
                n0 = jnp.zeros((b, heads_a, dk), F32)
                m0 = jnp.zeros((b, heads_a), F32)
            else:
                c0, n0, m0 = st["a_C"][e], st["a_n"][e], st["a_m"][e]
            gif = _matmul_wt(h, P["ab_wt"], layer=e, row0=4 * w_a, n=2 * heads_a)
            if emit:
                proj_a, wq["a", e] = _matmul_wt(h, P["ab_wt"], layer=e, row0=0, n=4 * w_a, emit=True)
                proj_b, proj_b16, wq["b", e] = _matmul_wt(h, P["ab_wt_b"], layer=e, row0=0, n=3 * w_b, emit=True,
                                                          bf16_copy=True)
            else:
                proj_a = _matmul_wt(h, wq["a", e][None], layer=0, row0=0, n=4 * w_a)
                proj_b = _matmul_wt(h, wq["b", e][None], layer=0, row0=0, n=3 * w_b)
            ha, C, n, mm = _mlstm(proj_a, gif, P["ab_b_if"][e], P["a_norm_g"][e], c0, n0, m0, batch=b, seq=t)
            if st is None:
                hb = _attn_prompt(proj_b16, P["bias_table"][e], batch=b, seq=t, heads=heads_b, dh=dh_b)
                keep = min(BAND_PAST, t)
            else:
                hb = _attn_sample(proj_b, st["b_k"], st["b_v"], e, P["bias_table"][e],
                                  batch=b, seq=t, heads=heads_b, dh=dh_b)
                keep = t
            kept = proj_b.reshape(b, t, 3 * w_b)[:, t - keep:, :]
            a_C.append(C)
            a_n.append(n)
            a_m.append(mm)
            b_k.append(kept[:, :, w_b:2 * w_b].reshape(b, keep, heads_b, dh_b))
            b_v.append(kept[:, :, 2 * w_b:].reshape(b, keep, heads_b, dh_b))
            x = _matmul_resid([ha, hb], P["ab_w_out"], x.reshape(m, d), gate_rows(gt1.reshape(b, d)), layer=e)
        else:
            o = l // 2
            if st is None:
                s0 = jnp.zeros((b, heads_c, dh_c, dh_c), F32)
                shift0 = jnp.zeros((b, d), F32)
            else:
                s0, shift0 = st["c_S"][o], st["c_shift"][o]
            (xr, xw, xk, xv, xa, xg), shift = _norm_mix(x, shift0, P["norm1_g"][l], sc1, sh1, P["c_mu"][o])
            flat = lambda z: z.reshape(m, d)
            if emit:
                r, wq["r", o] = _matmul(flat(xr), P["c_wr"], layer=o, emit=True)
                k, wq["k", o] = _matmul(flat(xk), P["c_wk"], layer=o, emit=True)
                v, wq["v", o] = _matmul(flat(xv), P["c_wv"], layer=o, emit=True)
            else:
                r, k, v = (_matmul(flat(z), wq[name, o]) for z, name in ((xr, "r"), (xk, "k"), (xv, "v")))
            lw = _lora(flat(xw), P["c_w1"][o], P["c_w2"][o], mid="tanh", epilogue="log_decay", bias=P["c_w0"][o])
            a = _lora(flat(xa), P["c_a1"][o], P["c_a2"][o], mid="none", epilogue="sigmoid", bias=P["c_a0"][o])
            g = _lora(flat(xg), P["c_g1"][o], P["c_g2"][o], mid="sigmoid", epilogue="none")
            if v_first is None:
                v_first = v
            else:
                v = _lora(flat(xv), P["c_v1"][o - 1], P["c_v2"][o - 1], mid="none", epilogue="vmix",
                          bias=P["c_v0"][o - 1], extra=(v, v_first))
            y, S = _rwkv(r, lw, k, v, a, g, P["c_k_k"][o], P["c_k_a"][o], P["c_r_k"][o], P["c_lnx_g"][o],
                         P["c_lnx_b"][o], s0, batch=b, seq=t)
            c_S.append(S)
            c_sh.append(shift)
            x = _matmul_resid([y], P["c_wo"], x.reshape(m, d), gate_rows(gt1.reshape(b, d)), layer=o)
        x = x.reshape(b, t, d)
        h = _norm_mod(x, P["norm2_g"][l], sc2, sh2).reshape(m, d)
        if emit:
            ada_next = (c_all, ada_w, ada_b, l + 1) if l + 1 < depth else None
            outs = _matmul_swiglu(h, P["ffn_w1"], P["ffn_w3"], layer=l, emit=True, ada=ada_next, w2=P["ffn_w2"])
            hid, wq["w1", l], wq["w3", l] = outs[:3]
            if ada_next is not None:
                shared["mod", l + 1] = outs[3]
            wq["w2", l] = outs[-1]
        else:
            hid = _matmul_swiglu(h, wq["w1", l], wq["w3", l])
        x = _matmul_resid([hid], wq["w2", l], x.reshape(m, d), gate_rows(gt2.reshape(b, d)),
                          tm_pref=1024, tn_pref=512).reshape(b, t, d)
    y = _final_norm(x, P["final_g"])
    return (y, jnp.stack(a_C), jnp.stack(a_n), jnp.stack(a_m), jnp.stack(b_k), jnp.stack(b_v),
            jnp.stack(c_S), jnp.stack(c_sh))


def kernel(x_prompt, x_sample, c_prompt, c_sample, state_a_C, state_a_n, state_a_m, cache_b_k, cache_b_v, state_c_S, state_c_shift, ada_w, ada_b, norm1_g, norm2_g, final_g, ab_w_in, ab_b_if, a_norm_g, b_rel_bias, ab_w_out, c_mu, c_wr, c_wk, c_wv, c_wo, c_w0, c_w1, c_w2, c_a0, c_a1, c_a2, c_v0, c_v1, c_v2, c_g1, c_g2, c_k_k, c_k_a, c_r_k, c_lnx_g, c_lnx_b, ffn_w1, ffn_w3, ffn_w2):
    n_ab, dec_b, heads_a, dk, dv = state_a_C.shape
    _, _, width, heads_b, dh_b = cache_b_k.shape
    w_a = heads_a * dk
    gate_lo = 2 * w_a + 2 * heads_a * dv
    gate_hi = gate_lo + 2 * heads_a
    assert gate_lo == 4 * w_a
    ab_wt = jnp.swapaxes(ab_w_in, 1, 2)
    P = dict(ada_w=ada_w, norm1_g=norm1_g, norm2_g=norm2_g, final_g=final_g,
             ab_wt=ab_wt, ab_wt_b=ab_wt[:, gate_hi:, :],
             ab_b_if=ab_b_if, a_norm_g=a_norm_g,
             bias_table=_rel_bias_table(b_rel_bias),
             ab_w_out=ab_w_out, c_mu=c_mu, c_wr=c_wr, c_wk=c_wk, c_wv=c_wv, c_wo=c_wo,
             c_w0=c_w0, c_w1=c_w1, c_w2=c_w2, c_a0=c_a0, c_a1=c_a1, c_a2=c_a2, c_v0=c_v0, c_v1=c_v1, c_v2=c_v2,
             c_g1=c_g1, c_g2=c_g2, c_k_k=c_k_k, c_k_a=c_k_a, c_r_k=c_r_k, c_lnx_g=c_lnx_g, c_lnx_b=c_lnx_b,
             ffn_w1=ffn_w1, ffn_w3=ffn_w3, ffn_w2=ffn_w2,
             a_shape=state_a_C.shape[1:], b_shape=(heads_b, dh_b), c_shape=state_c_S.shape[1:])
    st = dict(a_C=state_a_C, a_n=state_a_n, a_m=state_a_m,
              b_k=cache_b_k.reshape(n_ab, dec_b, width * heads_b, dh_b),
              b_v=cache_b_v.reshape(n_ab, dec_b, width * heads_b, dh_b),
              c_S=state_c_S, c_shift=state_c_shift)

    n_p, n_s = c_prompt.shape[0], c_sample.shape[0]
    rows = -(-(n_p + n_s) // 16) * 16
    c_all = jnp.concatenate([c_prompt, c_sample, jnp.zeros((rows - n_p - n_s, c_prompt.shape[1]), F32)], axis=0)
    ada = (c_all, ada_w, ada_b.reshape(ada_b.shape[0], 1, ada_b.shape[1]))

    shared = {}
    outs_p = _trunk(x_prompt, slice(0, n_p), ada, P, None, shared)
    outs_s = _trunk(x_sample, slice(n_p, n_p + n_s), ada, P, st, shared)
    return (outs_p[0], outs_s[0]) + tuple(outs_p[1:]) + tuple(outs_s[1:])
```

```python
import functools

import numpy as np
import jax
import jax.numpy as jnp
from jax import lax
from jax.experimental import pallas as pl
from jax.experimental.pallas import tpu as pltpu

F32 = jnp.float32
BF16 = jnp.bfloat16

CHUNK = 64
N_PREV_CHUNKS = 8
BAND_PAST = N_PREV_CHUNKS * CHUNK
PAST_LEN = 2048
REL_MAX = 2 * CHUNK
GATE_CAP = 15.0
EPS = 1e-6
GN_EPS = 64e-5

V7X_VMEM_BYTES = 64 * 1024 * 1024
VMEM_LIMIT = V7X_VMEM_BYTES - 4 * 1024 * 1024
LANES = 128
MATMUL_ROW_SUBTILE = 1024


def _params(*sem):
    return pltpu.CompilerParams(dimension_semantics=sem, vmem_limit_bytes=VMEM_LIMIT)


def _tile(n, pref, mult):
    if n <= pref:
        return n
    t = (pref // mult) * mult
    while t >= mult:
        if n % t == 0:
            return t
        t -= mult
    return n


def _silu(x):
    return x * jax.nn.sigmoid(x)


def _resident(block_shape, index_map):
    return pl.BlockSpec(block_shape, index_map, pipeline_mode=pl.Buffered(1))


ADA_TILE = 256


def _ada_tile(c_ref, w_ref, b_ref):
    cs = _silu(c_ref[...]).astype(BF16)
    return jnp.dot(cs, w_ref[...].astype(BF16), preferred_element_type=F32) + b_ref[...]


def _ada_kernel(c_ref, w_ref, b_ref, o_ref):
    o_ref[...] = _ada_tile(c_ref, w_ref, b_ref)


def _ada_mod(c_all, ada_w, ada_b, layer):
    _, d, n = ada_w.shape
    rows = c_all.shape[0]
    tn = _tile(n, 2 * ADA_TILE, LANES)
    return pl.pallas_call(
        _ada_kernel,
        grid=(n // tn,),
        in_specs=[
            pl.BlockSpec((rows, d), lambda j: (0, 0)),
            pl.BlockSpec((None, d, tn), lambda j: (layer, 0, j)),
            pl.BlockSpec((None, 1, tn), lambda j: (layer, 0, j)),
        ],
        out_specs=pl.BlockSpec((rows, tn), lambda j: (0, j)),
        out_shape=jax.ShapeDtypeStruct((rows, n), F32),
        compiler_params=_params("arbitrary"),
        name="ada_mod",
    )(c_all, ada_w, ada_b)


def _norm_mod_value(x, g, sc, sh):
    y = x * lax.rsqrt(jnp.mean(x * x, axis=-1, keepdims=True) + EPS)
    return (y * g) * (1.0 + sc) + sh


def _norm_mod_kernel(x_ref, g_ref, sc_ref, sh_ref, o_ref):
    o_ref[0] = _norm_mod_value(x_ref[0], g_ref[...], sc_ref[0], sh_ref[0]).astype(o_ref.dtype)


def _norm_mod(x, g, sc, sh, out_dtype=BF16):
    b, t, d = x.shape
    tt = _tile(t, 256, 16)
    row = pl.BlockSpec((1, tt, d), lambda bi, i: (bi, i, 0))
    per_batch = pl.BlockSpec((1, 1, d), lambda bi, i: (bi, 0, 0))
    return pl.pallas_call(
        _norm_mod_kernel,
        grid=(b, t // tt),
        in_specs=[row, pl.BlockSpec((1, d), lambda bi, i: (0, 0)), per_batch, per_batch],
        out_specs=row,
        out_shape=jax.ShapeDtypeStruct((b, t, d), out_dtype),
        compiler_params=_params("arbitrary", "arbitrary"),
        name="norm_mod",
    )(x, g.reshape(1, d), sc, sh)


def _final_norm_kernel(x_ref, g_ref, o_ref):
    x = x_ref[0]
    o_ref[0] = (x * lax.rsqrt(jnp.mean(x * x, axis=-1, keepdims=True) + EPS)) * g_ref[...]


def _final_norm(x, g):
    b, t, d = x.shape
    tt = _tile(t, 256, 8)
    row = pl.BlockSpec((1, tt, d), lambda bi, i: (bi, i, 0))
    return pl.pallas_call(
        _final_norm_kernel,
        grid=(b, t // tt),
        in_specs=[row, pl.BlockSpec((1, d), lambda bi, i: (0, 0))],
        out_specs=row,
        out_shape=jax.ShapeDtypeStruct((b, t, d), F32),
        compiler_params=_params("arbitrary", "arbitrary"),
        name="final_norm",
    )(x, g.reshape(1, d))


def _norm_mix_kernel(x_ref, xp_ref, s0_ref, g_ref, sc_ref, sh_ref, mu_ref, *out_refs):
    i = pl.program_id(1)
    mix_refs, last_ref = out_refs[:6], out_refs[6]
    tt, d = x_ref.shape[1], x_ref.shape[2]
    x, xp = x_ref[0], xp_ref[0]
    inv = lax.rsqrt(jnp.mean(x * x, axis=-1, keepdims=True) + EPS)
    inv_p = lax.rsqrt(jnp.mean(xp * xp, axis=-1, keepdims=True) + EPS)
    cw = LANES if d % LANES == 0 else d
    row_id = lax.broadcasted_iota(jnp.int32, (tt, cw), 0)
    for c in range(d // cw):
        cols = slice(c * cw, (c + 1) * cw)
        g, sc, sh = g_ref[:, cols], sc_ref[0, :, cols], sh_ref[0, :, cols]
        h = ((x_ref[0, :, cols] * inv) * g) * (1.0 + sc) + sh
        hp = ((xp_ref[0, :, cols] * inv_p) * g) * (1.0 + sc) + sh
        prev_row = jnp.where(i == 0, s0_ref[0, :, cols], hp[7:8, :])
        shifted = jnp.where(row_id == 0, prev_row, pltpu.roll(h, 1, axis=0))
        xx = shifted - h
        for j in range(6):
            mix_refs[j][0, :, cols] = (h + xx * mu_ref[j:j + 1, cols]).astype(BF16)
        last_ref[0, :, cols] = h[tt - 8:, :]


def _norm_mix(x, shift0, g, sc, sh, mu):
    b, t, d = x.shape
    tt = _tile(t, 256, 16)
    row = pl.BlockSpec((1, tt, d), lambda bi, i: (bi, i, 0))
    prev8 = pl.BlockSpec((1, 8, d), lambda bi, i: (bi, jnp.maximum(i * (tt // 8) - 1, 0), 0))
    per_batch = pl.BlockSpec((1, 1, d), lambda bi, i: (bi, 0, 0))
    outs = pl.pallas_call(
        _norm_mix_kernel,
        grid=(b, t // tt),
        in_specs=[row, prev8, per_batch, pl.BlockSpec((1, d), lambda bi, i: (0, 0)), per_batch, per_batch,
                  pl.BlockSpec((6, d), lambda bi, i: (0, 0))],
        out_specs=[row] * 6 + [pl.BlockSpec((1, 8, d), lambda bi, i: (bi, 0, 0))],
        out_shape=[jax.ShapeDtypeStruct((b, t, d), BF16)] * 6 + [jax.ShapeDtypeStruct((b, 8, d), F32)],
        compiler_params=_params("arbitrary", "arbitrary"),
        name="norm_mix",
    )(x, x, shift0.reshape(b, 1, d), g.reshape(1, d), sc, sh, mu)
    return outs[:6], outs[6][:, 7, :]


def _mm_rows(tm):
    sub = MATMUL_ROW_SUBTILE if tm % MATMUL_ROW_SUBTILE == 0 else tm
    return sub, tm // sub


def _row_slice(m, sub):
    return pl.ds(m * sub, sub) if isinstance(m, int) else pl.ds(pl.multiple_of(m * sub, sub), sub)


def _for_row_subtiles(n_sub, body):
    if n_sub <= 2:
        for m in range(n_sub):
            body(m, 0)
    else:
        lax.fori_loop(0, n_sub, body, 0)


def _emit_bf16_weights(wb_refs, wq_refs):
    @pl.when(pl.program_id(0) == 0)
    def _():
        for wb_ref, wq_ref in zip(wb_refs, wq_refs):
            wq_ref[...] = wb_ref[...]


def _mm_kernel(a_ref, w_ref, o_ref, *rest, tm, w_transposed, emit):
    if w_ref.dtype == BF16:
        wb_ref = w_ref
    else:
        wb_ref = rest[-1]
        wb_ref[...] = w_ref[...].astype(BF16)
    sub, n_sub = _mm_rows(tm)

    def body(m, carry):
        rows = _row_slice(m, sub)
        if w_transposed:
            acc = lax.dot_general(a_ref[rows, :], wb_ref[...], (((1,), (1,)), ((), ())),
                                  preferred_element_type=F32)
        else:
            acc = jnp.dot(a_ref[rows, :], wb_ref[...], preferred_element_type=F32)
        o_ref[rows, :] = acc.astype(o_ref.dtype)
        return carry

    _for_row_subtiles(n_sub, body)
    if emit:
        _emit_bf16_weights([wb_ref], [rest[0]])


def _mm_resid_kernel(*refs, tm, parts):
    a_refs, w_refs = refs[:parts], refs[parts:2 * parts]
    r_ref, g_ref, o_ref = refs[2 * parts:2 * parts + 3]
    wb_refs = refs[2 * parts + 3:]
    if wb_refs:
        for w_ref, wb_ref in zip(w_refs, wb_refs):
            wb_ref[...] = w_ref[...].astype(BF16)
    else:
        wb_refs = w_refs
    sub, n_sub = _mm_rows(tm)
    per_row_gate = g_ref.shape[0] != 1

    def body(m, carry):
        rows = _row_slice(m, sub)
        acc = jnp.dot(a_refs[0][rows, :], wb_refs[0][...], preferred_element_type=F32)
        for a_ref, wb_ref in zip(a_refs[1:], wb_refs[1:]):
            acc = acc + jnp.dot(a_ref[rows, :], wb_ref[...], preferred_element_type=F32)
        gate = g_ref[rows, :] if per_row_gate else g_ref[...]
        o_ref[rows, :] = r_ref[rows, :] + gate * acc
        return carry

    _for_row_subtiles(n_sub, body)


def _mm_swiglu_kernel(a_ref, w1_ref, w3_ref, *rest, tm, emit, ada_tiles, cast_w2):
    rest = list(rest)
    if ada_tiles:
        c_ref, aw_ref, ab_ref = rest[:3]
        del rest[:3]
    if cast_w2:
        w2_ref = rest.pop(0)
    o_ref = rest.pop(0)
    if emit:
        wq_refs = rest[:2]
        del rest[:2]
        w1b_ref, w3b_ref = rest[-2:]
        w1b_ref[...] = w1_ref[...].astype(BF16)
        w3b_ref[...] = w3_ref[...].astype(BF16)
    else:
        w1b_ref, w3b_ref = w1_ref, w3_ref
    sub, n_sub = _mm_rows(tm)

    def body(m, carry):
        rows = _row_slice(m, sub)
        a = a_ref[rows, :]
        u = jnp.dot(a, w1b_ref[...], preferred_element_type=F32)
        v = jnp.dot(a, w3b_ref[...], preferred_element_type=F32)
        o_ref[rows, :] = (_silu(u) * v).astype(o_ref.dtype)
        return carry

    _for_row_subtiles(n_sub, body)
    if ada_tiles:
        mod_ref = rest.pop(0)
    if cast_w2:
        rest.pop(0)[...] = w2_ref[...].astype(BF16)
    if emit:
        _emit_bf16_weights([w1b_ref, w3b_ref], wq_refs)
    if ada_tiles:
        step = pl.program_id(0) * pl.num_programs(1) + pl.program_id(1)

        @pl.when(step < ada_tiles)
        def _():
            mod_ref[...] = _ada_tile(c_ref, aw_ref, ab_ref)


def _parked(n_tiles):
    return lambda i, j: jnp.where(i == 0, j, n_tiles - 1)


def _weight_spec(w, layer, k, tn):
    if w.ndim == 3:
        return pl.BlockSpec((None, k, tn), lambda i, j: (layer, 0, j))
    return pl.BlockSpec((k, tn), lambda i, j: (0, j))


def _matmul(a, w, *, layer=0, emit=False, tm_pref=2048, tn_pref=512):
    m, k = a.shape
    n = w.shape[-1]
    tm = _tile(m, tm_pref, MATMUL_ROW_SUBTILE)
    tn = _tile(n, tn_pref, LANES)
    park = _parked(n // tn)
    out_specs = [pl.BlockSpec((tm, tn), lambda i, j: (i, j))]
    out_shape = [jax.ShapeDtypeStruct((m, n), F32)]
    if emit:
        out_specs.append(pl.BlockSpec((k, tn), lambda i, j: (0, park(i, j))))
        out_shape.append(jax.ShapeDtypeStruct((k, n), BF16))
    outs = pl.pallas_call(
        functools.partial(_mm_kernel, tm=tm, w_transposed=False, emit=emit),
        grid=(m // tm, n // tn),
        in_specs=[_resident((tm, k), lambda i, j: (i, 0)), _weight_spec(w, layer, k, tn)],
        out_specs=out_specs,
        out_shape=out_shape,
        scratch_shapes=[pltpu.VMEM((k, tn), BF16)] if w.dtype != BF16 else [],
        compiler_params=_params("arbitrary", "arbitrary"),
        name="matmul",
    )(a, w)
    return tuple(outs) if emit else outs[0]


def _matmul_wt(a, wt, *, layer, row0, n, emit=False, tm_pref=2048, tn_pref=512):
    m, k = a.shape
    tm = _tile(m, tm_pref, MATMUL_ROW_SUBTILE)
    tn = _tile(n, tn_pref, LANES)
    assert row0 % tn == 0 and n % tn == 0
    park = _parked(n // tn)
    out_specs = [pl.BlockSpec((tm, tn), lambda i, j: (i, j))]
    out_shape = [jax.ShapeDtypeStruct((m, n), F32)]
    if emit:
        out_specs.append(pl.BlockSpec((tn, k), lambda i, j: (park(i, j), 0)))
        out_shape.append(jax.ShapeDtypeStruct((n, k), BF16))
    outs = pl.pallas_call(
        functools.partial(_mm_kernel, tm=tm, w_transposed=True, emit=emit),
        grid=(m // tm, n // tn),
        in_specs=[_resident((tm, k), lambda i, j: (i, 0)),
                  pl.BlockSpec((None, tn, k), lambda i, j: (layer, row0 // tn + j, 0))],
        out_specs=out_specs,
        out_shape=out_shape,
        scratch_shapes=[pltpu.VMEM((tn, k), BF16)] if wt.dtype != BF16 else [],
        compiler_params=_params("arbitrary", "arbitrary"),
        name="matmul_wt",
    )(a, wt)
    return tuple(outs) if emit else outs[0]


def _matmul_resid(a_parts, w, res, gate, *, layer=0, tm_pref=2048, tn_pref=512):
    parts = len(a_parts)
    m = a_parts[0].shape[0]
    ks = [a.shape[1] for a in a_parts]
    offs = [sum(ks[:p]) for p in range(parts)]
    n = w.shape[-1]
    assert w.shape[-2] == sum(ks) and all(off % k == 0 for off, k in zip(offs, ks))
    tm = _tile(m, tm_pref, MATMUL_ROW_SUBTILE)
    tn = _tile(n, tn_pref, LANES)
    if gate.shape[0] == 1:
        gate_spec = pl.BlockSpec((1, tn), lambda i, j: (0, j))
    else:
        gate_spec = pl.BlockSpec((tm, tn), lambda i, j: (i, j))
    if w.ndim == 3:
        w_specs = [pl.BlockSpec((None, k, tn), functools.partial(lambda i, j, blk: (layer, blk, j), blk=off // k))
                   for off, k in zip(offs, ks)]
    else:
        w_specs = [pl.BlockSpec((k, tn), functools.partial(lambda i, j, blk: (blk, j), blk=off // k))
                   for off, k in zip(offs, ks)]
    staged = w.dtype != BF16
    return pl.pallas_call(
        functools.partial(_mm_resid_kernel, tm=tm, parts=parts),
        grid=(m // tm, n // tn),
        in_specs=[_resident((tm, k), lambda i, j: (i, 0)) for k in ks] + w_specs
                 + [pl.BlockSpec((tm, tn), lambda i, j: (i, j)), gate_spec],
        out_specs=pl.BlockSpec((tm, tn), lambda i, j: (i, j)),
        out_shape=jax.ShapeDtypeStruct((m, n), F32),
        scratch_shapes=[pltpu.VMEM((k, tn), BF16) for k in ks] if staged else [],
        compiler_params=_params("arbitrary", "arbitrary"),
        name="matmul_resid",
    )(*a_parts, *([w] * parts), res, gate)


def _matmul_swiglu(a, w1, w3, *, layer=0, emit=False, ada=None, w2=None, tm_pref=2048, tn_pref=256):
    m, k = a.shape
    n = w1.shape[-1]
    assert emit == (w1.dtype != BF16)
    tm = _tile(m, tm_pref, MATMUL_ROW_SUBTILE)
    tn = _tile(n, tn_pref, LANES)
    n_i, n_j = m // tm, n // tn
    steps = n_i * n_j
    if ada is not None:
        c_all, ada_w, ada_b, next_layer = ada
        rows, d = c_all.shape
        n_mod = ada_w.shape[-1]
        if n_mod % ADA_TILE != 0 or n_mod // ADA_TILE > steps:
            outs = _matmul_swiglu(a, w1, w3, layer=layer, emit=emit, w2=w2, tm_pref=tm_pref, tn_pref=tn_pref)
            outs = list(outs) if isinstance(outs, tuple) else [outs]
            outs.insert(3 if emit else 1, _ada_mod(c_all, ada_w, ada_b, next_layer))
            return tuple(outs)
    if w2 is not None and (w2.shape[1] % steps != 0 or (w2.shape[1] // steps) % 16 != 0):
        outs = _matmul_swiglu(a, w1, w3, layer=layer, emit=emit, ada=ada, tm_pref=tm_pref, tn_pref=tn_pref)
        outs = outs if isinstance(outs, tuple) else (outs,)
        return outs + (w2[layer].astype(BF16),)
    step_of = lambda i, j: i * n_j + j
    in_specs = [_resident((tm, k), lambda i, j: (i, 0)), _weight_spec(w1, layer, k, tn),
                _weight_spec(w3, layer, k, tn)]
    args = [a, w1, w3]
    out_specs = [pl.BlockSpec((tm, tn), lambda i, j: (i, j))]
    out_shape = [jax.ShapeDtypeStruct((m, n), BF16)]
    if emit:
        park = _parked(n_j)
        out_specs += [pl.BlockSpec((k, tn), lambda i, j: (0, park(i, j)))] * 2
        out_shape += [jax.ShapeDtypeStruct((k, n), BF16)] * 2
    ada_tiles = 0
    if ada is not None:
        ada_tiles = n_mod // ADA_TILE
        tile_of = lambda i, j: jnp.minimum(step_of(i, j), ada_tiles - 1)
        in_specs += [pl.BlockSpec((rows, d), lambda i, j: (0, 0)),
                     pl.BlockSpec((None, d, ADA_TILE), lambda i, j: (next_layer, 0, tile_of(i, j))),
                     pl.BlockSpec((None, 1, ADA_TILE), lambda i, j: (next_layer, 0, tile_of(i, j)))]
        args += [c_all, ada_w, ada_b]
        out_specs.append(pl.BlockSpec((rows, ADA_TILE), lambda i, j: (0, tile_of(i, j))))
        out_shape.append(jax.ShapeDtypeStruct((rows, n_mod), F32))
    if w2 is not None:
        k2, n2 = w2.shape[1:]
        rows2 = k2 // steps
        in_specs.append(pl.BlockSpec((None, rows2, n2), lambda i, j: (layer, step_of(i, j), 0)))
        args.append(w2)
        out_specs.append(pl.BlockSpec((rows2, n2), lambda i, j: (step_of(i, j), 0)))
        out_shape.append(jax.ShapeDtypeStruct((k2, n2), BF16))
    outs = pl.pallas_call(
        functools.partial(_mm_swiglu_kernel, tm=tm, emit=emit, ada_tiles=ada_tiles, cast_w2=w2 is not None),
        grid=(n_i, n_j),
        in_specs=in_specs,
        out_specs=out_specs,
        out_shape=out_shape,
        scratch_shapes=[pltpu.VMEM((k, tn), BF16)] * 2 if emit else [],
        compiler_params=_params("arbitrary", "arbitrary"),
        name="matmul_swiglu",
    )(*args)
    return tuple(outs) if len(outs) > 1 else outs[0]


def _lora_kernel(a_ref, w1_ref, w2_ref, *rest, mid, epilogue):
    o_ref = rest[-1]
    z = jnp.dot(a_ref[...], w1_ref[...], preferred_element_type=F32)
    if mid == "tanh":
        z = jnp.tanh(z)
    elif mid == "sigmoid":
        z = jax.nn.sigmoid(z)
    z = jnp.dot(z.astype(BF16), w2_ref[...], preferred_element_type=F32)
    if epilogue == "log_decay":
        o_ref[...] = (-float(np.exp(-0.5))) * jax.nn.sigmoid(rest[0][...] + z)
    elif epilogue == "sigmoid":
        o_ref[...] = jax.nn.sigmoid(rest[0][...] + z)
    elif epilogue == "vmix":
        v, vf = rest[1][...], rest[2][...]
        o_ref[...] = v + (vf - v) * jax.nn.sigmoid(rest[0][...] + z)
    else:
        o_ref[...] = z


def _lora(a, w1, w2, *, mid, epilogue, bias=None, extra=()):
    m, d = a.shape
    r = w1.shape[1]
    rp = -(-r // LANES) * LANES
    w1p = jnp.pad(w1, ((0, 0), (0, rp - r))).astype(BF16)
    w2p = jnp.pad(w2, ((0, rp - r), (0, 0))).astype(BF16)
    n = w2.shape[1]
    tm = _tile(m, 256, 16)
    row = pl.BlockSpec((tm, n), lambda i: (i, 0))
    in_specs = [pl.BlockSpec((tm, d), lambda i: (i, 0)), _resident((d, rp), lambda i: (0, 0)),
                _resident((rp, n), lambda i: (0, 0))]
    args = [a, w1p, w2p]
    if bias is not None:
        in_specs.append(pl.BlockSpec((1, n), lambda i: (0, 0)))
        args.append(bias.reshape(1, n))
    for e in extra:
        in_specs.append(row)
        args.append(e)
    return pl.pallas_call(
        functools.partial(_lora_kernel, mid=mid, epilogue=epilogue),
        grid=(m // tm,),
        in_specs=in_specs,
        out_specs=row,
        out_shape=jax.ShapeDtypeStruct((m, n), F32),
        compiler_params=_params("arbitrary"),
        name="lora_" + epilogue,
    )(*args)


def _dot_nt(a, b):
    return lax.dot_general(a, b, (((1,), (1,)), ((), ())), preferred_element_type=F32)


def _dot_tn(a, b):
    return lax.dot_general(a, b, (((0,), (0,)), ((), ())), preferred_element_type=F32)


MLSTM_CHUNK = 256


def _soft_cap(x):
    return GATE_CAP * jnp.tanh(x / GATE_CAP)


def _mlstm_kernel(q_ref, k_ref, v_ref, o_ref, gc_ref, gr_ref, bc_ref, br_ref, ng_ref, c0_ref, n0_ref, m0_ref,
                  h_ref, c_ref, n_ref, m_ref, *, heads, dk, dv, length):
    c_idx = pl.program_id(1)

    @pl.when(c_idx == 0)
    def _():
        c_ref[...] = c0_ref[...]
        n_ref[...] = n0_ref[...]
        m_ref[...] = m0_ref[...]

    gcol = gc_ref[...] + bc_ref[...]
    grow = gr_ref[...] + br_ref[...]
    li_col, lf_col = _soft_cap(gcol[:, :heads]), jax.nn.log_sigmoid(_soft_cap(gcol[:, heads:]))
    li_row, lf_row = _soft_cap(grow[:heads, :]), jax.nn.log_sigmoid(_soft_cap(grow[heads:, :]))
    r_id = lax.broadcasted_iota(jnp.int32, (length, length), 0)
    c_id = lax.broadcasted_iota(jnp.int32, (length, length), 1)
    tril = c_id <= r_id
    scale = dk ** -0.5

    hs = range(heads)
    gates = []
    for h in hs:
        lf_r, li_r = lf_row[h:h + 1, :], li_row[h:h + 1, :]
        lf_c, li_c = lf_col[:, h:h + 1], li_col[:, h:h + 1]
        b_col = jnp.sum(jnp.where(tril, lf_r, 0.0), axis=1, keepdims=True)
        b_row = jnp.sum(jnp.where(r_id <= c_id, lf_c, 0.0), axis=0, keepdims=True)
        b_last = b_col[length - 1:length, :]
        m_prev = m_ref[0, h:h + 1, :]
        log_d = jnp.where(tril, b_col - b_row + li_r, -jnp.inf)
        inter = b_col + m_prev
        m_t = jnp.maximum(inter, jnp.max(log_d, axis=1, keepdims=True))
        m_new = m_t[length - 1:length, :]
        gates.append(dict(p=jnp.exp(log_d - m_t), g=jnp.exp(inter - m_t), m_t=m_t, m_new=m_new,
                          wk_col=jnp.exp(b_last - b_col + li_c - m_new),
                          decay=jnp.exp(b_last + m_prev - m_new)))

    q = [q_ref[:, h * dk:(h + 1) * dk] for h in hs]
    k = [k_ref[:, h * dk:(h + 1) * dk] * scale for h in hs]
    kw = [k[h] * gates[h]["wk_col"] for h in hs]
    qb = [x.astype(BF16) for x in q]
    kb = [x.astype(BF16) for x in k]
    vb = [v_ref[:, h * dv:(h + 1) * dv].astype(BF16) for h in hs]
    c_state = [c_ref[0, h] for h in hs]
    n_state = [n_ref[0, h:h + 1, :] for h in hs]
    s_qk = [_dot_nt(qb[h], kb[h]) for h in hs]
    q_c = [jnp.dot(qb[h], c_state[h].astype(BF16), preferred_element_type=F32) for h in hs]
    kv = [_dot_tn(kw[h].astype(BF16), vb[h]) for h in hs]
    wqk = [gates[h]["p"] * s_qk[h] for h in hs]
    num = [jnp.dot(wqk[h].astype(BF16), vb[h], preferred_element_type=F32) + gates[h]["g"] * q_c[h] for h in hs]

    for h in hs:
        g = gates[h]
        den = jnp.sum(wqk[h], axis=1, keepdims=True) + g["g"] * jnp.sum(q[h] * n_state[h], axis=1, keepdims=True)
        hh = num[h] / jnp.maximum(jnp.abs(den), jnp.exp(-g["m_t"]))
        hh = hh * lax.rsqrt(jnp.mean(hh * hh, axis=1, keepdims=True) + EPS)
        hh = hh * ng_ref[:, h * dv:(h + 1) * dv]
        h_ref[:, h * dv:(h + 1) * dv] = (jax.nn.sigmoid(o_ref[:, h * dv:(h + 1) * dv]) * hh).astype(h_ref.dtype)
        c_ref[0, h] = g["decay"] * c_state[h] + kv[h]
        n_ref[0, h:h + 1, :] = g["decay"] * n_state[h] + jnp.sum(kw[h], axis=0, keepdims=True)
        m_ref[0, h:h + 1, :] = g["m_new"]


def _mlstm(proj, gif, b_if, a_norm_g, c0, n0, m0, *, batch, seq):
    _, heads, dk, dv = c0.shape
    length = _tile(seq, MLSTM_CHUNK, CHUNK)
    nc = seq // length
    wq, wv = heads * dk, heads * dv
    assert wq == wv
    gcol = gif
    grow = gif.reshape(batch, nc, length, 2 * heads).transpose(0, 1, 3, 2)
    row = lambda col: pl.BlockSpec((length, wq), lambda b, c: (b * nc + c, col))
    state4 = pl.BlockSpec((1, heads, dk, dv), lambda b, c: (b, 0, 0, 0))
    state3 = pl.BlockSpec((1, heads, dk), lambda b, c: (b, 0, 0))
    state_m = pl.BlockSpec((1, heads, 1), lambda b, c: (b, 0, 0))
    h, c_out, n_out, m_out = pl.pallas_call(
        functools.partial(_mlstm_kernel, heads=heads, dk=dk, dv=dv, length=length),
        grid=(batch, nc),
        in_specs=[row(0), row(1), row(2), row(3),
                  pl.BlockSpec((length, 2 * heads), lambda b, c: (b * nc + c, 0)),
                  pl.BlockSpec((None, None, 2 * heads, length), lambda b, c: (b, c, 0, 0)),
                  pl.BlockSpec((1, 2 * heads), lambda b, c: (0, 0)),
                  pl.BlockSpec((2 * heads, 1), lambda b, c: (0, 0)),
                  pl.BlockSpec((1, wv), lambda b, c: (0, 0)),
                  state4, state3, state_m],
        out_specs=[pl.BlockSpec((length, wv), lambda b, c: (b * nc + c, 0)), state4, state3, state_m],
        out_shape=[jax.ShapeDtypeStruct((batch * seq, wv), BF16),
                   jax.ShapeDtypeStruct((batch, heads, dk, dv), F32),
                   jax.ShapeDtypeStruct((batch, heads, dk), F32),
                   jax.ShapeDtypeStruct((batch, heads, 1), F32)],
        compiler_params=_params("arbitrary", "arbitrary"),
        name="mlstm",
    )(proj, proj, proj, proj, gcol, grow, b_if.reshape(1, 2 * heads), b_if.reshape(2 * heads, 1),
      a_norm_g.reshape(1, wv), c0, n0, m0.reshape(batch, heads, 1))
    return h, c_out, n_out, m_out.reshape(batch, heads)


def _rel_bias_table(rel_bias):
    n_e, heads, rel_size = rel_bias.shape
    band = BAND_PAST + CHUNK
    i = np.arange(CHUNK)[:, None]
    j = np.arange(band)[None, :]
    rel = (np.clip(i - j + BAND_PAST, -(CHUNK - 1), REL_MAX) + (CHUNK - 1)).reshape(-1)
    onehot = (jnp.asarray(rel, jnp.int32)[None, :] == jnp.arange(rel_size, dtype=jnp.int32)[:, None]).astype(F32)
    table = jnp.dot(rel_bias.reshape(n_e * heads, rel_size), onehot, precision=lax.Precision.HIGHEST)
    return table.reshape(n_e, heads, CHUNK, band)


def _band_chunks(qs, ks, vs, biases, scale):
    n = range(len(qs))
    s = [_dot_nt(qs[i], ks[i]) * scale + biases[i] for i in n]
    p = [jnp.exp(s[i] - jnp.max(s[i], axis=1, keepdims=True)) for i in n]
    o = [jnp.dot(p[i].astype(BF16), vs[i], preferred_element_type=F32) for i in n]
    return [o[i] / jnp.sum(p[i], axis=1, keepdims=True) for i in n]


ATTN_CHUNKS_PER_ITER = 8


def _attn_prompt_kernel(q_ref, k_ref, v_ref, bias_ref, o_ref, kb_ref, vb_ref, *, seq, scale):
    kb_ref[...] = k_ref[...].astype(BF16)
    vb_ref[...] = v_ref[...].astype(BF16)
    nc = seq // CHUNK
    band = BAND_PAST + CHUNK
    bias = bias_ref[0]

    lead = list(range(min(N_PREV_CHUNKS, nc)))
    for c0 in range(0, len(lead), ATTN_CHUNKS_PER_ITER):
        cs = lead[c0:c0 + ATTN_CHUNKS_PER_ITER]
        widths = [(c + 1) * CHUNK for c in cs]
        outs = _band_chunks([q_ref[c * CHUNK:(c + 1) * CHUNK, :].astype(BF16) for c in cs],
                            [kb_ref[0:w, :] for w in widths], [vb_ref[0:w, :] for w in widths],
                            [bias[:, band - w:] for w in widths], scale)
        for c, o in zip(cs, outs):
            o_ref[c * CHUNK:(c + 1) * CHUNK, :] = o.astype(o_ref.dtype)

    rest = nc - N_PREV_CHUNKS
    per_iter = ATTN_CHUNKS_PER_ITER if rest % ATTN_CHUNKS_PER_ITER == 0 else 1

    def body(it, carry):
        starts = [pl.multiple_of((N_PREV_CHUNKS + it * per_iter + u) * CHUNK, CHUNK) for u in range(per_iter)]
        k_rows = [pl.ds(pl.multiple_of(s - BAND_PAST, CHUNK), band) for s in starts]
        outs = _band_chunks([q_ref[pl.ds(s, CHUNK), :].astype(BF16) for s in starts],
                            [kb_ref[r, :] for r in k_rows], [vb_ref[r, :] for r in k_rows],
                            [bias] * per_iter, scale)
        for s, o in zip(starts, outs):
            o_ref[pl.ds(s, CHUNK), :] = o.astype(o_ref.dtype)
        return carry

    if rest > 0:
        lax.fori_loop(0, rest // per_iter, body, 0)


def _attn_prompt(proj_b, bias_table, *, batch, seq, heads, dh):
    col = lambda base: pl.BlockSpec((seq, dh), lambda b, h: (b, base * heads + h))
    return pl.pallas_call(
        functools.partial(_attn_prompt_kernel, seq=seq, scale=dh ** -0.5),
        grid=(batch, heads),
        in_specs=[col(0), col(1), col(2),
                  pl.BlockSpec((1, CHUNK, BAND_PAST + CHUNK), lambda b, h: (h, 0, 0))],
        out_specs=pl.BlockSpec((seq, dh), lambda b, h: (b, h)),
        out_shape=jax.ShapeDtypeStruct((batch * seq, heads * dh), BF16),
        scratch_shapes=[pltpu.VMEM((seq, dh), BF16), pltpu.VMEM((seq, dh), BF16)],
        compiler_params=_params("arbitrary", "arbitrary"),
        name="attn_prompt",
    )(proj_b, proj_b, proj_b, bias_table)


def _attn_sample_kernel(q_ref, k_ref, v_ref, pk_ref, pv_ref, bias_ref, o_ref, *, seq, width, heads, dh, scale):
    off = BAND_PAST - width
    hs = range(heads)
    cols = [slice(h * dh, (h + 1) * dh) for h in hs]
    rows = [pl.ds(h, width, stride=heads) for h in hs]
    q = [q_ref[:, c].astype(BF16) for c in cols]
    s_past = [_dot_nt(q[h], pk_ref[rows[h], :].astype(BF16)) * scale + bias_ref[h, :seq, off:off + width] for h in hs]
    s_new = [_dot_nt(q[h], k_ref[:, cols[h]].astype(BF16)) * scale + bias_ref[h, :seq, BAND_PAST:BAND_PAST + seq]
             for h in hs]
    m = [jnp.maximum(jnp.max(s_past[h], axis=1, keepdims=True), jnp.max(s_new[h], axis=1, keepdims=True)) for h in hs]
    p_past = [jnp.exp(s_past[h] - m[h]) for h in hs]
    p_new = [jnp.exp(s_new[h] - m[h]) for h in hs]
    o = [jnp.dot(p_past[h].astype(BF16), pv_ref[rows[h], :].astype(BF16), preferred_element_type=F32)
         + jnp.dot(p_new[h].astype(BF16), v_ref[:, cols[h]].astype(BF16), preferred_element_type=F32) for h in hs]
    for h in hs:
        denom = jnp.sum(p_past[h], axis=1, keepdims=True) + jnp.sum(p_new[h], axis=1, keepdims=True)
        o_ref[:, cols[h]] = (o[h] / denom).astype(o_ref.dtype)


def _attn_sample(proj_b, past_k, past_v, layer, bias_table, *, batch, seq, heads, dh):
    width = past_k.shape[2] // heads
    wb = heads * dh
    col = lambda base: pl.BlockSpec((seq, wb), lambda b: (b, base))
    past = pl.BlockSpec((None, None, width * heads, dh), lambda b: (layer, b, 0, 0))
    return pl.pallas_call(
        functools.partial(_attn_sample_kernel, seq=seq, width=width, heads=heads, dh=dh, scale=dh ** -0.5),
        grid=(batch,),
        in_specs=[col(0), col(1), col(2), past, past,
                  pl.BlockSpec((heads, CHUNK, BAND_PAST + CHUNK), lambda b: (0, 0, 0))],
        out_specs=pl.BlockSpec((seq, wb), lambda b: (b, 0)),
        out_shape=jax.ShapeDtypeStruct((batch * seq, wb), BF16),
        compiler_params=_params("arbitrary"),
        name="attn_sample",
    )(proj_b, proj_b, proj_b, past_k, past_v, bias_table)


def _split_dot(a_exact, x):
    hi = x.astype(BF16)
    lo = (x - hi.astype(F32)).astype(BF16)
    return jnp.dot(a_exact, hi, preferred_element_type=F32) + jnp.dot(a_exact, lo, preferred_element_type=F32)


def _block_diag_rows(x, lo_mask):
    return jnp.concatenate([jnp.where(lo_mask, x, 0.0), jnp.where(lo_mask, 0.0, x)], axis=0).astype(BF16)


def _rwkv_kernel(r_ref, lw_ref, k_ref, v_ref, a_ref, g_ref, kk_ref, ka_ref, rk_ref, lg_ref, lb_ref, s0_ref,
                 o_ref, s_ref, s2_ref, *, pairs, dh, length, n_chunks):
    c_idx = pl.program_id(2)
    pw_ = 2 * dh
    ps = range(pairs)

    @pl.when(c_idx == 0)
    def _():
        for p in ps:
            s2_ref[p] = jnp.concatenate([s0_ref[0, 2 * p], s0_ref[0, 2 * p + 1]], axis=1)

    def lane_lo(shape, half):
        return lax.broadcasted_iota(jnp.int32, shape, 1) < half

    f_lo = lane_lo((length, pw_), dh)
    f_lo_s = lane_lo((dh, pw_), dh)
    t_lo = lane_lo((length, 2 * length), length)
    r_id = lax.broadcasted_iota(jnp.int32, (length, length), 0)
    c_id = lax.broadcasted_iota(jnp.int32, (length, length), 1)
    tril_bf = jnp.where(c_id <= r_id, 1.0, 0.0).astype(BF16)
    row2 = lax.broadcasted_iota(jnp.int32, (length, 2 * length), 0)
    col2 = lax.broadcasted_iota(jnp.int32, (length, 2 * length), 1) & (length - 1)
    strict2 = col2 < row2
    row4 = lax.broadcasted_iota(jnp.int32, (length, 4 * length), 0)
    col4 = lax.broadcasted_iota(jnp.int32, (length, 4 * length), 1) & (length - 1)
    incl4 = col4 <= row4
    n_double = max(int(np.ceil(np.log2(length))), 1)

    def head_sums(x, lo_mask):
        s_lo = jnp.sum(jnp.where(lo_mask, x, 0.0), axis=1, keepdims=True)
        s_hi = jnp.sum(jnp.where(lo_mask, 0.0, x), axis=1, keepdims=True)
        return jnp.where(lo_mask, s_lo, s_hi)

    rows = r_ref.shape[0]

    def load(ref):
        x = ref[...]
        return x if rows == length else jnp.concatenate([x, jnp.zeros((length - rows, x.shape[1]), F32)], axis=0)

    r_all, lw_all, k_all, v_all, a_all = load(r_ref), load(lw_ref), load(k_ref), load(v_ref), load(a_ref)
    cum = _split_dot(tril_bf, lw_all)
    w_in = jnp.exp(cum)
    w_inv = jnp.exp(-cum)
    w_ex = jnp.exp(cum - lw_all)
    kk_all = k_all * kk_ref[...]
    k2_all = k_all * (1.0 + (a_all - 1.0) * ka_ref[...])
    rt_all = r_all * w_in
    kt_all = k2_all * w_inv
    ba_all = a_all * w_inv
    rk2_all = r_all * k2_all * rk_ref[...]
    sl = lambda x, p: x[:, p * pw_:(p + 1) * pw_]

    lhs, rhs, rhs_bd = [], [], []
    for p in ps:
        kk = sl(kk_all, p)
        kk = kk / jnp.maximum(jnp.sqrt(head_sums(kk * kk, f_lo)), 1e-12)
        at = (-kk) * sl(w_ex, p)
        bt = kk * sl(ba_all, p)
        kt = sl(kt_all, p)
        lhs.append(jnp.concatenate([at, sl(rt_all, p)], axis=0).astype(BF16))
        rhs.append(jnp.concatenate([bt, kt], axis=0).astype(BF16))
        rhs_bd.append(jnp.concatenate([_block_diag_rows(bt, f_lo), _block_diag_rows(kt, f_lo)], axis=0))
    s0 = [s2_ref[p] for p in ps]
    vf = [sl(v_all, p) for p in ps]
    v_bd = [_block_diag_rows(vf[p], f_lo) for p in ps]
    aals = [_dot_nt(lhs[p], jnp.concatenate([rhs_bd[p], _block_diag_rows(s0[p], f_lo_s)], axis=0)) for p in ps]
    aa = [z[:, :4 * length] for z in aals]
    ls = [z[:, 4 * length:] for z in aals]
    x = [ls[p][:length] + jnp.dot(jnp.where(strict2, aa[p][:length, 2 * length:], 0.0).astype(BF16), v_bd[p],
                                  preferred_element_type=F32) for p in ps]
    pw = [jnp.where(strict2, aa[p][:length, :2 * length], 0.0) for p in ps]
    for j in range(n_double):
        pwb = [q.astype(BF16) for q in pw]
        if j + 1 < n_double:
            both = [jnp.dot(pwb[p], jnp.concatenate([_block_diag_rows(x[p], f_lo), _block_diag_rows(pw[p], t_lo)],
                                                    axis=1), preferred_element_type=F32) for p in ps]
            x = [x[p] + both[p][:, :pw_] for p in ps]
            pw = [both[p][:, pw_:] for p in ps]
        else:
            x = [x[p] + jnp.dot(pwb[p], _block_diag_rows(x[p], f_lo), preferred_element_type=F32) for p in ps]
    uv_bd = [jnp.concatenate([_block_diag_rows(x[p], f_lo), v_bd[p]], axis=0) for p in ps]
    y = [ls[p][length:] + jnp.dot(jnp.where(incl4, aa[p][length:, :], 0.0).astype(BF16), uv_bd[p],
                                  preferred_element_type=F32) for p in ps]
    uv = [jnp.concatenate([x[p], vf[p]], axis=0).astype(BF16) for p in ps]
    ds = [_dot_tn(uv[p], rhs[p]) for p in ps]
    for p in ps:
        delta = jnp.where(f_lo_s, ds[p][:dh, :], ds[p][dh:, :])
        s2_ref[p] = (s0[p] + delta) * sl(w_in, p)[length - 1:length, :]

    inv_dh = 1.0 / dh
    for p in ps:
        cols = slice(p * pw_, (p + 1) * pw_)
        yc = y[p] - head_sums(y[p], f_lo) * inv_dh
        yn = yc * lax.rsqrt(head_sums(yc * yc, f_lo) * inv_dh + GN_EPS)
        yn = yn * lg_ref[:, cols] + lb_ref[:, cols]
        bonus = head_sums(sl(rk2_all, p), f_lo) * vf[p]
        o_ref[:, cols] = ((yn + bonus)[:rows] * g_ref[:, cols]).astype(o_ref.dtype)

    @pl.when(c_idx == n_chunks - 1)
    def _():
        for p in ps:
            s_ref[0, 2 * p] = s2_ref[p][:, :dh]
            s_ref[0, 2 * p + 1] = s2_ref[p][:, dh:]


def _rwkv(r, lw, k, v, a, g, k_k, k_a, r_k, lnx_g, lnx_b, s0, *, batch, seq):
    _, heads, dh, _ = s0.shape
    d = heads * dh
    rows = min(CHUNK, seq)
    length = CHUNK
    nc = seq // rows
    assert 2 * dh == LANES and heads % 2 == 0 and rows % 8 == 0
    hg = _tile(heads, 32, 2)
    wg = hg * dh
    row = pl.BlockSpec((rows, wg), lambda b, gi, c: (b * nc + c, gi))
    par = pl.BlockSpec((1, wg), lambda b, gi, c: (0, gi))
    state = pl.BlockSpec((1, hg, dh, dh), lambda b, gi, c: (b, gi, 0, 0))
    return pl.pallas_call(
        functools.partial(_rwkv_kernel, pairs=hg // 2, dh=dh, length=length, n_chunks=nc),
        grid=(batch, heads // hg, nc),
        in_specs=[row] * 6 + [par] * 5 + [state],
        out_specs=[row, state],
        out_shape=[jax.ShapeDtypeStruct((batch * seq, d), BF16), jax.ShapeDtypeStruct(s0.shape, F32)],
        scratch_shapes=[pltpu.VMEM((hg // 2, dh, 2 * dh), F32)],
        compiler_params=_params("arbitrary", "arbitrary", "arbitrary"),
        name="rwkv",
    )(r, lw, k, v, a, g, k_k.reshape(1, d), k_a.reshape(1, d), r_k.reshape(1, d), lnx_g.reshape(1, d),
      lnx_b.reshape(1, d), s0)


def _trunk(x, rows, ada, P, st, shared):
    emit = st is None
    c_all, ada_w, ada_b = ada
    wq = shared
    b, t, d = x.shape
    m = b * t
    depth = P["ada_w"].shape[0]
    _, heads_a, dk, dv = P["a_shape"]
    heads_b, dh_b = P["b_shape"]
    _, heads_c, dh_c, _ = P["c_shape"]
    w_a = heads_a * dk
    w_b = heads_b * dh_b
    a_C, a_n, a_m, b_k, b_v, c_S, c_sh = [], [], [], [], [], [], []
    v_first = None

    def gate_rows(gt):
        return gt if b == 1 else jnp.broadcast_to(gt[:, None, :], (b, t, d)).reshape(m, d)

    for l in range(depth):
        if emit and l == 0:
            shared["mod", 0] = _ada_mod(c_all, ada_w, ada_b, 0)
        sh1, sc1, gt1, sh2, sc2, gt2 = (z.reshape(b, 1, d) for z in jnp.split(shared["mod", l][rows], 6, axis=-1))
        if l % 2 == 0:
            e = l // 2
            h = _norm_mod(x, P["norm1_g"][l], sc1, sh1).reshape(m, d)
            if st is None:
                c0 = jnp.zeros((b, heads_a, dk, dv), F32)
                n0 = jnp.zeros((b, heads_a, dk), F32)
                m0 = jnp.zeros((b, heads_a), F32)
            else:
                c0, n0, m0 = st["a_C"][e], st["a_n"][e], st["a_m"][e]
            gif = _matmul_wt(h, P["ab_wt"], layer=e, row0=4 * w_a, n=2 * heads_a)
            if emit:
                proj_a, wq["a", e] = _matmul_wt(h, P["ab_wt"], layer=e, row0=0, n=4 * w_a, emit=True)
                proj_b, wq["b", e] = _matmul_wt(h, P["ab_wt_b"], layer=e, row0=0, n=3 * w_b, emit=True)
            else:
                proj_a = _matmul_wt(h, wq["a", e][None], layer=0, row0=0, n=4 * w_a)
                proj_b = _matmul_wt(h, wq["b", e][None], layer=0, row0=0, n=3 * w_b)
            ha, C, n, mm = _mlstm(proj_a, gif, P["ab_b_if"][e], P["a_norm_g"][e], c0, n0, m0, batch=b, seq=t)
            if st is None:
                hb = _attn_prompt(proj_b, P["bias_table"][e], batch=b, seq=t, heads=heads_b, dh=dh_b)
                keep = min(BAND_PAST, t)
            else:
                hb = _attn_sample(proj_b, st["b_k"], st["b_v"], e, P["bias_table"][e],
                                  batch=b, seq=t, heads=heads_b, dh=dh_b)
                keep = t
            kept = proj_b.reshape(b, t, 3 * w_b)[:, t - keep:, :]
            a_C.append(C)
            a_n.append(n)
            a_m.append(mm)
            b_k.append(kept[:, :, w_b:2 * w_b].reshape(b, keep, heads_b, dh_b))
            b_v.append(kept[:, :, 2 * w_b:].reshape(b, keep, heads_b, dh_b))
            x = _matmul_resid([ha, hb], P["ab_w_out"], x.reshape(m, d), gate_rows(gt1.reshape(b, d)), layer=e)
        else:
            o = l // 2
            if st is None:
                s0 = jnp.zeros((b, heads_c, dh_c, dh_c), F32)
                shift0 = jnp.zeros((b, d), F32)
            else:
                s0, shift0 = st["c_S"][o], st["c_shift"][o]
            (xr, xw, xk, xv, xa, xg), shift = _norm_mix(x, shift0, P["norm1_g"][l], sc1, sh1, P["c_mu"][o])
            flat = lambda z: z.reshape(m, d)
            if emit:
                r, wq["r", o] = _matmul(flat(xr), P["c_wr"], layer=o, emit=True)
                k, wq["k", o] = _matmul(flat(xk), P["c_wk"], layer=o, emit=True)
                v, wq["v", o] = _matmul(flat(xv), P["c_wv"], layer=o, emit=True)
            else:
                r, k, v = (_matmul(flat(z), wq[name, o]) for z, name in ((xr, "r"), (xk, "k"), (xv, "v")))
            lw = _lora(flat(xw), P["c_w1"][o], P["c_w2"][o], mid="tanh", epilogue="log_decay", bias=P["c_w0"][o])
            a = _lora(flat(xa), P["c_a1"][o], P["c_a2"][o], mid="none", epilogue="sigmoid", bias=P["c_a0"][o])
            g = _lora(flat(xg), P["c_g1"][o], P["c_g2"][o], mid="sigmoid", epilogue="none")
            if v_first is None:
                v_first = v
            else:
                v = _lora(flat(xv), P["c_v1"][o - 1], P["c_v2"][o - 1], mid="none", epilogue="vmix",
                          bias=P["c_v0"][o - 1], extra=(v, v_first))
            y, S = _rwkv(r, lw, k, v, a, g, P["c_k_k"][o], P["c_k_a"][o], P["c_r_k"][o], P["c_lnx_g"][o],
                         P["c_lnx_b"][o], s0, batch=b, seq=t)
            c_S.append(S)
            c_sh.append(shift)
            x = _matmul_resid([y], P["c_wo"], x.reshape(m, d), gate_rows(gt1.reshape(b, d)), layer=o)
        x = x.reshape(b, t, d)
        h = _norm_mod(x, P["norm2_g"][l], sc2, sh2).reshape(m, d)
        if emit:
            ada_next = (c_all, ada_w, ada_b, l + 1) if l + 1 < depth else None
            outs = _matmul_swiglu(h, P["ffn_w1"], P["ffn_w3"], layer=l, emit=True, ada=ada_next, w2=P["ffn_w2"])
            hid, wq["w1", l], wq["w3", l] = outs[:3]
            if ada_next is not None:
                shared["mod", l + 1] = outs[3]
            wq["w2", l] = outs[-1]
        else:
            hid = _matmul_swiglu(h, wq["w1", l], wq["w3", l])
        x = _matmul_resid([hid], wq["w2", l], x.reshape(m, d), gate_rows(gt2.reshape(b, d)),
                          tm_pref=1024, tn_pref=512).reshape(b, t, d)
    y = _final_norm(x, P["final_g"])
    return (y, jnp.stack(a_C), jnp.stack(a_n), jnp.stack(a_m), jnp.stack(b_k), jnp.stack(b_v),
            jnp.stack(c_S), jnp.stack(c_sh))


def kernel(x_prompt, x_sample, c_prompt, c_sample, state_a_C, state_a_n, state_a_m, cache_b_k, cache_b_v, state_c_S, state_c_shift, ada_w, ada_b, norm1_g, norm2_g, final_g, ab_w_in, ab_b_if, a_norm_g, b_rel_bias, ab_w_out, c_mu, c_wr, c_wk, c_wv, c_wo, c_w0, c_w1, c_w2, c_a0, c_a1, c_a2, c_v0, c_v1, c_v2, c_g1, c_g2, c_k_k, c_k_a, c_r_k, c_lnx_g, c_lnx_b, ffn_w1, ffn_w3, ffn_w2):
    n_ab, dec_b, heads_a, dk, dv = state_a_C.shape
    _, _, width, heads_b, dh_b = cache_b_k.shape
    w_a = heads_a * dk
    gate_lo = 2 * w_a + 2 * heads_a * dv
    gate_hi = gate_lo + 2 * heads_a
    assert gate_lo == 4 * w_a
    ab_wt = jnp.swapaxes(ab_w_in, 1, 2)
    P = dict(ada_w=ada_w, norm1_g=norm1_g, norm2_g=norm2_g, final_g=final_g,
             ab_wt=ab_wt, ab_wt_b=ab_wt[:, gate_hi:, :],
             ab_b_if=ab_b_if, a_norm_g=a_norm_g,
             bias_table=_rel_bias_table(b_rel_bias),
             ab_w_out=ab_w_out, c_mu=c_mu, c_wr=c_wr, c_wk=c_wk, c_wv=c_wv, c_wo=c_wo,
             c_w0=c_w0, c_w1=c_w1, c_w2=c_w2, c_a0=c_a0, c_a1=c_a1, c_a2=c_a2, c_v0=c_v0, c_v1=c_v1, c_v2=c_v2,
             c_g1=c_g1, c_g2=c_g2, c_k_k=c_k_k, c_k_a=c_k_a, c_r_k=c_r_k, c_lnx_g=c_lnx_g, c_lnx_b=c_lnx_b,
             ffn_w1=ffn_w1, ffn_w3=ffn_w3, ffn_w2=ffn_w2,
             a_shape=state_a_C.shape[1:], b_shape=(heads_b, dh_b), c_shape=state_c_S.shape[1:])
    st = dict(a_C=state_a_C, a_n=state_a_n, a_m=state_a_m,
              b_k=cache_b_k.reshape(n_ab, dec_b, width * heads_b, dh_b),
              b_v=cache_b_v.reshape(n_ab, dec_b, width * heads_b, dh_b),
              c_S=state_c_S, c_shift=state_c_shift)

    n_p, n_s = c_prompt.shape[0], c_sample.shape[0]
    rows = -(-(n_p + n_s) // 16) * 16
    c_all = jnp.concatenate([c_prompt, c_sample, jnp.zeros((rows - n_p - n_s, c_prompt.shape[1]), F32)], axis=0)
    ada = (c_all, ada_w, ada_b.reshape(ada_b.shape[0], 1, ada_b.shape[1]))

    shared = {}
    outs_p = _trunk(x_prompt, slice(0, n_p), ada, P, None, shared)
    outs_s = _trunk(x_sample, slice(n_p, n_p + n_s), ada, P, st, shared)
    return (outs_p[0], outs_s[0]) + tuple(outs_p[1:]) + tuple(outs_s[1:])
```

```python
import functools

import numpy as np
import jax
import jax.numpy as jnp
from jax import lax
from jax.experimental import pallas as pl
from jax.experimental.pallas import tpu as pltpu

F32 = jnp.float32
BF16 = jnp.bfloat16

CHUNK = 64
N_PREV_CHUNKS = 8
BAND_PAST = N_PREV_CHUNKS * CHUNK
PAST_LEN = 2048
REL_MAX = 2 * CHUNK
GATE_CAP = 15.0
EPS = 1e-6
GN_EPS = 64e-5

V7X_VMEM_BYTES = 64 * 1024 * 1024
VMEM_LIMIT = V7X_VMEM_BYTES - 4 * 1024 * 1024
LANES = 128
MATMUL_ROW_SUBTILE = 1024


def _params(*sem):
    return pltpu.CompilerParams(dimension_semantics=sem, vmem_limit_bytes=VMEM_LIMIT)


def _tile(n, pref, mult):
    if n <= pref:
        return n
    t = (pref // mult) * mult
    while t >= mult:
        if n % t == 0:
            return t
        t -= mult
    return n


def _silu(x):
    return x * jax.nn.sigmoid(x)


def _resident(block_shape, index_map):
    return pl.BlockSpec(block_shape, index_map, pipeline_mode=pl.Buffered(1))


ADA_TILE = 256


def _ada_tile(c_ref, w_ref, b_ref):
    cs = _silu(c_ref[...]).astype(BF16)
    return jnp.dot(cs, w_ref[...].astype(BF16), preferred_element_type=F32) + b_ref[...]


def _ada_kernel(c_ref, w_ref, b_ref, o_ref):
    o_ref[...] = _ada_tile(c_ref, w_ref, b_ref)


def _ada_mod(c_all, ada_w, ada_b, layer):
    _, d, n = ada_w.shape
    rows = c_all.shape[0]
    tn = _tile(n, 2 * ADA_TILE, LANES)
    return pl.pallas_call(
        _ada_kernel,
        grid=(n // tn,),
        in_specs=[
            pl.BlockSpec((rows, d), lambda j: (0, 0)),
            pl.BlockSpec((None, d, tn), lambda j: (layer, 0, j)),
            pl.BlockSpec((None, 1, tn), lambda j: (layer, 0, j)),
        ],
        out_specs=pl.BlockSpec((rows, tn), lambda j: (0, j)),
        out_shape=jax.ShapeDtypeStruct((rows, n), F32),
        compiler_params=_params("arbitrary"),
        name="ada_mod",
    )(c_all, ada_w, ada_b)


def _norm_mod_value(x, g, sc, sh):
    y = x * lax.rsqrt(jnp.mean(x * x, axis=-1, keepdims=True) + EPS)
    return (y * g) * (1.0 + sc) + sh


def _norm_mod_kernel(x_ref, g_ref, sc_ref, sh_ref, o_ref):
    o_ref[0] = _norm_mod_value(x_ref[0], g_ref[...], sc_ref[0], sh_ref[0]).astype(o_ref.dtype)


def _norm_mod(x, g, sc, sh, out_dtype=BF16):
    b, t, d = x.shape
    tt = _tile(t, 256, 16)
    row = pl.BlockSpec((1, tt, d), lambda bi, i: (bi, i, 0))
    per_batch = pl.BlockSpec((1, 1, d), lambda bi, i: (bi, 0, 0))
    return pl.pallas_call(
        _norm_mod_kernel,
        grid=(b, t // tt),
        in_specs=[row, pl.BlockSpec((1, d), lambda bi, i: (0, 0)), per_batch, per_batch],
        out_specs=row,
        out_shape=jax.ShapeDtypeStruct((b, t, d), out_dtype),
        compiler_params=_params("arbitrary", "arbitrary"),
        name="norm_mod",
    )(x, g.reshape(1, d), sc, sh)


def _final_norm_kernel(x_ref, g_ref, o_ref):
    x = x_ref[0]
    o_ref[0] = (x * lax.rsqrt(jnp.mean(x * x, axis=-1, keepdims=True) + EPS)) * g_ref[...]


def _final_norm(x, g):
    b, t, d = x.shape
    tt = _tile(t, 256, 8)
    row = pl.BlockSpec((1, tt, d), lambda bi, i: (bi, i, 0))
    return pl.pallas_call(
        _final_norm_kernel,
        grid=(b, t // tt),
        in_specs=[row, pl.BlockSpec((1, d), lambda bi, i: (0, 0))],
        out_specs=row,
        out_shape=jax.ShapeDtypeStruct((b, t, d), F32),
        compiler_params=_params("arbitrary", "arbitrary"),
        name="final_norm",
    )(x, g.reshape(1, d))


def _norm_mix_kernel(x_ref, xp_ref, s0_ref, g_ref, sc_ref, sh_ref, mu_ref, *out_refs):
    i = pl.program_id(1)
    mix_refs, last_ref = out_refs[:6], out_refs[6]
    tt, d = x_ref.shape[1], x_ref.shape[2]
    x, xp = x_ref[0], xp_ref[0]
    inv = lax.rsqrt(jnp.mean(x * x, axis=-1, keepdims=True) + EPS)
    inv_p = lax.rsqrt(jnp.mean(xp * xp, axis=-1, keepdims=True) + EPS)
    cw = LANES if d % LANES == 0 else d
    row_id = lax.broadcasted_iota(jnp.int32, (tt, cw), 0)
    for c in range(d // cw):
        cols = slice(c * cw, (c + 1) * cw)
        g, sc, sh = g_ref[:, cols], sc_ref[0, :, cols], sh_ref[0, :, cols]
        h = ((x_ref[0, :, cols] * inv) * g) * (1.0 + sc) + sh
        hp = ((xp_ref[0, :, cols] * inv_p) * g) * (1.0 + sc) + sh
        prev_row = jnp.where(i == 0, s0_ref[0, :, cols], hp[7:8, :])
        shifted = jnp.where(row_id == 0, prev_row, pltpu.roll(h, 1, axis=0))
        xx = shifted - h
        for j in range(6):
            mix_refs[j][0, :, cols] = (h + xx * mu_ref[j:j + 1, cols]).astype(BF16)
        last_ref[0, :, cols] = h[tt - 8:, :]


def _norm_mix(x, shift0, g, sc, sh, mu):
    b, t, d = x.shape
    tt = _tile(t, 256, 16)
    row = pl.BlockSpec((1, tt, d), lambda bi, i: (bi, i, 0))
    prev8 = pl.BlockSpec((1, 8, d), lambda bi, i: (bi, jnp.maximum(i * (tt // 8) - 1, 0), 0))
    per_batch = pl.BlockSpec((1, 1, d), lambda bi, i: (bi, 0, 0))
    outs = pl.pallas_call(
        _norm_mix_kernel,
        grid=(b, t // tt),
        in_specs=[row, prev8, per_batch, pl.BlockSpec((1, d), lambda bi, i: (0, 0)), per_batch, per_batch,
                  pl.BlockSpec((6, d), lambda bi, i: (0, 0))],
        out_specs=[row] * 6 + [pl.BlockSpec((1, 8, d), lambda bi, i: (bi, 0, 0))],
        out_shape=[jax.ShapeDtypeStruct((b, t, d), BF16)] * 6 + [jax.ShapeDtypeStruct((b, 8, d), F32)],
        compiler_params=_params("arbitrary", "arbitrary"),
        name="norm_mix",
    )(x, x, shift0.reshape(b, 1, d), g.reshape(1, d), sc, sh, mu)
    return outs[:6], outs[6][:, 7, :]


def _mm_rows(tm):
    sub = MATMUL_ROW_SUBTILE if tm % MATMUL_ROW_SUBTILE == 0 else tm
    return sub, tm // sub


def _row_slice(m, sub):
    return pl.ds(m * sub, sub) if isinstance(m, int) else pl.ds(pl.multiple_of(m * sub, sub), sub)


def _for_row_subtiles(n_sub, body):
    if n_sub <= 2:
        for m in range(n_sub):
            body(m, 0)
    else:
        lax.fori_loop(0, n_sub, body, 0)


def _emit_bf16_weights(wb_refs, wq_refs):
    @pl.when(pl.program_id(0) == 0)
    def _():
        for wb_ref, wq_ref in zip(wb_refs, wq_refs):
            wq_ref[...] = wb_ref[...]


def _mm_kernel(a_ref, w_ref, o_ref, *rest, tm, w_transposed, emit):
    if w_ref.dtype == BF16:
        wb_ref = w_ref
    else:
        wb_ref = rest[-1]
        wb_ref[...] = w_ref[...].astype(BF16)
    sub, n_sub = _mm_rows(tm)

    def body(m, carry):
        rows = _row_slice(m, sub)
        if w_transposed:
            acc = lax.dot_general(a_ref[rows, :], wb_ref[...], (((1,), (1,)), ((), ())),
                                  preferred_element_type=F32)
        else:
            acc = jnp.dot(a_ref[rows, :], wb_ref[...], preferred_element_type=F32)
        o_ref[rows, :] = acc.astype(o_ref.dtype)
        return carry

    _for_row_subtiles(n_sub, body)
    if emit:
        _emit_bf16_weights([wb_ref], [rest[0]])


def _mm_resid_kernel(*refs, tm, parts):
    a_refs, w_refs = refs[:parts], refs[parts:2 * parts]
    r_ref, g_ref, o_ref = refs[2 * parts:2 * parts + 3]
    wb_refs = refs[2 * parts + 3:]
    if wb_refs:
        for w_ref, wb_ref in zip(w_refs, wb_refs):
            wb_ref[...] = w_ref[...].astype(BF16)
    else:
        wb_refs = w_refs
    sub, n_sub = _mm_rows(tm)
    per_row_gate = g_ref.shape[0] != 1

    def body(m, carry):
        rows = _row_slice(m, sub)
        acc = jnp.dot(a_refs[0][rows, :], wb_refs[0][...], preferred_element_type=F32)
        for a_ref, wb_ref in zip(a_refs[1:], wb_refs[1:]):
            acc = acc + jnp.dot(a_ref[rows, :], wb_ref[...], preferred_element_type=F32)
        gate = g_ref[rows, :] if per_row_gate else g_ref[...]
        o_ref[rows, :] = r_ref[rows, :] + gate * acc
        return carry

    _for_row_subtiles(n_sub, body)


def _mm_swiglu_kernel(a_ref, w1_ref, w3_ref, *rest, tm, emit, ada_tiles, cast_w2):
    rest = list(rest)
    if ada_tiles:
        c_ref, aw_ref, ab_ref = rest[:3]
        del rest[:3]
    if cast_w2:
        w2_ref = rest.pop(0)
    o_ref = rest.pop(0)
    if emit:
        wq_refs = rest[:2]
        del rest[:2]
        w1b_ref, w3b_ref = rest[-2:]
        w1b_ref[...] = w1_ref[...].astype(BF16)
        w3b_ref[...] = w3_ref[...].astype(BF16)
    else:
        w1b_ref, w3b_ref = w1_ref, w3_ref
    sub, n_sub = _mm_rows(tm)

    def body(m, carry):
        rows = _row_slice(m, sub)
        a = a_ref[rows, :]
        u = jnp.dot(a, w1b_ref[...], preferred_element_type=F32)
        v = jnp.dot(a, w3b_ref[...], preferred_element_type=F32)
        o_ref[rows, :] = (_silu(u) * v).astype(o_ref.dtype)
        return carry

    _for_row_subtiles(n_sub, body)
    if ada_tiles:
        mod_ref = rest.pop(0)
    if cast_w2:
        rest.pop(0)[...] = w2_ref[...].astype(BF16)
    if emit:
        _emit_bf16_weights([w1b_ref, w3b_ref], wq_refs)
    if ada_tiles:
        step = pl.program_id(0) * pl.num_programs(1) + pl.program_id(1)

        @pl.when(step < ada_tiles)
        def _():
            mod_ref[...] = _ada_tile(c_ref, aw_ref, ab_ref)


def _parked(n_tiles):
    return lambda i, j: jnp.where(i == 0, j, n_tiles - 1)


def _weight_spec(w, layer, k, tn):
    if w.ndim == 3:
        return pl.BlockSpec((None, k, tn), lambda i, j: (layer, 0, j))
    return pl.BlockSpec((k, tn), lambda i, j: (0, j))


def _matmul(a, w, *, layer=0, emit=False, tm_pref=2048, tn_pref=512):
    m, k = a.shape
    n = w.shape[-1]
    tm = _tile(m, tm_pref, MATMUL_ROW_SUBTILE)
    tn = _tile(n, tn_pref, LANES)
    park = _parked(n // tn)
    out_specs = [pl.BlockSpec((tm, tn), lambda i, j: (i, j))]
    out_shape = [jax.ShapeDtypeStruct((m, n), F32)]
    if emit:
        out_specs.append(pl.BlockSpec((k, tn), lambda i, j: (0, park(i, j))))
        out_shape.append(jax.ShapeDtypeStruct((k, n), BF16))
    outs = pl.pallas_call(
        functools.partial(_mm_kernel, tm=tm, w_transposed=False, emit=emit),
        grid=(m // tm, n // tn),
        in_specs=[_resident((tm, k), lambda i, j: (i, 0)), _weight_spec(w, layer, k, tn)],
        out_specs=out_specs,
        out_shape=out_shape,
        scratch_shapes=[pltpu.VMEM((k, tn), BF16)] if w.dtype != BF16 else [],
        compiler_params=_params("arbitrary", "arbitrary"),
        name="matmul",
    )(a, w)
    return tuple(outs) if emit else outs[0]


def _matmul_wt(a, wt, *, layer, row0, n, emit=False, tm_pref=2048, tn_pref=512):
    m, k = a.shape
    tm = _tile(m, tm_pref, MATMUL_ROW_SUBTILE)
    tn = _tile(n, tn_pref, LANES)
    assert row0 % tn == 0 and n % tn == 0
    park = _parked(n // tn)
    out_specs = [pl.BlockSpec((tm, tn), lambda i, j: (i, j))]
    out_shape = [jax.ShapeDtypeStruct((m, n), F32)]
    if emit:
        out_specs.append(pl.BlockSpec((tn, k), lambda i, j: (park(i, j), 0)))
        out_shape.append(jax.ShapeDtypeStruct((n, k), BF16))
    outs = pl.pallas_call(
        functools.partial(_mm_kernel, tm=tm, w_transposed=True, emit=emit),
        grid=(m // tm, n // tn),
        in_specs=[_resident((tm, k), lambda i, j: (i, 0)),
                  pl.BlockSpec((None, tn, k), lambda i, j: (layer, row0 // tn + j, 0))],
        out_specs=out_specs,
        out_shape=out_shape,
        scratch_shapes=[pltpu.VMEM((tn, k), BF16)] if wt.dtype != BF16 else [],
        compiler_params=_params("arbitrary", "arbitrary"),
        name="matmul_wt",
    )(a, wt)
    return tuple(outs) if emit else outs[0]


def _matmul_resid(a_parts, w, res, gate, *, layer=0, tm_pref=2048, tn_pref=512):
    parts = len(a_parts)
    m = a_parts[0].shape[0]
    ks = [a.shape[1] for a in a_parts]
    offs = [sum(ks[:p]) for p in range(parts)]
    n = w.shape[-1]
    assert w.shape[-2] == sum(ks) and all(off % k == 0 for off, k in zip(offs, ks))
    tm = _tile(m, tm_pref, MATMUL_ROW_SUBTILE)
    tn = _tile(n, tn_pref, LANES)
    if gate.shape[0] == 1:
        gate_spec = pl.BlockSpec((1, tn), lambda i, j: (0, j))
    else:
        gate_spec = pl.BlockSpec((tm, tn), lambda i, j: (i, j))
    if w.ndim == 3:
        w_specs = [pl.BlockSpec((None, k, tn), functools.partial(lambda i, j, blk: (layer, blk, j), blk=off // k))
                   for off, k in zip(offs, ks)]
    else:
        w_specs = [pl.BlockSpec((k, tn), functools.partial(lambda i, j, blk: (blk, j), blk=off // k))
                   for off, k in zip(offs, ks)]
    staged = w.dtype != BF16
    return pl.pallas_call(
        functools.partial(_mm_resid_kernel, tm=tm, parts=parts),
        grid=(m // tm, n // tn),
        in_specs=[_resident((tm, k), lambda i, j: (i, 0)) for k in ks] + w_specs
                 + [pl.BlockSpec((tm, tn), lambda i, j: (i, j)), gate_spec],
        out_specs=pl.BlockSpec((tm, tn), lambda i, j: (i, j)),
        out_shape=jax.ShapeDtypeStruct((m, n), F32),
        scratch_shapes=[pltpu.VMEM((k, tn), BF16) for k in ks] if staged else [],
        compiler_params=_params("arbitrary", "arbitrary"),
        name="matmul_resid",
    )(*a_parts, *([w] * parts), res, gate)


def _matmul_swiglu(a, w1, w3, *, layer=0, emit=False, ada=None, w2=None, tm_pref=2048, tn_pref=256):
    m, k = a.shape
    n = w1.shape[-1]
    assert emit == (w1.dtype != BF16)
    tm = _tile(m, tm_pref, MATMUL_ROW_SUBTILE)
    tn = _tile(n, tn_pref, LANES)
    n_i, n_j = m // tm, n // tn
    steps = n_i * n_j
    if ada is not None:
        c_all, ada_w, ada_b, next_layer = ada
        rows, d = c_all.shape
        n_mod = ada_w.shape[-1]
        if n_mod % ADA_TILE != 0 or n_mod // ADA_TILE > steps:
            outs = _matmul_swiglu(a, w1, w3, layer=layer, emit=emit, w2=w2, tm_pref=tm_pref, tn_pref=tn_pref)
            outs = list(outs) if isinstance(outs, tuple) else [outs]
            outs.insert(3 if emit else 1, _ada_mod(c_all, ada_w, ada_b, next_layer))
            return tuple(outs)
    if w2 is not None and (w2.shape[1] % steps != 0 or (w2.shape[1] // steps) % 16 != 0):
        outs = _matmul_swiglu(a, w1, w3, layer=layer, emit=emit, ada=ada, tm_pref=tm_pref, tn_pref=tn_pref)
        outs = outs if isinstance(outs, tuple) else (outs,)
        return outs + (w2[layer].astype(BF16),)
    step_of = lambda i, j: i * n_j + j
    in_specs = [_resident((tm, k), lambda i, j: (i, 0)), _weight_spec(w1, layer, k, tn),
                _weight_spec(w3, layer, k, tn)]
    args = [a, w1, w3]
    out_specs = [pl.BlockSpec((tm, tn), lambda i, j: (i, j))]
    out_shape = [jax.ShapeDtypeStruct((m, n), BF16)]
    if emit:
        park = _parked(n_j)
        out_specs += [pl.BlockSpec((k, tn), lambda i, j: (0, park(i, j)))] * 2
        out_shape += [jax.ShapeDtypeStruct((k, n), BF16)] * 2
    ada_tiles = 0
    if ada is not None:
        ada_tiles = n_mod // ADA_TILE
        tile_of = lambda i, j: jnp.minimum(step_of(i, j), ada_tiles - 1)
        in_specs += [pl.BlockSpec((rows, d), lambda i, j: (0, 0)),
                     pl.BlockSpec((None, d, ADA_TILE), lambda i, j: (next_layer, 0, tile_of(i, j))),
                     pl.BlockSpec((None, 1, ADA_TILE), lambda i, j: (next_layer, 0, tile_of(i, j)))]
        args += [c_all, ada_w, ada_b]
        out_specs.append(pl.BlockSpec((rows, ADA_TILE), lambda i, j: (0, tile_of(i, j))))
        out_shape.append(jax.ShapeDtypeStruct((rows, n_mod), F32))
    if w2 is not None:
        k2, n2 = w2.shape[1:]
        rows2 = k2 // steps
        in_specs.append(pl.BlockSpec((None, rows2, n2), lambda i, j: (layer, step_of(i, j), 0)))
        args.append(w2)
        out_specs.append(pl.BlockSpec((rows2, n2), lambda i, j: (step_of(i, j), 0)))
        out_shape.append(jax.ShapeDtypeStruct((k2, n2), BF16))
    outs = pl.pallas_call(
        functools.partial(_mm_swiglu_kernel, tm=tm, emit=emit, ada_tiles=ada_tiles, cast_w2=w2 is not None),
        grid=(n_i, n_j),
        in_specs=in_specs,
        out_specs=out_specs,
        out_shape=out_shape,
        scratch_shapes=[pltpu.VMEM((k, tn), BF16)] * 2 if emit else [],
        compiler_params=_params("arbitrary", "arbitrary"),
        name="matmul_swiglu",
    )(*args)
    return tuple(outs) if len(outs) > 1 else outs[0]


def _lora_kernel(a_ref, w1_ref, w2_ref, *rest, mid, epilogue):
    o_ref = rest[-1]
    z = jnp.dot(a_ref[...], w1_ref[...], preferred_element_type=F32)
    if mid == "tanh":
        z = jnp.tanh(z)
    elif mid == "sigmoid":
        z = jax.nn.sigmoid(z)
    z = jnp.dot(z.astype(BF16), w2_ref[...], preferred_element_type=F32)
    if epilogue == "log_decay":
        o_ref[...] = (-float(np.exp(-0.5))) * jax.nn.sigmoid(rest[0][...] + z)
    elif epilogue == "sigmoid":
        o_ref[...] = jax.nn.sigmoid(rest[0][...] + z)
    elif epilogue == "vmix":
        v, vf = rest[1][...], rest[2][...]
        o_ref[...] = v + (vf - v) * jax.nn.sigmoid(rest[0][...] + z)
    else:
        o_ref[...] = z


def _lora(a, w1, w2, *, mid, epilogue, bias=None, extra=()):
    m, d = a.shape
    r = w1.shape[1]
    rp = -(-r // LANES) * LANES
    w1p = jnp.pad(w1, ((0, 0), (0, rp - r))).astype(BF16)
    w2p = jnp.pad(w2, ((0, rp - r), (0, 0))).astype(BF16)
    n = w2.shape[1]
    tm = _tile(m, 256, 16)
    row = pl.BlockSpec((tm, n), lambda i: (i, 0))
    in_specs = [pl.BlockSpec((tm, d), lambda i: (i, 0)), _resident((d, rp), lambda i: (0, 0)),
                _resident((rp, n), lambda i: (0, 0))]
    args = [a, w1p, w2p]
    if bias is not None:
        in_specs.append(pl.BlockSpec((1, n), lambda i: (0, 0)))
        args.append(bias.reshape(1, n))
    for e in extra:
        in_specs.append(row)
        args.append(e)
    return pl.pallas_call(
        functools.partial(_lora_kernel, mid=mid, epilogue=epilogue),
        grid=(m // tm,),
        in_specs=in_specs,
        out_specs=row,
        out_shape=jax.ShapeDtypeStruct((m, n), F32),
        compiler_params=_params("arbitrary"),
        name="lora_" + epilogue,
    )(*args)


def _dot_nt(a, b):
    return lax.dot_general(a, b, (((1,), (1,)), ((), ())), preferred_element_type=F32)


def _dot_tn(a, b):
    return lax.dot_general(a, b, (((0,), (0,)), ((), ())), preferred_element_type=F32)


MLSTM_CHUNK = 256


def _soft_cap(x):
    return GATE_CAP * jnp.tanh(x / GATE_CAP)


def _mlstm_kernel(q_ref, k_ref, v_ref, o_ref, gc_ref, gr_ref, bc_ref, br_ref, ng_ref, c0_ref, n0_ref, m0_ref,
                  h_ref, c_ref, n_ref, m_ref, *, heads, dk, dv, length):
    c_idx = pl.program_id(1)

    @pl.when(c_idx == 0)
    def _():
        c_ref[...] = c0_ref[...]
        n_ref[...] = n0_ref[...]
        m_ref[...] = m0_ref[...]

    gcol = gc_ref[...] + bc_ref[...]
    grow = gr_ref[...] + br_ref[...]
    li_col, lf_col = _soft_cap(gcol[:, :heads]), jax.nn.log_sigmoid(_soft_cap(gcol[:, heads:]))
    li_row, lf_row = _soft_cap(grow[:heads, :]), jax.nn.log_sigmoid(_soft_cap(grow[heads:, :]))
    r_id = lax.broadcasted_iota(jnp.int32, (length, length), 0)
    c_id = lax.broadcasted_iota(jnp.int32, (length, length), 1)
    tril = c_id <= r_id
    scale = dk ** -0.5

    hs = range(heads)
    gates = []
    for h in hs:
        lf_r, li_r = lf_row[h:h + 1, :], li_row[h:h + 1, :]
        lf_c, li_c = lf_col[:, h:h + 1], li_col[:, h:h + 1]
        b_col = jnp.sum(jnp.where(tril, lf_r, 0.0), axis=1, keepdims=True)
        b_row = jnp.sum(jnp.where(r_id <= c_id, lf_c, 0.0), axis=0, keepdims=True)
        b_last = b_col[length - 1:length, :]
        m_prev = m_ref[0, h:h + 1, :]
        log_d = jnp.where(tril, b_col - b_row + li_r, -jnp.inf)
        inter = b_col + m_prev
        m_t = jnp.maximum(inter, jnp.max(log_d, axis=1, keepdims=True))
        m_new = m_t[length - 1:length, :]
        gates.append(dict(p=jnp.exp(log_d - m_t), g=jnp.exp(inter - m_t), m_t=m_t, m_new=m_new,
                          wk_col=jnp.exp(b_last - b_col + li_c - m_new),
                          decay=jnp.exp(b_last + m_prev - m_new)))

    q = [q_ref[:, h * dk:(h + 1) * dk] for h in hs]
    k = [k_ref[:, h * dk:(h + 1) * dk] * scale for h in hs]
    kw = [k[h] * gates[h]["wk_col"] for h in hs]
    qb = [x.astype(BF16) for x in q]
    kb = [x.astype(BF16) for x in k]
    vb = [v_ref[:, h * dv:(h + 1) * dv].astype(BF16) for h in hs]
    c_state = [c_ref[0, h] for h in hs]
    n_state = [n_ref[0, h:h + 1, :] for h in hs]
    s_qk = [_dot_nt(qb[h], kb[h]) for h in hs]
    q_c = [jnp.dot(qb[h], c_state[h].astype(BF16), preferred_element_type=F32) for h in hs]
    kv = [_dot_tn(kw[h].astype(BF16), vb[h]) for h in hs]
    wqk = [gates[h]["p"] * s_qk[h] for h in hs]
    num = [jnp.dot(wqk[h].astype(BF16), vb[h], preferred_element_type=F32) + gates[h]["g"] * q_c[h] for h in hs]

    for h in hs:
        g = gates[h]
        den = jnp.sum(wqk[h], axis=1, keepdims=True) + g["g"] * jnp.sum(q[h] * n_state[h], axis=1, keepdims=True)
        hh = num[h] / jnp.maximum(jnp.abs(den), jnp.exp(-g["m_t"]))
        hh = hh * lax.rsqrt(jnp.mean(hh * hh, axis=1, keepdims=True) + EPS)
        hh = hh * ng_ref[:, h * dv:(h + 1) * dv]
        h_ref[:, h * dv:(h + 1) * dv] = (jax.nn.sigmoid(o_ref[:, h * dv:(h + 1) * dv]) * hh).astype(h_ref.dtype)
        c_ref[0, h] = g["decay"] * c_state[h] + kv[h]
        n_ref[0, h:h + 1, :] = g["decay"] * n_state[h] + jnp.sum(kw[h], axis=0, keepdims=True)
        m_ref[0, h:h + 1, :] = g["m_new"]


def _mlstm(proj, gif, b_if, a_norm_g, c0, n0, m0, *, batch, seq):
    _, heads, dk, dv = c0.shape
    length = _tile(seq, MLSTM_CHUNK, CHUNK)
    nc = seq // length
    wq, wv = heads * dk, heads * dv
    assert wq == wv
    gcol = gif
    grow = gif.reshape(batch, nc, length, 2 * heads).transpose(0, 1, 3, 2)
    row = lambda col: pl.BlockSpec((length, wq), lambda b, c: (b * nc + c, col))
    state4 = pl.BlockSpec((1, heads, dk, dv), lambda b, c: (b, 0, 0, 0))
    state3 = pl.BlockSpec((1, heads, dk), lambda b, c: (b, 0, 0))
    state_m = pl.BlockSpec((1, heads, 1), lambda b, c: (b, 0, 0))
    h, c_out, n_out, m_out = pl.pallas_call(
        functools.partial(_mlstm_kernel, heads=heads, dk=dk, dv=dv, length=length),
        grid=(batch, nc),
        in_specs=[row(0), row(1), row(2), row(3),
                  pl.BlockSpec((length, 2 * heads), lambda b, c: (b * nc + c, 0)),
                  pl.BlockSpec((None, None, 2 * heads, length), lambda b, c: (b, c, 0, 0)),
                  pl.BlockSpec((1, 2 * heads), lambda b, c: (0, 0)),
                  pl.BlockSpec((2 * heads, 1), lambda b, c: (0, 0)),
                  pl.BlockSpec((1, wv), lambda b, c: (0, 0)),
                  state4, state3, state_m],
        out_specs=[pl.BlockSpec((length, wv), lambda b, c: (b * nc + c, 0)), state4, state3, state_m],
        out_shape=[jax.ShapeDtypeStruct((batch * seq, wv), BF16),
                   jax.ShapeDtypeStruct((batch, heads, dk, dv), F32),
                   jax.ShapeDtypeStruct((batch, heads, dk), F32),
                   jax.ShapeDtypeStruct((batch, heads, 1), F32)],
        compiler_params=_params("arbitrary", "arbitrary"),
        name="mlstm",
    )(proj, proj, proj, proj, gcol, grow, b_if.reshape(1, 2 * heads), b_if.reshape(2 * heads, 1),
      a_norm_g.reshape(1, wv), c0, n0, m0.reshape(batch, heads, 1))
    return h, c_out, n_out, m_out.reshape(batch, heads)


def _rel_bias_table(rel_bias):
    n_e, heads, rel_size = rel_bias.shape
    band = BAND_PAST + CHUNK
    i = np.arange(CHUNK)[:, None]
    j = np.arange(band)[None, :]
    rel = (np.clip(i - j + BAND_PAST, -(CHUNK - 1), REL_MAX) + (CHUNK - 1)).reshape(-1)
    onehot = (jnp.asarray(rel, jnp.int32)[None, :] == jnp.arange(rel_size, dtype=jnp.int32)[:, None]).astype(F32)
    table = jnp.dot(rel_bias.reshape(n_e * heads, rel_size), onehot, precision=lax.Precision.HIGHEST)
    return table.reshape(n_e, heads, CHUNK, band)


def _band_chunks(qs, ks, vs, biases, scale):
    n = range(len(qs))
    s = [_dot_nt(qs[i], ks[i]) * scale + biases[i] for i in n]
    p = [jnp.exp(s[i] - jnp.max(s[i], axis=1, keepdims=True)) for i in n]
    o = [jnp.dot(p[i].astype(BF16), vs[i], preferred_element_type=F32) for i in n]
    return [o[i] / jnp.sum(p[i], axis=1, keepdims=True) for i in n]


ATTN_CHUNKS_PER_ITER = 12


def _attn_prompt_kernel(q_ref, k_ref, v_ref, bias_ref, o_ref, kb_ref, vb_ref, *, seq, scale):
    kb_ref[...] = k_ref[...].astype(BF16)
    vb_ref[...] = v_ref[...].astype(BF16)
    nc = seq // CHUNK
    band = BAND_PAST + CHUNK
    bias = bias_ref[0]

    lead = list(range(min(N_PREV_CHUNKS, nc)))
    for c0 in range(0, len(lead), ATTN_CHUNKS_PER_ITER):
        cs = lead[c0:c0 + ATTN_CHUNKS_PER_ITER]
        widths = [(c + 1) * CHUNK for c in cs]
        outs = _band_chunks([q_ref[c * CHUNK:(c + 1) * CHUNK, :].astype(BF16) for c in cs],
                            [kb_ref[0:w, :] for w in widths], [vb_ref[0:w, :] for w in widths],
                            [bias[:, band - w:] for w in widths], scale)
        for c, o in zip(cs, outs):
            o_ref[c * CHUNK:(c + 1) * CHUNK, :] = o.astype(o_ref.dtype)

    rest = nc - N_PREV_CHUNKS
    per_iter = ATTN_CHUNKS_PER_ITER if rest % ATTN_CHUNKS_PER_ITER == 0 else 1

    def body(it, carry):
        starts = [pl.multiple_of((N_PREV_CHUNKS + it * per_iter + u) * CHUNK, CHUNK) for u in range(per_iter)]
        k_rows = [pl.ds(pl.multiple_of(s - BAND_PAST, CHUNK), band) for s in starts]
        outs = _band_chunks([q_ref[pl.ds(s, CHUNK), :].astype(BF16) for s in starts],
                            [kb_ref[r, :] for r in k_rows], [vb_ref[r, :] for r in k_rows],
                            [bias] * per_iter, scale)
        for s, o in zip(starts, outs):
            o_ref[pl.ds(s, CHUNK), :] = o.astype(o_ref.dtype)
        return carry

    if rest > 0:
        lax.fori_loop(0, rest // per_iter, body, 0)


def _attn_prompt(proj_b, bias_table, *, batch, seq, heads, dh):
    col = lambda base: pl.BlockSpec((seq, dh), lambda b, h: (b, base * heads + h))
    return pl.pallas_call(
        functools.partial(_attn_prompt_kernel, seq=seq, scale=dh ** -0.5),
        grid=(batch, heads),
        in_specs=[col(0), col(1), col(2),
                  pl.BlockSpec((1, CHUNK, BAND_PAST + CHUNK), lambda b, h: (h, 0, 0))],
        out_specs=pl.BlockSpec((seq, dh), lambda b, h: (b, h)),
        out_shape=jax.ShapeDtypeStruct((batch * seq, heads * dh), BF16),
        scratch_shapes=[pltpu.VMEM((seq, dh), BF16), pltpu.VMEM((seq, dh), BF16)],
        compiler_params=_params("arbitrary", "arbitrary"),
        name="attn_prompt",
    )(proj_b, proj_b, proj_b, bias_table)


def _attn_sample_kernel(q_ref, k_ref, v_ref, pk_ref, pv_ref, bias_ref, o_ref, *, seq, width, heads, dh, scale):
    off = BAND_PAST - width
    hs = range(heads)
    cols = [slice(h * dh, (h + 1) * dh) for h in hs]
    rows = [pl.ds(h, width, stride=heads) for h in hs]
    q = [q_ref[:, c].astype(BF16) for c in cols]
    s_past = [_dot_nt(q[h], pk_ref[rows[h], :].astype(BF16)) * scale + bias_ref[h, :seq, off:off + width] for h in hs]
    s_new = [_dot_nt(q[h], k_ref[:, cols[h]].astype(BF16)) * scale + bias_ref[h, :seq, BAND_PAST:BAND_PAST + seq]
             for h in hs]
    m = [jnp.maximum(jnp.max(s_past[h], axis=1, keepdims=True), jnp.max(s_new[h], axis=1, keepdims=True)) for h in hs]
    p_past = [jnp.exp(s_past[h] - m[h]) for h in hs]
    p_new = [jnp.exp(s_new[h] - m[h]) for h in hs]
    o = [jnp.dot(p_past[h].astype(BF16), pv_ref[rows[h], :].astype(BF16), preferred_element_type=F32)
         + jnp.dot(p_new[h].astype(BF16), v_ref[:, cols[h]].astype(BF16), preferred_element_type=F32) for h in hs]
    for h in hs:
        denom = jnp.sum(p_past[h], axis=1, keepdims=True) + jnp.sum(p_new[h], axis=1, keepdims=True)
        o_ref[:, cols[h]] = (o[h] / denom).astype(o_ref.dtype)


def _attn_sample(proj_b, past_k, past_v, layer, bias_table, *, batch, seq, heads, dh):
    width = past_k.shape[2] // heads
    wb = heads * dh
    col = lambda base: pl.BlockSpec((seq, wb), lambda b: (b, base))
    past = pl.BlockSpec((None, None, width * heads, dh), lambda b: (layer, b, 0, 0))
    return pl.pallas_call(
        functools.partial(_attn_sample_kernel, seq=seq, width=width, heads=heads, dh=dh, scale=dh ** -0.5),
        grid=(batch,),
        in_specs=[col(0), col(1), col(2), past, past,
                  pl.BlockSpec((heads, CHUNK, BAND_PAST + CHUNK), lambda b: (0, 0, 0))],
        out_specs=pl.BlockSpec((seq, wb), lambda b: (b, 0)),
        out_shape=jax.ShapeDtypeStruct((batch * seq, wb), BF16),
        compiler_params=_params("arbitrary"),
        name="attn_sample",
    )(proj_b, proj_b, proj_b, past_k, past_v, bias_table)


def _split_dot(a_exact, x):
    hi = x.astype(BF16)
    lo = (x - hi.astype(F32)).astype(BF16)
    return jnp.dot(a_exact, hi, preferred_element_type=F32) + jnp.dot(a_exact, lo, preferred_element_type=F32)


def _block_diag_rows(x, lo_mask):
    return jnp.concatenate([jnp.where(lo_mask, x, 0.0), jnp.where(lo_mask, 0.0, x)], axis=0).astype(BF16)


def _rwkv_kernel(r_ref, lw_ref, k_ref, v_ref, a_ref, g_ref, kk_ref, ka_ref, rk_ref, lg_ref, lb_ref, s0_ref,
                 o_ref, s_ref, s2_ref, *, pairs, dh, length, n_chunks):
    c_idx = pl.program_id(2)
    pw_ = 2 * dh
    ps = range(pairs)

    @pl.when(c_idx == 0)
    def _():
        for p in ps:
            s2_ref[p] = jnp.concatenate([s0_ref[0, 2 * p], s0_ref[0, 2 * p + 1]], axis=1)

    def lane_lo(shape, half):
        return lax.broadcasted_iota(jnp.int32, shape, 1) < half

    f_lo = lane_lo((length, pw_), dh)
    f_lo_s = lane_lo((dh, pw_), dh)
    t_lo = lane_lo((length, 2 * length), length)
    r_id = lax.broadcasted_iota(jnp.int32, (length, length), 0)
    c_id = lax.broadcasted_iota(jnp.int32, (length, length), 1)
    tril_bf = jnp.where(c_id <= r_id, 1.0, 0.0).astype(BF16)
    row2 = lax.broadcasted_iota(jnp.int32, (length, 2 * length), 0)
    col2 = lax.broadcasted_iota(jnp.int32, (length, 2 * length), 1) & (length - 1)
    strict2 = col2 < row2
    row4 = lax.broadcasted_iota(jnp.int32, (length, 4 * length), 0)
    col4 = lax.broadcasted_iota(jnp.int32, (length, 4 * length), 1) & (length - 1)
    incl4 = col4 <= row4
    n_double = max(int(np.ceil(np.log2(length))), 1)

    def head_sums(x, lo_mask):
        s_lo = jnp.sum(jnp.where(lo_mask, x, 0.0), axis=1, keepdims=True)
        s_hi = jnp.sum(jnp.where(lo_mask, 0.0, x), axis=1, keepdims=True)
        return jnp.where(lo_mask, s_lo, s_hi)

    rows = r_ref.shape[0]

    def load(ref):
        x = ref[...]
        return x if rows == length else jnp.concatenate([x, jnp.zeros((length - rows, x.shape[1]), F32)], axis=0)

    r_all, lw_all, k_all, v_all, a_all = load(r_ref), load(lw_ref), load(k_ref), load(v_ref), load(a_ref)
    cum = _split_dot(tril_bf, lw_all)
    w_in = jnp.exp(cum)
    w_inv = jnp.exp(-cum)
    w_ex = jnp.exp(cum - lw_all)
    kk_all = k_all * kk_ref[...]
    k2_all = k_all * (1.0 + (a_all - 1.0) * ka_ref[...])
    rt_all = r_all * w_in
    kt_all = k2_all * w_inv
    ba_all = a_all * w_inv
    rk2_all = r_all * k2_all * rk_ref[...]
    sl = lambda x, p: x[:, p * pw_:(p + 1) * pw_]

    lhs, rhs, rhs_bd = [], [], []
    for p in ps:
        kk = sl(kk_all, p)
        kk = kk / jnp.maximum(jnp.sqrt(head_sums(kk * kk, f_lo)), 1e-12)
        at = (-kk) * sl(w_ex, p)
        bt = kk * sl(ba_all, p)
        kt = sl(kt_all, p)
        lhs.append(jnp.concatenate([at, sl(rt_all, p)], axis=0).astype(BF16))
        rhs.append(jnp.concatenate([bt, kt], axis=0).astype(BF16))
        rhs_bd.append(jnp.concatenate([_block_diag_rows(bt, f_lo), _block_diag_rows(kt, f_lo)], axis=0))
    s0 = [s2_ref[p] for p in ps]
    vf = [sl(v_all, p) for p in ps]
    v_bd = [_block_diag_rows(vf[p], f_lo) for p in ps]
    aals = [_dot_nt(lhs[p], jnp.concatenate([rhs_bd[p], _block_diag_rows(s0[p], f_lo_s)], axis=0)) for p in ps]
    aa = [z[:, :4 * length] for z in aals]
    ls = [z[:, 4 * length:] for z in aals]
    x = [ls[p][:length] + jnp.dot(jnp.where(strict2, aa[p][:length, 2 * length:], 0.0).astype(BF16), v_bd[p],
                                  preferred_element_type=F32) for p in ps]
    pw = [jnp.where(strict2, aa[p][:length, :2 * length], 0.0) for p in ps]
    for j in range(n_double):
        pwb = [q.astype(BF16) for q in pw]
        if j + 1 < n_double:
            both = [jnp.dot(pwb[p], jnp.concatenate([_block_diag_rows(x[p], f_lo), _block_diag_rows(pw[p], t_lo)],
                                                    axis=1), preferred_element_type=F32) for p in ps]
            x = [x[p] + both[p][:, :pw_] for p in ps]
            pw = [both[p][:, pw_:] for p in ps]
        else:
            x = [x[p] + jnp.dot(pwb[p], _block_diag_rows(x[p], f_lo), preferred_element_type=F32) for p in ps]
    uv_bd = [jnp.concatenate([_block_diag_rows(x[p], f_lo), v_bd[p]], axis=0) for p in ps]
    y = [ls[p][length:] + jnp.dot(jnp.where(incl4, aa[p][length:, :], 0.0).astype(BF16), uv_bd[p],
                                  preferred_element_type=F32) for p in ps]
    uv = [jnp.concatenate([x[p], vf[p]], axis=0).astype(BF16) for p in ps]
    ds = [_dot_tn(uv[p], rhs[p]) for p in ps]
    for p in ps:
        delta = jnp.where(f_lo_s, ds[p][:dh, :], ds[p][dh:, :])
        s2_ref[p] = (s0[p] + delta) * sl(w_in, p)[length - 1:length, :]

    inv_dh = 1.0 / dh
    for p in ps:
        cols = slice(p * pw_, (p + 1) * pw_)
        yc = y[p] - head_sums(y[p], f_lo) * inv_dh
        yn = yc * lax.rsqrt(head_sums(yc * yc, f_lo) * inv_dh + GN_EPS)
        yn = yn * lg_ref[:, cols] + lb_ref[:, cols]
        bonus = head_sums(sl(rk2_all, p), f_lo) * vf[p]
        o_ref[:, cols] = ((yn + bonus)[:rows] * g_ref[:, cols]).astype(o_ref.dtype)

    @pl.when(c_idx == n_chunks - 1)
    def _():
        for p in ps:
            s_ref[0, 2 * p] = s2_ref[p][:, :dh]
            s_ref[0, 2 * p + 1] = s2_ref[p][:, dh:]


def _rwkv(r, lw, k, v, a, g, k_k, k_a, r_k, lnx_g, lnx_b, s0, *, batch, seq):
    _, heads, dh, _ = s0.shape
    d = heads * dh
    rows = min(CHUNK, seq)
    length = CHUNK
    nc = seq // rows
    assert 2 * dh == LANES and heads % 2 == 0 and rows % 8 == 0
    hg = _tile(heads, 64, 2)
    wg = hg * dh
    row = pl.BlockSpec((rows, wg), lambda b, gi, c: (b * nc + c, gi))
    par = pl.BlockSpec((1, wg), lambda b, gi, c: (0, gi))
    state = pl.BlockSpec((1, hg, dh, dh), lambda b, gi, c: (b, gi, 0, 0))
    return pl.pallas_call(
        functools.partial(_rwkv_kernel, pairs=hg // 2, dh=dh, length=length, n_chunks=nc),
        grid=(batch, heads // hg, nc),
        in_specs=[row] * 6 + [par] * 5 + [state],
        out_specs=[row, state],
        out_shape=[jax.ShapeDtypeStruct((batch * seq, d), BF16), jax.ShapeDtypeStruct(s0.shape, F32)],
        scratch_shapes=[pltpu.VMEM((hg // 2, dh, 2 * dh), F32)],
        compiler_params=_params("arbitrary", "arbitrary", "arbitrary"),
        name="rwkv",
    )(r, lw, k, v, a, g, k_k.reshape(1, d), k_a.reshape(1, d), r_k.reshape(1, d), lnx_g.reshape(1, d),
      lnx_b.reshape(1, d), s0)


def _trunk(x, rows, ada, P, st, shared):
    emit = st is None
    c_all, ada_w, ada_b = ada
    wq = shared
    b, t, d = x.shape
    m = b * t
    depth = P["ada_w"].shape[0]
    _, heads_a, dk, dv = P["a_shape"]
    heads_b, dh_b = P["b_shape"]
    _, heads_c, dh_c, _ = P["c_shape"]
    w_a = heads_a * dk
    w_b = heads_b * dh_b
    a_C, a_n, a_m, b_k, b_v, c_S, c_sh = [], [], [], [], [], [], []
    v_first = None

    def gate_rows(gt):
        return gt if b == 1 else jnp.broadcast_to(gt[:, None, :], (b, t, d)).reshape(m, d)

    for l in range(depth):
        if emit and l == 0:
            shared["mod", 0] = _ada_mod(c_all, ada_w, ada_b, 0)
        sh1, sc1, gt1, sh2, sc2, gt2 = (z.reshape(b, 1, d) for z in jnp.split(shared["mod", l][rows], 6, axis=-1))
        if l % 2 == 0:
            e = l // 2
            h = _norm_mod(x, P["norm1_g"][l], sc1, sh1).reshape(m, d)
            if st is None:
                c0 = jnp.zeros((b, heads_a, dk, dv), F32)
                n0 = jnp.zeros((b, heads_a, dk), F32)
                m0 = jnp.zeros((b, heads_a), F32)
            else:
                c0, n0, m0 = st["a_C"][e], st["a_n"][e], st["a_m"][e]
            gif = _matmul_wt(h, P["ab_wt"], layer=e, row0=4 * w_a, n=2 * heads_a)
            if emit:
                proj_a, wq["a", e] = _matmul_wt(h, P["ab_wt"], layer=e, row0=0, n=4 * w_a, emit=True)
                proj_b, wq["b", e] = _matmul_wt(h, P["ab_wt_b"], layer=e, row0=0, n=3 * w_b, emit=True)
            else:
                proj_a = _matmul_wt(h, wq["a", e][None], layer=0, row0=0, n=4 * w_a)
                proj_b = _matmul_wt(h, wq["b", e][None], layer=0, row0=0, n=3 * w_b)
            ha, C, n, mm = _mlstm(proj_a, gif, P["ab_b_if"][e], P["a_norm_g"][e], c0, n0, m0, batch=b, seq=t)
            if st is None:
                hb = _attn_prompt(proj_b, P["bias_table"][e], batch=b, seq=t, heads=heads_b, dh=dh_b)
                keep = min(BAND_PAST, t)
            else:
                hb = _attn_sample(proj_b, st["b_k"], st["b_v"], e, P["bias_table"][e],
                                  batch=b, seq=t, heads=heads_b, dh=dh_b)
                keep = t
            kept = proj_b.reshape(b, t, 3 * w_b)[:, t - keep:, :]
            a_C.append(C)
            a_n.append(n)
            a_m.append(mm)
            b_k.append(kept[:, :, w_b:2 * w_b].reshape(b, keep, heads_b, dh_b))
            b_v.append(kept[:, :, 2 * w_b:].reshape(b, keep, heads_b, dh_b))
            x = _matmul_resid([ha, hb], P["ab_w_out"], x.reshape(m, d), gate_rows(gt1.reshape(b, d)), layer=e)
        else:
            o = l // 2
            if st is None:
                s0 = jnp.zeros((b, heads_c, dh_c, dh_c), F32)
                shift0 = jnp.zeros((b, d), F32)
            else:
                s0, shift0 = st["c_S"][o], st["c_shift"][o]
            (xr, xw, xk, xv, xa, xg), shift = _norm_mix(x, shift0, P["norm1_g"][l], sc1, sh1, P["c_mu"][o])
            flat = lambda z: z.reshape(m, d)
            if emit:
                r, wq["r", o] = _matmul(flat(xr), P["c_wr"], layer=o, emit=True)
                k, wq["k", o] = _matmul(flat(xk), P["c_wk"], layer=o, emit=True)
                v, wq["v", o] = _matmul(flat(xv), P["c_wv"], layer=o, emit=True)
            else:
                r, k, v = (_matmul(flat(z), wq[name, o]) for z, name in ((xr, "r"), (xk, "k"), (xv, "v")))
            lw = _lora(flat(xw), P["c_w1"][o], P["c_w2"][o], mid="tanh", epilogue="log_decay", bias=P["c_w0"][o])
            a = _lora(flat(xa), P["c_a1"][o], P["c_a2"][o], mid="none", epilogue="sigmoid", bias=P["c_a0"][o])
            g = _lora(flat(xg), P["c_g1"][o], P["c_g2"][o], mid="sigmoid", epilogue="none")
            if v_first is None:
                v_first = v
            else:
                v = _lora(flat(xv), P["c_v1"][o - 1], P["c_v2"][o - 1], mid="none", epilogue="vmix",
                          bias=P["c_v0"][o - 1], extra=(v, v_first))
            y, S = _rwkv(r, lw, k, v, a, g, P["c_k_k"][o], P["c_k_a"][o], P["c_r_k"][o], P["c_lnx_g"][o],
                         P["c_lnx_b"][o], s0, batch=b, seq=t)
            c_S.append(S)
            c_sh.append(shift)
            x = _matmul_resid([y], P["c_wo"], x.reshape(m, d), gate_rows(gt1.reshape(b, d)), layer=o)
        x = x.reshape(b, t, d)
        h = _norm_mod(x, P["norm2_g"][l], sc2, sh2).reshape(m, d)
        if emit:
            ada_next = (c_all, ada_w, ada_b, l + 1) if l + 1 < depth else None
            outs = _matmul_swiglu(h, P["ffn_w1"], P["ffn_w3"], layer=l, emit=True, ada=ada_next, w2=P["ffn_w2"])
            hid, wq["w1", l], wq["w3", l] = outs[:3]
            if ada_next is not None:
                shared["mod", l + 1] = outs[3]
            wq["w2", l] = outs[-1]
        else:
            hid = _matmul_swiglu(h, wq["w1", l], wq["w3", l])
        x = _matmul_resid([hid], wq["w2", l], x.reshape(m, d), gate_rows(gt2.reshape(b, d)),
                          tm_pref=1024, tn_pref=512).reshape(b, t, d)
    y = _final_norm(x, P["final_g"])
    return (y, jnp.stack(a_C), jnp.stack(a_n), jnp.stack(a_m), jnp.stack(b_k), jnp.stack(b_v),
            jnp.stack(c_S), jnp.stack(c_sh))


def kernel(x_prompt, x_sample, c_prompt, c_sample, state_a_C, state_a_n, state_a_m, cache_b_k, cache_b_v, state_c_S, state_c_shift, ada_w, ada_b, norm1_g, norm2_g, final_g, ab_w_in, ab_b_if, a_norm_g, b_rel_bias, ab_w_out, c_mu, c_wr, c_wk, c_wv, c_wo, c_w0, c_w1, c_w2, c_a0, c_a1, c_a2, c_v0, c_v1, c_v2, c_g1, c_g2, c_k_k, c_k_a, c_r_k, c_lnx_g, c_lnx_b, ffn_w1, ffn_w3, ffn_w2):
    n_ab, dec_b, heads_a, dk, dv = state_a_C.shape
    _, _, width, heads_b, dh_b = cache_b_k.shape
    w_a = heads_a * dk
    gate_lo = 2 * w_a + 2 * heads_a * dv
    gate_hi = gate_lo + 2 * heads_a
    assert gate_lo == 4 * w_a
    ab_wt = jnp.swapaxes(ab_w_in, 1, 2)
    P = dict(ada_w=ada_w, norm1_g=norm1_g, norm2_g=norm2_g, final_g=final_g,
             ab_wt=ab_wt, ab_wt_b=ab_wt[:, gate_hi:, :],
             ab_b_if=ab_b_if, a_norm_g=a_norm_g,
             bias_table=_rel_bias_table(b_rel_bias),
             ab_w_out=ab_w_out, c_mu=c_mu, c_wr=c_wr, c_wk=c_wk, c_wv=c_wv, c_wo=c_wo,
             c_w0=c_w0, c_w1=c_w1, c_w2=c_w2, c_a0=c_a0, c_a1=c_a1, c_a2=c_a2, c_v0=c_v0, c_v1=c_v1, c_v2=c_v2,
             c_g1=c_g1, c_g2=c_g2, c_k_k=c_k_k, c_k_a=c_k_a, c_r_k=c_r_k, c_lnx_g=c_lnx_g, c_lnx_b=c_lnx_b,
             ffn_w1=ffn_w1, ffn_w3=ffn_w3, ffn_w2=ffn_w2,
             a_shape=state_a_C.shape[1:], b_shape=(heads_b, dh_b), c_shape=state_c_S.shape[1:])
    st = dict(a_C=state_a_C, a_n=state_a_n, a_m=state_a_m,
              b_k=cache_b_k.reshape(n_ab, dec_b, width * heads_b, dh_b),
              b_v=cache_b_v.reshape(n_ab, dec_b, width * heads_b, dh_b),
              c_S=state_c_S, c_shift=state_c_shift)

    n_p, n_s = c_prompt.shape[0], c_sample.shape[0]
    rows = -(-(n_p + n_s) // 16) * 16
    c_all = jnp.concatenate([c_prompt, c_sample, jnp.zeros((rows - n_p - n_s, c_prompt.shape[1]), F32)], axis=0)
    ada = (c_all, ada_w, ada_b.reshape(ada_b.shape[0], 1, ada_b.shape[1]))

    shared = {}
    outs_p = _trunk(x_prompt, slice(0, n_p), ada, P, None, shared)
    outs_s = _trunk(x_sample, slice(n_p, n_p + n_s), ada, P, st, shared)
    return (outs_p[0], outs_s[0]) + tuple(outs_p[1:]) + tuple(outs_s[1:])
```
